```python
import jax, jax.numpy as jnp
from jax import lax
import numpy as np

D_MODEL = 1024
BATCH = 16
SEQ = 2048
DEPTH = 2

N_MIXERS = 2
EPS = 1e-6
LRU_WIDTH = 1280
LRU_BLOCKS = 10
LRU_BLOCK_W = LRU_WIDTH // LRU_BLOCKS
LRU_CONV = 4
LRU_C = 8.0
N_HEADS = 16
HEAD_DIM = 64
N_KV = 4
GROUP = N_HEADS // N_KV
N_BRANCH = 3
CMP_BLOCK = 32
CMP_STRIDE = 16
CMP_HIDDEN = 256
SEL_BLOCK = 64
N_SELECT = 8
WINDOW = 512
Q_BLOCK = 64
FORCED_BONUS = 1e4
NEG = -1e30
Q_COLS = N_HEADS * HEAD_DIM
KV_COLS = 2 * N_KV * HEAD_DIM
NSA_IN = Q_COLS + N_BRANCH * KV_COLS + N_BRANCH * N_HEADS
D_FF = 3072
FFN_CONV = 3

kernel_name = "hybrid_rglru_nsa_convffn"


def rmsnorm(x, g):
    xf = x.astype(jnp.float32)
    y = xf * lax.rsqrt(jnp.mean(xf * xf, axis=-1, keepdims=True) + EPS)
    return (y * g.astype(jnp.float32)).astype(x.dtype)


def causal_dwconv(x, w, b):
    K, C = w.shape
    y = lax.conv_general_dilated(x, w[:, None, :].astype(x.dtype), window_strides=(1,),
                                 padding=[(K - 1, 0)], dimension_numbers=('NWC', 'WIO', 'NWC'),
                                 feature_group_count=C)
    return y + b.astype(x.dtype)


def alibi_slopes():
    s = 2.0 ** (-8.0 * np.arange(1, N_HEADS + 1) / N_HEADS)
    return jnp.asarray(s, dtype=jnp.float32).reshape(N_KV, GROUP)


def masked_softmax(s, m):
    return jax.nn.softmax(jnp.where(m, s, NEG), axis=-1)


def rglru_mixer(x, w_in, conv_w, conv_b, gate_w, gate_b, a_param, w_out):
    B, S, _ = x.shape
    y_br, x_br = jnp.split(x @ w_in, 2, axis=-1)
    y_br = jax.nn.gelu(y_br)
    xc = causal_dwconv(x_br, conv_w, conv_b)
    xb = xc.reshape(B, S, LRU_BLOCKS, LRU_BLOCK_W)
    g = jnp.einsum('bsnc,gncd->gbsnd', xb, gate_w).reshape(2, B, S, LRU_WIDTH)
    g = jax.nn.sigmoid(g.astype(jnp.float32) + gate_b.astype(jnp.float32)[:, None, None, :])
    r_gate, i_gate = g[0], g[1]
    log_a = -LRU_C * r_gate * jax.nn.softplus(-a_param.astype(jnp.float32))
    a = jnp.exp(log_a)
    mult = jnp.sqrt(-jnp.expm1(2.0 * log_a))
    bterm = mult * i_gate * xc.astype(jnp.float32)

    def combine(left, right):
        a1, b1 = left
        a2, b2 = right
        return a1 * a2, a2 * b1 + b2

    _, h = lax.associative_scan(combine, (a, bterm), axis=1)
    return (h.astype(x.dtype) * y_br) @ w_out


def nsa_mixer(x, w_in, cmp_pos, cmp_w1, cmp_b1, cmp_w2, w_out):
    B, S, _ = x.shape
    dt = x.dtype
    f32 = jnp.float32
    proj = x @ w_in
    q = proj[..., :Q_COLS].reshape(B, S, N_KV, GROUP, HEAD_DIM).transpose(0, 2, 3, 1, 4)
    q = q * (HEAD_DIM ** -0.5)
    kv = proj[..., Q_COLS:Q_COLS + N_BRANCH * KV_COLS].reshape(B, S, N_BRANCH, 2, N_KV, HEAD_DIM)
    kv = kv.transpose(2, 3, 0, 4, 1, 5)
    gates = jax.nn.sigmoid(proj[..., Q_COLS + N_BRANCH * KV_COLS:].astype(f32))
    gates = gates.reshape(B, S, N_BRANCH, N_KV, GROUP).transpose(2, 0, 3, 4, 1)
    slopes = alibi_slopes()

    n_cmp = (S - CMP_BLOCK) // CMP_STRIDE + 1
    cmp_start = jnp.arange(n_cmp) * CMP_STRIDE
    tok_idx = cmp_start[:, None] + jnp.arange(CMP_BLOCK)[None, :]

    def compress(t, pos, w1, b1, w2):
        blk = (t[:, :, tok_idx] + pos.astype(dt)).reshape(B, N_KV, n_cmp, CMP_BLOCK * HEAD_DIM)
        return jax.nn.gelu(blk @ w1 + b1.astype(dt)) @ w2

    k_cmp = compress(kv[0, 0], cmp_pos[0], cmp_w1[0], cmp_b1[0], cmp_w2[0])
    v_cmp = compress(kv[0, 1], cmp_pos[1], cmp_w1[1], cmp_b1[1], cmp_w2[1])
    cmp_end = cmp_start + CMP_BLOCK - 1
    cmp_center = cmp_start.astype(f32) + (CMP_BLOCK - 1) / 2.0

    n_sel = S // SEL_BLOCK
    n_top = min(N_SELECT, n_sel)
    sel_j = jnp.arange(n_sel)
    overlap = ((cmp_start[:, None] < (sel_j[None, :] + 1) * SEL_BLOCK) &
               (cmp_start[:, None] + CMP_BLOCK > sel_j[None, :] * SEL_BLOCK)).astype(f32)
    k_blocks = kv[1, 0].reshape(B, N_KV, n_sel, SEL_BLOCK, HEAD_DIM)
    v_blocks = kv[1, 1].reshape(B, N_KV, n_sel, SEL_BLOCK, HEAD_DIM)
    gather = jax.vmap(jax.vmap(lambda blk, ix: blk[ix]))

    pad = ((0, 0), (0, 0), (WINDOW, 0), (0, 0))
    k_win = jnp.pad(kv[2, 0], pad)
    v_win = jnp.pad(kv[2, 1], pad)

    def query_block(q0):
        qc = lax.dynamic_slice_in_dim(q, q0, Q_BLOCK, axis=3)
        gc = lax.dynamic_slice_in_dim(gates, q0, Q_BLOCK, axis=4)
        t = q0 + jnp.arange(Q_BLOCK)
        tf = t.astype(f32)
        sl = slopes[:, :, None, None]

        m_c = cmp_end[None, :] <= t[:, None]
        s = jnp.einsum('bgrqd,bgcd->bgrqc', qc, k_cmp, preferred_element_type=f32)
        s = s - sl * (tf[:, None] - cmp_center[None, :])
        p_cmp = masked_softmax(s, m_c) * m_c
        o_cmp = jnp.einsum('bgrqc,bgcd->bgrqd', p_cmp.astype(dt), v_cmp)

        imp = jnp.einsum('bgrqc,cj->bgqj', p_cmp, overlap)
        cur = t // SEL_BLOCK
        forced = (sel_j[None, :] == 0) | (sel_j[None, :] == cur[:, None]) | (sel_j[None, :] == cur[:, None] - 1)
        future = sel_j[None, :] > cur[:, None]
        score = jnp.where(forced, FORCED_BONUS, jnp.where(future, -1.0, imp))
        _, idx = lax.top_k(score, n_top)
        kg = gather(k_blocks, idx).reshape(B, N_KV, Q_BLOCK, n_top * SEL_BLOCK, HEAD_DIM)
        vg = gather(v_blocks, idx).reshape(B, N_KV, Q_BLOCK, n_top * SEL_BLOCK, HEAD_DIM)
        pos = (idx[..., None] * SEL_BLOCK + jnp.arange(SEL_BLOCK)).reshape(B, N_KV, Q_BLOCK, n_top * SEL_BLOCK)
        dist = (t[:, None] - pos)[:, :, None]
        s = jnp.einsum('bgrqd,bgqkd->bgrqk', qc, kg, preferred_element_type=f32)
        s = s - sl * dist.astype(f32)
        p = masked_softmax(s, dist >= 0)
        o_sel = jnp.einsum('bgrqk,bgqkd->bgrqd', p.astype(dt), vg)

        kw = lax.dynamic_slice_in_dim(k_win, q0, WINDOW + Q_BLOCK, axis=2)
        vw = lax.dynamic_slice_in_dim(v_win, q0, WINDOW + Q_BLOCK, axis=2)
        pos_w = q0 - WINDOW + jnp.arange(WINDOW + Q_BLOCK)
        dist_w = t[:, None] - pos_w[None, :]
        m_w = (dist_w >= 0) & (dist_w < WINDOW) & (pos_w[None, :] >= 0)
        s = jnp.einsum('bgrqd,bgkd->bgrqk', qc, kw, preferred_element_type=f32)
        s = s - sl * dist_w.astype(f32)
        p = masked_softmax(s, m_w)
        o_win = jnp.einsum('bgrqk,bgkd->bgrqd', p.astype(dt), vw)

        o = (gc[0][..., None] * o_cmp.astype(f32) + gc[1][..., None] * o_sel.astype(f32)
             + gc[2][..., None] * o_win.astype(f32)).astype(dt)
        return o.transpose(0, 3, 1, 2, 4).reshape(B, Q_BLOCK, Q_COLS)

    starts = jnp.arange(S // Q_BLOCK) * Q_BLOCK
    o = lax.map(query_block, starts)
    o = o.transpose(1, 0, 2, 3).reshape(B, S, Q_COLS)
    return o @ w_out


def conv_ffn(x, w_in, conv_w, conv_b, w_out):
    a, b = jnp.split(x @ w_in, 2, axis=-1)
    a = causal_dwconv(a, conv_w, conv_b)
    return (jax.nn.gelu(a) * b) @ w_out


def setup_inputs(seed: int = 0) -> dict:
    key = jax.random.key(seed)
    ks = iter(jax.random.split(key, 32))
    n_a = len(range(0, DEPTH, N_MIXERS))
    n_b = len(range(1, DEPTH, N_MIXERS))

    def nrm(shape, scale):
        return jax.random.normal(next(ks), shape, jnp.float32) * scale

    def gain(shape):
        return 1.0 + nrm(shape, 0.05)

    x = nrm((BATCH, SEQ, D_MODEL), 1.0)
    lru_norm_g = gain((n_a, D_MODEL))
    lru_w_in = nrm((n_a, D_MODEL, 2 * LRU_WIDTH), D_MODEL ** -0.5)
    lru_conv_w = nrm((n_a, LRU_CONV, LRU_WIDTH), LRU_CONV ** -0.5)
    lru_conv_b = nrm((n_a, LRU_WIDTH), 0.02)
    lru_gate_w = nrm((n_a, 2, LRU_BLOCKS, LRU_BLOCK_W, LRU_BLOCK_W), LRU_BLOCK_W ** -0.5)
    lru_gate_b = nrm((n_a, 2, LRU_WIDTH), 0.1)
    a_c = jax.random.uniform(next(ks), (n_a, LRU_WIDTH), jnp.float32, 0.9, 0.999)
    s = a_c ** (1.0 / LRU_C)
    lru_a_param = jnp.log(s) - jnp.log1p(-s)
    lru_w_out = nrm((n_a, LRU_WIDTH, D_MODEL), LRU_WIDTH ** -0.5)
    nsa_norm_g = gain((n_b, D_MODEL))
    nsa_w_in = nrm((n_b, D_MODEL, NSA_IN), D_MODEL ** -0.5)
    nsa_cmp_pos = nrm((n_b, 2, CMP_BLOCK, HEAD_DIM), 0.1)
    nsa_cmp_w1 = nrm((n_b, 2, CMP_BLOCK * HEAD_DIM, CMP_HIDDEN), (CMP_BLOCK * HEAD_DIM) ** -0.5)
    nsa_cmp_b1 = nrm((n_b, 2, CMP_HIDDEN), 0.02)
    nsa_cmp_w2 = nrm((n_b, 2, CMP_HIDDEN, HEAD_DIM), CMP_HIDDEN ** -0.5)
    nsa_w_out = nrm((n_b, Q_COLS, D_MODEL), Q_COLS ** -0.5)
    ffn_norm_g = gain((DEPTH, D_MODEL))
    ffn_w_in = nrm((DEPTH, D_MODEL, 2 * D_FF), D_MODEL ** -0.5)
    ffn_conv_w = nrm((DEPTH, FFN_CONV, D_FF), FFN_CONV ** -0.5)
    ffn_conv_b = nrm((DEPTH, D_FF), 0.02)
    ffn_w_out = nrm((DEPTH, D_FF, D_MODEL), D_FF ** -0.5)
    final_norm_g = gain((D_MODEL,))
    return {"x": x, "lru_norm_g": lru_norm_g, "lru_w_in": lru_w_in, "lru_conv_w": lru_conv_w,
            "lru_conv_b": lru_conv_b, "lru_gate_w": lru_gate_w, "lru_gate_b": lru_gate_b,
            "lru_a_param": lru_a_param, "lru_w_out": lru_w_out, "nsa_norm_g": nsa_norm_g,
            "nsa_w_in": nsa_w_in, "nsa_cmp_pos": nsa_cmp_pos, "nsa_cmp_w1": nsa_cmp_w1,
            "nsa_cmp_b1": nsa_cmp_b1, "nsa_cmp_w2": nsa_cmp_w2, "nsa_w_out": nsa_w_out,
            "ffn_norm_g": ffn_norm_g, "ffn_w_in": ffn_w_in, "ffn_conv_w": ffn_conv_w,
            "ffn_conv_b": ffn_conv_b, "ffn_w_out": ffn_w_out, "final_norm_g": final_norm_g}


def reference(x, lru_norm_g, lru_w_in, lru_conv_w, lru_conv_b, lru_gate_w, lru_gate_b,
              lru_a_param, lru_w_out, nsa_norm_g, nsa_w_in, nsa_cmp_pos, nsa_cmp_w1,
              nsa_cmp_b1, nsa_cmp_w2, nsa_w_out, ffn_norm_g, ffn_w_in, ffn_conv_w,
              ffn_conv_b, ffn_w_out, final_norm_g):
    h = x
    for layer in range(DEPTH):
        mixer, j = layer % N_MIXERS, layer // N_MIXERS
        if mixer == 0:
            h = h + rglru_mixer(rmsnorm(h, lru_norm_g[j]), lru_w_in[j], lru_conv_w[j], lru_conv_b[j],
                                lru_gate_w[j], lru_gate_b[j], lru_a_param[j], lru_w_out[j])
        else:
            h = h + nsa_mixer(rmsnorm(h, nsa_norm_g[j]), nsa_w_in[j], nsa_cmp_pos[j], nsa_cmp_w1[j],
                              nsa_cmp_b1[j], nsa_cmp_w2[j], nsa_w_out[j])
        h = h + conv_ffn(rmsnorm(h, ffn_norm_g[layer]), ffn_w_in[layer], ffn_conv_w[layer],
                         ffn_conv_b[layer], ffn_w_out[layer])
    return rmsnorm(h, final_norm_g)
```

```python
import functools
import math

import numpy as np
import jax
import jax.numpy as jnp
from jax import lax
from jax.experimental import pallas as pl
from jax.experimental.pallas import tpu as pltpu

BF16 = jnp.bfloat16
F32 = jnp.float32

EPS = 1e-6
LRU_BLOCK_W = 128
LRU_CONV = 4
LRU_C = 8.0
N_HEADS = 16
HEAD_DIM = 64
N_KV = 4
GROUP = N_HEADS // N_KV
CMP_BLOCK = 32
CMP_STRIDE = 16
SEL_BLOCK = 64
N_SELECT = 8
WINDOW = 512
FORCED_BONUS = 1e4
NEG = -1e30
FFN_CONV = 3

LANES = 128
SUBLANES = 8
VMEM_LIMIT = 56 * 1024 * 1024


def _gelu(x):
    c = math.sqrt(2.0 / math.pi)
    return x * (0.5 * (1.0 + jnp.tanh(c * (x + 0.044715 * (x * x * x)))))


def _rmsnorm(x, g):
    return x * lax.rsqrt(jnp.mean(x * x, axis=-1, keepdims=True) + EPS) * g


def _dot(a, b):
    return jnp.dot(a, b, preferred_element_type=F32)


def _dot_nt(a, b):
    return lax.dot_general(a, b, (((1,), (1,)), ((), ())), preferred_element_type=F32)


def _params(sem):
    return pltpu.CompilerParams(dimension_semantics=sem, vmem_limit_bytes=VMEM_LIMIT)


SCAN_CHUNK = SUBLANES * SUBLANES


def _lru_kernel(x_ref, g_ref, wy_ref, wx_ref, cw_ref, cb_ref, gw_ref, gb_ref, ap_ref, wo_ref,
                o_ref, xb_scr, a_scr, b_scr, hc_scr, u_scr, *, tm, nblk):
    t = pl.program_id(1)

    @pl.when(t == 0)
    def _():
        xb_scr[0:SUBLANES, :] = jnp.zeros((SUBLANES, xb_scr.shape[1]), F32)
        hc_scr[...] = jnp.zeros(hc_scr.shape, F32)

    x = x_ref[...]
    xn = _rmsnorm(x, g_ref[...]).astype(BF16)
    y = _gelu(_dot(xn, wy_ref[...]))
    xb = _dot(xn, wx_ref[...])

    xb_scr[SUBLANES:SUBLANES + tm, :] = xb
    cw = cw_ref[...]
    xc = (cw[3:4, :] * xb + cw[2:3, :] * xb_scr[7:7 + tm, :] + cw[1:2, :] * xb_scr[6:6 + tm, :]
          + cw[0:1, :] * xb_scr[5:5 + tm, :] + cb_ref[...])
    xb_scr[0:SUBLANES, :] = xb_scr[tm:tm + SUBLANES, :]

    z = -ap_ref[...]
    c8 = -LRU_C * (jnp.maximum(z, 0.0) + jnp.log1p(jnp.exp(-jnp.abs(z))))
    gb = gb_ref[...]

    sub = lax.broadcasted_iota(jnp.int32, (SUBLANES, LANES), 0)
    for n in range(nblk):
        lo, hi = n * LANES, (n + 1) * LANES
        xcn = xc[:, lo:hi]
        xcb = xcn.astype(BF16)
        r = jax.nn.sigmoid(_dot(xcb, gw_ref[0, n]) + gb[0:1, lo:hi])
        ig = jax.nn.sigmoid(_dot(xcb, gw_ref[1, n]) + gb[1:2, lo:hi])
        log_a = c8[:, lo:hi] * r
        a = jnp.exp(log_a)
        mult = jnp.sqrt(-jnp.tanh(log_a) * (a * a + 1.0))
        a_scr[n] = a
        b_scr[n] = mult * ig * xcn

        carry = hc_scr[n]
        for c in range(tm // SCAN_CHUNK):
            base = c * SCAN_CHUNK
            acum, hloc = [], []
            for j in range(SUBLANES):
                aj = a_scr.at[n][pl.ds(base + j, SUBLANES, stride=SUBLANES), :]
                bj = b_scr.at[n][pl.ds(base + j, SUBLANES, stride=SUBLANES), :]
                if j == 0:
                    acum.append(aj)
                    hloc.append(bj)
                else:
                    hloc.append(aj * hloc[-1] + bj)
                    acum.append(aj * acum[-1])
            p, e = acum[-1], hloc[-1]
            for d in (1, 2, 4):
                keep = sub >= d
                psh = pltpu.roll(p, d, 0)
                esh = pltpu.roll(e, d, 0)
                e = jnp.where(keep, p * esh + e, e)
                p = jnp.where(keep, p * psh, p)
            hend = e + p * carry
            cin = jnp.where(sub == 0, carry, pltpu.roll(hend, 1, 0))
            for j in range(SUBLANES):
                b_scr.at[n][pl.ds(base + j, SUBLANES, stride=SUBLANES), :] = (
                    hloc[j] + acum[j] * cin)
            carry = jnp.broadcast_to(hend[SUBLANES - 1:SUBLANES, :], (SUBLANES, LANES))
        hc_scr[n] = carry
        u_scr[:, lo:hi] = (b_scr[n] * y[:, lo:hi]).astype(BF16)

    o_ref[...] = x + _dot(u_scr[...], wo_ref[...])


def _lru_layer(h, g, w_in, conv_w, conv_b, gate_w, gate_b, a_param, w_out, *, tm=256):
    B, S, D = h.shape
    W = w_out.shape[0]
    nblk = W // LANES
    assert gate_w.shape == (2, nblk, LANES, LANES) and S % tm == 0 and tm % SCAN_CHUNK == 0
    wy = w_in[:, :W].astype(BF16)
    wx = w_in[:, W:].astype(BF16)
    const = lambda shape: pl.BlockSpec(shape, lambda b, t: (0,) * len(shape))
    return pl.pallas_call(
        functools.partial(_lru_kernel, tm=tm, nblk=nblk),
        grid=(B, S // tm),
        in_specs=[
            pl.BlockSpec((None, tm, D), lambda b, t: (b, t, 0)),
            const((1, D)), const((D, W)), const((D, W)), const((LRU_CONV, W)), const((1, W)),
            const((2, nblk, LANES, LANES)), const((2, W)), const((1, W)), const((W, D)),
        ],
        out_specs=pl.BlockSpec((None, tm, D), lambda b, t: (b, t, 0)),
        out_shape=jax.ShapeDtypeStruct((B, S, D), F32),
        scratch_shapes=[
            pltpu.VMEM((tm + SUBLANES, W), F32),
            pltpu.VMEM((nblk, tm, LANES), F32),
            pltpu.VMEM((nblk, tm, LANES), F32),
            pltpu.VMEM((nblk, SUBLANES, LANES), F32),
            pltpu.VMEM((tm, W), BF16),
        ],
        compiler_params=_params(("arbitrary", "arbitrary")),
        name="lru_layer",
    )(h, g.reshape(1, D), wy, wx, conv_w, conv_b.reshape(1, W), gate_w.astype(BF16), gate_b,
      a_param.reshape(1, W), w_out.astype(BF16))


def _ffn_kernel(x_ref, g_ref, wa_ref, wb_ref, cw_ref, cb_ref, wo_ref, fg_ref, o_ref,
                xn_scr, acc_scr, a_scr, halo_scr, *, tm, final_norm):
    t = pl.program_id(1)
    f = pl.program_id(2)
    nf = pl.num_programs(2)

    @pl.when(f == 0)
    def _():
        xn_scr[...] = _rmsnorm(x_ref[...], g_ref[...]).astype(BF16)
        acc_scr[...] = jnp.zeros(acc_scr.shape, F32)

    @pl.when(t == 0)
    def _():
        halo_scr[f] = jnp.zeros(halo_scr.shape[1:], F32)

    xn = xn_scr[...]
    a = _dot(xn, wa_ref[...])
    b = _dot(xn, wb_ref[...])
    a_scr[0:SUBLANES, :] = halo_scr[f]
    a_scr[SUBLANES:SUBLANES + tm, :] = a
    cw = cw_ref[...]
    ac = (cw[2:3, :] * a + cw[1:2, :] * a_scr[7:7 + tm, :] + cw[0:1, :] * a_scr[6:6 + tm, :]
          + cb_ref[...])
    halo_scr[f] = a_scr[tm:tm + SUBLANES, :]
    hmid = (_gelu(ac) * b).astype(BF16)
    acc_scr[...] += _dot(hmid, wo_ref[...])

    @pl.when(f == nf - 1)
    def _():
        out = x_ref[...] + acc_scr[...]
        if final_norm:
            out = _rmsnorm(out, fg_ref[...])
        o_ref[...] = out


def _ffn_layer(h, g, w_in, conv_w, conv_b, w_out, final_g, *, final_norm, tm=512, tf=512):
    B, S, D = h.shape
    F = w_out.shape[0]
    assert S % tm == 0 and F % tf == 0
    nf = F // tf
    wa = w_in[:, :F].astype(BF16)
    wb = w_in[:, F:].astype(BF16)
    return pl.pallas_call(
        functools.partial(_ffn_kernel, tm=tm, final_norm=final_norm),
        grid=(B, S // tm, nf),
        in_specs=[
            pl.BlockSpec((None, tm, D), lambda b, t, f: (b, t, 0)),
            pl.BlockSpec((1, D), lambda b, t, f: (0, 0)),
            pl.BlockSpec((D, tf), lambda b, t, f: (0, f)),
            pl.BlockSpec((D, tf), lambda b, t, f: (0, f)),
            pl.BlockSpec((FFN_CONV, tf), lambda b, t, f: (0, f)),
            pl.BlockSpec((1, tf), lambda b, t, f: (0, f)),
            pl.BlockSpec((tf, D), lambda b, t, f: (f, 0)),
            pl.BlockSpec((1, D), lambda b, t, f: (0, 0)),
        ],
        out_specs=pl.BlockSpec((None, tm, D), lambda b, t, f: (b, t, 0)),
        out_shape=jax.ShapeDtypeStruct((B, S, D), F32),
        scratch_shapes=[
            pltpu.VMEM((tm, D), BF16),
            pltpu.VMEM((tm, D), F32),
            pltpu.VMEM((tm + SUBLANES, tf), F32),
            pltpu.VMEM((nf, SUBLANES, tf), F32),
        ],
        compiler_params=_params(("arbitrary", "arbitrary", "arbitrary")),
        name="ffn_layer",
    )(h, g.reshape(1, D), wa, wb, conv_w, conv_b.reshape(1, F), w_out.astype(BF16),
      final_g.reshape(1, D))


def _nsa_proj_kernel(x_ref, g_ref, wq_ref, wkv_ref, wg_ref, q_ref, kv_ref, gt_ref):
    xn = _rmsnorm(x_ref[...], g_ref[...]).astype(BF16)
    q_ref[...] = (_dot(xn, wq_ref[...]) * (HEAD_DIM ** -0.5)).astype(BF16)
    kv_ref[...] = _dot(xn, wkv_ref[...]).astype(BF16)
    gt_ref[...] = jax.nn.sigmoid(_dot(xn, wg_ref[...]))


def _nsa_proj(h, g, wq, wkv, wg, *, tm=512):
    B, S, D = h.shape
    nq, nkv, ng = wq.shape[1], wkv.shape[1], wg.shape[1]
    const = lambda shape: pl.BlockSpec(shape, lambda b, t: (0,) * len(shape))
    row = lambda n: pl.BlockSpec((None, tm, n), lambda b, t: (b, t, 0))
    return pl.pallas_call(
        _nsa_proj_kernel,
        grid=(B, S // tm),
        in_specs=[row(D), const((1, D)), const((D, nq)), const((D, nkv)), const((D, ng))],
        out_specs=[row(nq), row(nkv), row(ng)],
        out_shape=[jax.ShapeDtypeStruct((B, S, nq), BF16),
                   jax.ShapeDtypeStruct((B, S, nkv), BF16),
                   jax.ShapeDtypeStruct((B, S, ng), F32)],
        compiler_params=_params(("arbitrary", "arbitrary")),
        name="nsa_proj",
    )(h, g.reshape(1, D), wq, wkv, wg)


def _cmp_kernel(t2_ref, pos_ref, w1_ref, b1_ref, w2_ref, o_ref, *, ngroups):
    half = w1_ref.shape[0] // 2
    w_top = w1_ref[0:half, :]
    w_bot = w1_ref[half:, :]
    pos = pos_ref[...].astype(BF16)
    bias = _dot(pos[:, :half], w_top) + _dot(pos[:, half:], w_bot) + b1_ref[...]
    bias = bias[0:1, :]
    nrow = t2_ref.shape[1]
    out = jnp.zeros(o_ref.shape, F32)
    for g in range(ngroups):
        t2 = t2_ref[g]
        u = _dot(t2, w_top)
        v = _dot(t2, w_bot)
        hid = u + pltpu.roll(v, nrow - 1, 0) + bias
        out = out + _dot(_gelu(hid).astype(BF16), w2_ref[g])
    o_ref[...] = out.astype(o_ref.dtype)


def _nsa_compress(t2, pos, w1, b1, w2pad):
    B, _, G, nrow, half = t2.shape
    H = w1.shape[2]
    nout = w2pad.shape[3]
    return pl.pallas_call(
        functools.partial(_cmp_kernel, ngroups=G),
        grid=(B, 2),
        in_specs=[
            pl.BlockSpec((None, None, G, nrow, half), lambda b, s: (b, s, 0, 0, 0)),
            pl.BlockSpec((None, SUBLANES, 2 * half), lambda b, s: (s, 0, 0)),
            pl.BlockSpec((None, 2 * half, H), lambda b, s: (s, 0, 0)),
            pl.BlockSpec((None, 1, H), lambda b, s: (s, 0, 0)),
            pl.BlockSpec((None, G, H, nout), lambda b, s: (s, 0, 0, 0)),
        ],
        out_specs=pl.BlockSpec((None, None, nrow, nout), lambda b, s: (b, s, 0, 0)),
        out_shape=jax.ShapeDtypeStruct((B, 2, nrow, nout), BF16),
        compiler_params=_params(("arbitrary", "arbitrary")),
        name="nsa_compress",
    )(t2, pos, w1, b1, w2pad)


NSLOT = 2 * GROUP


def _attn_kernel(q_ref, ks_ref, vs_ref, kw_ref, vw_ref, kc_ref, vc_ref, gt_ref, ovt_ref, ex_ref,
                 o_ref, qs_scr, selb_scr, m_scr, l_scr, acc_scr, oc_scr, os_scr,
                 *, tq, tk, seq):
    pair = pl.program_id(1)
    qt = pl.program_id(2)
    q0 = qt * tq
    ncmp = kc_ref.shape[0]
    nsel = seq // SEL_BLOCK

    lane = lax.broadcasted_iota(jnp.int32, (tq, LANES), 1)
    row = lax.broadcasted_iota(jnp.int32, (tq, LANES), 0)
    low = lane < HEAD_DIM

    def slope(slot):
        s0 = 2.0 ** (-(slot + 1) / 2.0)
        s1 = 2.0 ** (-(NSLOT + slot + 1) / 2.0)
        return jnp.where(pair == 0, s0, s1).astype(F32)

    for r in range(GROUP):
        qc = q_ref[:, r * LANES:(r + 1) * LANES]
        zero = jnp.zeros_like(qc)
        qs_scr[r] = jnp.where(low, qc, zero)
        qs_scr[GROUP + r] = jnp.where(low, zero, qc)

    kc = kc_ref[...]
    vc = vc_ref[...]
    cidx = lax.broadcasted_iota(jnp.int32, (1, ncmp), 1)
    center = (cidx * CMP_STRIDE - q0).astype(F32) + (CMP_BLOCK - 1) / 2.0
    cend = lax.broadcasted_iota(jnp.int32, (tq, ncmp), 1) * CMP_STRIDE + (CMP_BLOCK - 1)
    mc = cend <= (q0 + lax.broadcasted_iota(jnp.int32, (tq, ncmp), 0))
    psum = [jnp.zeros((tq, ncmp), F32), jnp.zeros((tq, ncmp), F32)]
    for slot in range(NSLOT):
        s = _dot_nt(qs_scr[slot], kc) + slope(slot) * center
        sm = jnp.where(mc, s, NEG)
        mx = jnp.max(sm, axis=1, keepdims=True)
        e = jnp.where(mc, jnp.exp(sm - mx), 0.0)
        l = jnp.sum(e, axis=1, keepdims=True)
        p = e / jnp.where(l > 0.0, l, 1.0)
        psum[slot // GROUP] = psum[slot // GROUP] + p
        oc_scr[slot] = _dot(p.astype(BF16), vc)

    ovt = ovt_ref[...]
    jidx = lax.broadcasted_iota(jnp.int32, (nsel, tq), 0)
    cur = (q0 + lax.broadcasted_iota(jnp.int32, (nsel, tq), 1)) // SEL_BLOCK
    forced = (jidx == 0) | (jidx == cur) | (jidx == cur - 1)
    future = jidx > cur
    col = lax.broadcasted_iota(jnp.int32, (tq, LANES), 1)
    trow = q0 + row
    for gi in range(2):
        ps = psum[gi]
        p1 = ps.astype(BF16)
        r1 = ps - p1.astype(F32)
        p2 = r1.astype(BF16)
        p3 = (r1 - p2.astype(F32)).astype(BF16)
        imp = _dot_nt(ovt, p1) + _dot_nt(ovt, p2) + _dot_nt(ovt, p3)
        score = jnp.where(forced, FORCED_BONUS, jnp.where(future, -1.0, imp))
        selt = jnp.zeros((nsel, tq), F32)
        for _ in range(min(N_SELECT, nsel)):
            mx = jnp.max(score, axis=0, keepdims=True)
            cand = jnp.where(score == mx, jidx, nsel)
            first = jnp.min(cand, axis=0, keepdims=True)
            hit = jidx == first
            selt = jnp.where(hit, 1.0, selt)
            score = jnp.where(hit, -3e38, score)
        selt = jnp.concatenate([selt, jnp.zeros((LANES - nsel, tq), F32)], axis=0)
        sel = selt.T.astype(BF16)
        for cchunk in range(seq // LANES):
            c0 = cchunk * LANES
            hitk = _dot(sel, ex_ref[:, c0:c0 + LANES])
            causal = jnp.where((col + c0) <= trow, 0.0, NEG)
            selb_scr[gi, :, c0:c0 + LANES] = jnp.where(hitk > 0.5, causal, NEG)

    def sweep(k_ref, v_ref, j_lo, j_hi, bias_fn, out_scr):
        m_scr[...] = jnp.full(m_scr.shape, NEG, F32)
        l_scr[...] = jnp.zeros(l_scr.shape, F32)
        acc_scr[...] = jnp.zeros(acc_scr.shape, F32)

        def body(j, carry):
            k0 = pl.multiple_of(j * tk, tk)
            k2 = k_ref[pl.ds(k0, tk), :]
            v2 = v_ref[pl.ds(k0, tk), :]
            pos = (k0 - q0 + lax.broadcasted_iota(jnp.int32, (1, tk), 1)).astype(F32)
            biases = bias_fn(k0)
            for slot in range(NSLOT):
                s = _dot_nt(qs_scr[slot], k2) + slope(slot) * pos + biases[slot // GROUP]
                m_prev = m_scr[slot]
                m_new = jnp.maximum(m_prev, jnp.max(s, axis=1, keepdims=True))
                alpha = jnp.exp(m_prev - m_new)
                p = jnp.exp(s - pltpu.repeat(m_new, tk // LANES, axis=1))
                l_scr[slot] = alpha * l_scr[slot] + jnp.sum(p, axis=1, keepdims=True)
                acc_scr[slot] = alpha * acc_scr[slot] + _dot(p.astype(BF16), v2)
                m_scr[slot] = m_new
            return carry

        lax.fori_loop(j_lo, j_hi, body, 0)
        for slot in range(NSLOT):
            out_scr[slot] = acc_scr[slot] / l_scr[slot]

    def sel_bias(k0):
        return [selb_scr[0, :, pl.ds(k0, tk)], selb_scr[1, :, pl.ds(k0, tk)]]

    rc = (lax.broadcasted_iota(jnp.int32, (tq, tk), 0)
          - lax.broadcasted_iota(jnp.int32, (tq, tk), 1))

    def win_bias(k0):
        dist = rc + (q0 - k0)
        b = jnp.where(dist >= 0, jnp.where(dist < WINDOW, 0.0, NEG), NEG)
        return [b, b]

    j_hi = (q0 + tq + tk - 1) // tk
    sweep(ks_ref, vs_ref, 0, j_hi, sel_bias, os_scr)
    j_lo = jnp.maximum(q0 - (WINDOW - 1), 0) // tk
    sweep(kw_ref, vw_ref, j_lo, j_hi, win_bias, acc_scr)

    gt = gt_ref[...]
    for r in range(GROUP):
        comb = []
        for gi in range(2):
            slot = gi * GROUP + r
            comb.append(gt[:, slot:slot + 1] * oc_scr[slot]
                        + gt[:, NSLOT + slot:NSLOT + slot + 1] * os_scr[slot]
                        + gt[:, 2 * NSLOT + slot:2 * NSLOT + slot + 1] * acc_scr[slot])
        o_ref[:, r * LANES:(r + 1) * LANES] = jnp.where(low, comb[0], comb[1]).astype(o_ref.dtype)


def _nsa_attention(q, kv, kvc, gates, *, tq=128, tk=256):
    B, S, _ = q.shape
    npair = N_KV // 2
    ncmp = kvc.shape[2]
    nsel = S // SEL_BLOCK
    assert S % tq == 0 and S % tk == 0 and tk % LANES == 0 and nsel <= LANES
    cstart = np.arange(ncmp) * CMP_STRIDE
    selj = np.arange(nsel)
    ovt = ((cstart[None, :] < (selj[:, None] + 1) * SEL_BLOCK)
           & (cstart[None, :] + CMP_BLOCK > selj[:, None] * SEL_BLOCK))
    ovt = jnp.asarray(ovt, BF16)
    expand = (np.arange(S)[None, :] // SEL_BLOCK) == np.arange(LANES)[:, None]
    expand = jnp.asarray(expand, BF16)

    def kvspec(cb):
        return pl.BlockSpec((None, S, LANES), lambda b, p, t: (b, 0, cb + p))

    def cspec(s):
        return pl.BlockSpec((None, None, ncmp, LANES), lambda b, p, t: (b, s, 0, p))

    qw = GROUP * LANES
    return pl.pallas_call(
        functools.partial(_attn_kernel, tq=tq, tk=tk, seq=S),
        grid=(B, npair, S // tq),
        in_specs=[
            pl.BlockSpec((None, tq, qw), lambda b, p, t: (b, t, p)),
            kvspec(4), kvspec(6), kvspec(8), kvspec(10),
            cspec(0), cspec(1),
            pl.BlockSpec((None, tq, LANES), lambda b, p, t: (b, t, p)),
            pl.BlockSpec((nsel, ncmp), lambda b, p, t: (0, 0)),
            pl.BlockSpec((LANES, S), lambda b, p, t: (0, 0)),
        ],
        out_specs=pl.BlockSpec((None, tq, qw), lambda b, p, t: (b, t, p)),
        out_shape=jax.ShapeDtypeStruct((B, S, npair * qw), BF16),
        scratch_shapes=[
            pltpu.VMEM((NSLOT, tq, LANES), BF16),
            pltpu.VMEM((2, tq, S), F32),
            pltpu.VMEM((NSLOT, tq, LANES), F32),
            pltpu.VMEM((NSLOT, tq, LANES), F32),
            pltpu.VMEM((NSLOT, tq, LANES), F32),
            pltpu.VMEM((NSLOT, tq, LANES), F32),
            pltpu.VMEM((NSLOT, tq, LANES), F32),
        ],
        compiler_params=_params(("arbitrary", "arbitrary", "arbitrary")),
        name="nsa_attention",
    )(q, kv, kv, kv, kv, kvc, kvc, gates, ovt, expand)


def _mm_res_kernel(a_ref, w_ref, r_ref, o_ref):
    o_ref[...] = r_ref[...] + _dot(a_ref[...], w_ref[...])


def _mm_res(a, w, res, *, tm=512):
    B, S, K = a.shape
    N = w.shape[1]
    return pl.pallas_call(
        _mm_res_kernel,
        grid=(B, S // tm),
        in_specs=[pl.BlockSpec((None, tm, K), lambda b, t: (b, t, 0)),
                  pl.BlockSpec((K, N), lambda b, t: (0, 0)),
                  pl.BlockSpec((None, tm, N), lambda b, t: (b, t, 0))],
        out_specs=pl.BlockSpec((None, tm, N), lambda b, t: (b, t, 0)),
        out_shape=jax.ShapeDtypeStruct((B, S, N), F32),
        compiler_params=_params(("arbitrary", "arbitrary")),
        name="mm_res",
    )(a, w, res)


def _head_pair_perm():
    perm = np.zeros(N_HEADS * HEAD_DIM, np.int32)
    d = np.arange(HEAD_DIM)
    for pair in range(N_KV // 2):
        for r in range(GROUP):
            for gi in range(2):
                h = (2 * pair + gi) * GROUP + r
                n0 = pair * GROUP * LANES + r * LANES + gi * HEAD_DIM
                perm[n0 + d] = h * HEAD_DIM + d
    return perm


def _nsa_layer(h, g, w_in, cmp_pos, cmp_w1, cmp_b1, cmp_w2, w_out):
    B, S, D = h.shape
    qcols = N_HEADS * HEAD_DIM
    kvcols = 3 * 2 * N_KV * HEAD_DIM
    ngate = 3 * N_HEADS
    perm = _head_pair_perm()
    wq = w_in[:, :qcols][:, perm].astype(BF16)
    wkv = w_in[:, qcols:qcols + kvcols].astype(BF16)
    wg_src = w_in[:, qcols + kvcols:]
    wg = jnp.zeros((D, (N_KV // 2) * LANES), F32)
    for pair in range(N_KV // 2):
        for br in range(3):
            src = br * N_HEADS + pair * NSLOT
            dst = pair * LANES + br * NSLOT
            wg = wg.at[:, dst:dst + NSLOT].set(wg_src[:, src:src + NSLOT])
    wg = wg.astype(BF16)

    q, kv, gates = _nsa_proj(h, g, wq, wkv, wg)

    ccols = 2 * N_KV * HEAD_DIM
    nrow = S // CMP_STRIDE
    t2 = kv[:, :, :ccols].reshape(B, S, 2, N_KV, HEAD_DIM).transpose(0, 2, 3, 1, 4)
    t2 = t2.reshape(B, 2, N_KV, nrow, CMP_STRIDE * HEAD_DIM)
    pos = jnp.broadcast_to(cmp_pos.reshape(2, 1, CMP_BLOCK * HEAD_DIM),
                           (2, SUBLANES, CMP_BLOCK * HEAD_DIM))
    hid = cmp_w1.shape[2]
    w2pad = jnp.zeros((2, N_KV, hid, N_KV * HEAD_DIM), F32)
    for gidx in range(N_KV):
        w2pad = w2pad.at[:, gidx, :, gidx * HEAD_DIM:(gidx + 1) * HEAD_DIM].set(cmp_w2)
    kvc = _nsa_compress(t2, pos, cmp_w1.astype(BF16), cmp_b1.reshape(2, 1, hid),
                        w2pad.astype(BF16))

    o = _nsa_attention(q, kv, kvc, gates)
    return _mm_res(o, w_out[perm, :].astype(BF16), h)


def kernel(x, lru_norm_g, lru_w_in, lru_conv_w, lru_conv_b, lru_gate_w, lru_gate_b, lru_a_param,
           lru_w_out, nsa_norm_g, nsa_w_in, nsa_cmp_pos, nsa_cmp_w1, nsa_cmp_b1, nsa_cmp_w2,
           nsa_w_out, ffn_norm_g, ffn_w_in, ffn_conv_w, ffn_conv_b, ffn_w_out, final_norm_g):
    h = _lru_layer(x, lru_norm_g[0], lru_w_in[0], lru_conv_w[0], lru_conv_b[0], lru_gate_w[0],
                   lru_gate_b[0], lru_a_param[0], lru_w_out[0])
    h = _ffn_layer(h, ffn_norm_g[0], ffn_w_in[0], ffn_conv_w[0], ffn_conv_b[0], ffn_w_out[0],
                   final_norm_g, final_norm=False)
    h = _nsa_layer(h, nsa_norm_g[0], nsa_w_in[0], nsa_cmp_pos[0], nsa_cmp_w1[0], nsa_cmp_b1[0],
                   nsa_cmp_w2[0], nsa_w_out[0])
    h = _ffn_layer(h, ffn_norm_g[1], ffn_w_in[1], ffn_conv_w[1], ffn_conv_b[1], ffn_w_out[1],
                   final_norm_g, final_norm=True)
    return h
```

```python
import functools
import math

import ml_dtypes
import numpy as np
import jax
import jax.numpy as jnp
from jax import lax
from jax.experimental import pallas as pl
from jax.experimental.pallas import tpu as pltpu

BF16 = jnp.bfloat16
F32 = jnp.float32

EPS = 1e-6
LRU_BLOCK_W = 128
LRU_CONV = 4
LRU_C = 8.0
N_HEADS = 16
HEAD_DIM = 64
N_KV = 4
GROUP = N_HEADS // N_KV
CMP_BLOCK = 32
CMP_STRIDE = 16
SEL_BLOCK = 64
N_SELECT = 8
WINDOW = 512
FORCED_BONUS = 1e4
NEG = -1e30
FFN_CONV = 3

LANES = 128
SUBLANES = 8
VMEM_LIMIT = 56 * 1024 * 1024


def _gelu(x):
    c = math.sqrt(2.0 / math.pi)
    return x * (0.5 * (1.0 + jnp.tanh(c * (x + 0.044715 * (x * x * x)))))


def _rmsnorm(x, g):
    return x * lax.rsqrt(jnp.mean(x * x, axis=-1, keepdims=True) + EPS) * g


def _dot(a, b):
    return jnp.dot(a, b, preferred_element_type=F32)


def _dot_nt(a, b):
    return lax.dot_general(a, b, (((1,), (1,)), ((), ())), preferred_element_type=F32)


def _params(sem):
    return pltpu.CompilerParams(dimension_semantics=sem, vmem_limit_bytes=VMEM_LIMIT)


SCAN_CHUNK = SUBLANES * SUBLANES


def _lru_kernel(x_ref, g_ref, wy_ref, wx_ref, cw_ref, cb_ref, gw_ref, gb_ref, ap_ref, wo_ref,
                o_ref, xb_scr, a_scr, b_scr, hc_scr, u_scr, *, tm, nblk):
    t = pl.program_id(1)

    @pl.when(t == 0)
    def _():
        xb_scr[0:SUBLANES, :] = jnp.zeros((SUBLANES, xb_scr.shape[1]), F32)
        hc_scr[...] = jnp.zeros(hc_scr.shape, F32)

    x = x_ref[...]
    xn = _rmsnorm(x, g_ref[...]).astype(BF16)
    y = _gelu(_dot(xn, wy_ref[...]))
    xb = _dot(xn, wx_ref[...])

    xb_scr[SUBLANES:SUBLANES + tm, :] = xb
    cw = cw_ref[...]
    xc = (cw[3:4, :] * xb + cw[2:3, :] * xb_scr[7:7 + tm, :] + cw[1:2, :] * xb_scr[6:6 + tm, :]
          + cw[0:1, :] * xb_scr[5:5 + tm, :] + cb_ref[...])
    xb_scr[0:SUBLANES, :] = xb_scr[tm:tm + SUBLANES, :]

    z = -ap_ref[...]
    c8 = -LRU_C * (jnp.maximum(z, 0.0) + jnp.log1p(jnp.exp(-jnp.abs(z))))
    gb = gb_ref[...]

    sub = lax.broadcasted_iota(jnp.int32, (SUBLANES, LANES), 0)
    for n in range(nblk):
        lo, hi = n * LANES, (n + 1) * LANES
        xcn = xc[:, lo:hi]
        xcb = xcn.astype(BF16)
        r = jax.nn.sigmoid(_dot(xcb, gw_ref[0, n]) + gb[0:1, lo:hi])
        ig = jax.nn.sigmoid(_dot(xcb, gw_ref[1, n]) + gb[1:2, lo:hi])
        log_a = c8[:, lo:hi] * r
        a = jnp.exp(log_a)
        mult = jnp.sqrt(-jnp.tanh(log_a) * (a * a + 1.0))
        a_scr[n] = a
        b_scr[n] = mult * ig * xcn

        carry = hc_scr[n]
        for c in range(tm // SCAN_CHUNK):
            base = c * SCAN_CHUNK
            acum, hloc = [], []
            for j in range(SUBLANES):
                aj = a_scr.at[n][pl.ds(base + j, SUBLANES, stride=SUBLANES), :]
                bj = b_scr.at[n][pl.ds(base + j, SUBLANES, stride=SUBLANES), :]
                if j == 0:
                    acum.append(aj)
                    hloc.append(bj)
                else:
                    hloc.append(aj * hloc[-1] + bj)
                    acum.append(aj * acum[-1])
            p, e = acum[-1], hloc[-1]
            for d in (1, 2, 4):
                keep = sub >= d
                psh = pltpu.roll(p, d, 0)
                esh = pltpu.roll(e, d, 0)
                e = jnp.where(keep, p * esh + e, e)
                p = jnp.where(keep, p * psh, p)
            hend = e + p * carry
            cin = jnp.where(sub == 0, carry, pltpu.roll(hend, 1, 0))
            for j in range(SUBLANES):
                b_scr.at[n][pl.ds(base + j, SUBLANES, stride=SUBLANES), :] = (
                    hloc[j] + acum[j] * cin)
            carry = jnp.broadcast_to(hend[SUBLANES - 1:SUBLANES, :], (SUBLANES, LANES))
        hc_scr[n] = carry
        u_scr[:, lo:hi] = (b_scr[n] * y[:, lo:hi]).astype(BF16)

    o_ref[...] = x + _dot(u_scr[...], wo_ref[...])


def _lru_layer(h, g, w_in, conv_w, conv_b, gate_w, gate_b, a_param, w_out, *, tm=256):
    B, S, D = h.shape
    W = w_out.shape[0]
    nblk = W // LANES
    assert gate_w.shape == (2, nblk, LANES, LANES) and S % tm == 0 and tm % SCAN_CHUNK == 0
    wy = w_in[:, :W].astype(BF16)
    wx = w_in[:, W:].astype(BF16)
    const = lambda shape: pl.BlockSpec(shape, lambda b, t: (0,) * len(shape))
    return pl.pallas_call(
        functools.partial(_lru_kernel, tm=tm, nblk=nblk),
        grid=(B, S // tm),
        in_specs=[
            pl.BlockSpec((None, tm, D), lambda b, t: (b, t, 0)),
            const((1, D)), const((D, W)), const((D, W)), const((LRU_CONV, W)), const((1, W)),
            const((2, nblk, LANES, LANES)), const((2, W)), const((1, W)), const((W, D)),
        ],
        out_specs=pl.BlockSpec((None, tm, D), lambda b, t: (b, t, 0)),
        out_shape=jax.ShapeDtypeStruct((B, S, D), F32),
        scratch_shapes=[
            pltpu.VMEM((tm + SUBLANES, W), F32),
            pltpu.VMEM((nblk, tm, LANES), F32),
            pltpu.VMEM((nblk, tm, LANES), F32),
            pltpu.VMEM((nblk, SUBLANES, LANES), F32),
            pltpu.VMEM((tm, W), BF16),
        ],
        compiler_params=_params(("arbitrary", "arbitrary")),
        name="lru_layer",
    )(h, g.reshape(1, D), wy, wx, conv_w, conv_b.reshape(1, W), gate_w.astype(BF16), gate_b,
      a_param.reshape(1, W), w_out.astype(BF16))


def _ffn_kernel(x_ref, g_ref, wa_ref, wb_ref, cw_ref, cb_ref, wo_ref, fg_ref, o_ref,
                xn_scr, acc_scr, a_scr, halo_scr, *, tm, final_norm):
    t = pl.program_id(1)
    f = pl.program_id(2)
    nf = pl.num_programs(2)

    @pl.when(f == 0)
    def _():
        xn_scr[...] = _rmsnorm(x_ref[...], g_ref[...]).astype(BF16)
        acc_scr[...] = jnp.zeros(acc_scr.shape, F32)

    @pl.when(t == 0)
    def _():
        halo_scr[f] = jnp.zeros(halo_scr.shape[1:], F32)

    xn = xn_scr[...]
    a = _dot(xn, wa_ref[...])
    b = _dot(xn, wb_ref[...])
    a_scr[0:SUBLANES, :] = halo_scr[f]
    a_scr[SUBLANES:SUBLANES + tm, :] = a
    cw = cw_ref[...]
    ac = (cw[2:3, :] * a + cw[1:2, :] * a_scr[7:7 + tm, :] + cw[0:1, :] * a_scr[6:6 + tm, :]
          + cb_ref[...])
    halo_scr[f] = a_scr[tm:tm + SUBLANES, :]
    hmid = (_gelu(ac) * b).astype(BF16)
    acc_scr[...] += _dot(hmid, wo_ref[...])

    @pl.when(f == nf - 1)
    def _():
        out = x_ref[...] + acc_scr[...]
        if final_norm:
            out = _rmsnorm(out, fg_ref[...])
        o_ref[...] = out


def _ffn_layer(h, g, w_in, conv_w, conv_b, w_out, final_g, *, final_norm, tm=512, tf=512):
    B, S, D = h.shape
    F = w_out.shape[0]
    assert S % tm == 0 and F % tf == 0
    nf = F // tf
    wa = w_in[:, :F].astype(BF16)
    wb = w_in[:, F:].astype(BF16)
    return pl.pallas_call(
        functools.partial(_ffn_kernel, tm=tm, final_norm=final_norm),
        grid=(B, S // tm, nf),
        in_specs=[
            pl.BlockSpec((None, tm, D), lambda b, t, f: (b, t, 0)),
            pl.BlockSpec((1, D), lambda b, t, f: (0, 0)),
            pl.BlockSpec((D, tf), lambda b, t, f: (0, f)),
            pl.BlockSpec((D, tf), lambda b, t, f: (0, f)),
            pl.BlockSpec((FFN_CONV, tf), lambda b, t, f: (0, f)),
            pl.BlockSpec((1, tf), lambda b, t, f: (0, f)),
            pl.BlockSpec((tf, D), lambda b, t, f: (f, 0)),
            pl.BlockSpec((1, D), lambda b, t, f: (0, 0)),
        ],
        out_specs=pl.BlockSpec((None, tm, D), lambda b, t, f: (b, t, 0)),
        out_shape=jax.ShapeDtypeStruct((B, S, D), F32),
        scratch_shapes=[
            pltpu.VMEM((tm, D), BF16),
            pltpu.VMEM((tm, D), F32),
            pltpu.VMEM((tm + SUBLANES, tf), F32),
            pltpu.VMEM((nf, SUBLANES, tf), F32),
        ],
        compiler_params=_params(("arbitrary", "arbitrary", "arbitrary")),
        name="ffn_layer",
    )(h, g.reshape(1, D), wa, wb, conv_w, conv_b.reshape(1, F), w_out.astype(BF16),
      final_g.reshape(1, D))


def _nsa_proj_kernel(x_ref, g_ref, wq_ref, wkv_ref, wg_ref, q_ref, kv_ref, gt_ref):
    xn = _rmsnorm(x_ref[...], g_ref[...]).astype(BF16)
    q_ref[...] = (_dot(xn, wq_ref[...]) * (HEAD_DIM ** -0.5 * LOG2E)).astype(BF16)
    kv_ref[...] = _dot(xn, wkv_ref[...]).astype(BF16)
    gt_ref[...] = jax.nn.sigmoid(_dot(xn, wg_ref[...]))


def _nsa_proj(h, g, wq, wkv, wg, *, tm=512):
    B, S, D = h.shape
    nq, nkv, ng = wq.shape[1], wkv.shape[1], wg.shape[1]
    const = lambda shape: pl.BlockSpec(shape, lambda b, t: (0,) * len(shape))
    row = lambda n: pl.BlockSpec((None, tm, n), lambda b, t: (b, t, 0))
    return pl.pallas_call(
        _nsa_proj_kernel,
        grid=(B, S // tm),
        in_specs=[row(D), const((1, D)), const((D, nq)), const((D, nkv)), const((D, ng))],
        out_specs=[row(nq), row(nkv), row(ng)],
        out_shape=[jax.ShapeDtypeStruct((B, S, nq), BF16),
                   jax.ShapeDtypeStruct((B, S, nkv), BF16),
                   jax.ShapeDtypeStruct((B, S, ng), F32)],
        compiler_params=_params(("arbitrary", "arbitrary")),
        name="nsa_proj",
    )(h, g.reshape(1, D), wq, wkv, wg)


def _cmp_kernel(t2_ref, pos_ref, w1_ref, b1_ref, w2_ref, o_ref, *, ngroups):
    half = w1_ref.shape[0] // 2
    w_top = w1_ref[0:half, :]
    w_bot = w1_ref[half:, :]
    pos = pos_ref[...].astype(BF16)
    bias = _dot(pos[:, :half], w_top) + _dot(pos[:, half:], w_bot) + b1_ref[...]
    bias = bias[0:1, :]
    nrow = t2_ref.shape[1]
    out = jnp.zeros(o_ref.shape, F32)
    for g in range(ngroups):
        t2 = t2_ref[g]
        u = _dot(t2, w_top)
        v = _dot(t2, w_bot)
        hid = u + pltpu.roll(v, nrow - 1, 0) + bias
        out = out + _dot(_gelu(hid).astype(BF16), w2_ref[g])
    o_ref[...] = out.astype(o_ref.dtype)


def _nsa_compress(t2, pos, w1, b1, w2pad):
    B, _, G, nrow, half = t2.shape
    H = w1.shape[2]
    nout = w2pad.shape[3]
    return pl.pallas_call(
        functools.partial(_cmp_kernel, ngroups=G),
        grid=(B, 2),
        in_specs=[
            pl.BlockSpec((None, None, G, nrow, half), lambda b, s: (b, s, 0, 0, 0)),
            pl.BlockSpec((None, SUBLANES, 2 * half), lambda b, s: (s, 0, 0)),
            pl.BlockSpec((None, 2 * half, H), lambda b, s: (s, 0, 0)),
            pl.BlockSpec((None, 1, H), lambda b, s: (s, 0, 0)),
            pl.BlockSpec((None, G, H, nout), lambda b, s: (s, 0, 0, 0)),
        ],
        out_specs=pl.BlockSpec((None, None, nrow, nout), lambda b, s: (b, s, 0, 0)),
        out_shape=jax.ShapeDtypeStruct((B, 2, nrow, nout), BF16),
        compiler_params=_params(("arbitrary", "arbitrary")),
        name="nsa_compress",
    )(t2, pos, w1, b1, w2pad)


NSLOT = 2 * GROUP
LOG2E = 1.4426950408889634
NPIECE = 4
SEL_LANE0 = 0
POS_LANE0 = 32
CMP_LANE0 = 40


def _bf16_pieces(x, n):
    out, r = [], np.float64(x)
    for _ in range(n):
        p = np.float64(np.asarray(r, np.float32).astype(ml_dtypes.bfloat16).astype(np.float32))
        out.append(p)
        r = r - p
    return out


def _feature_base(gi):
    return HEAD_DIM if gi == 0 else 0


def _attn_tables(S, ncmp):
    npair = N_KV // 2
    qf = np.zeros((npair, NSLOT, LANES), np.float32)
    for pair in range(npair):
        for slot in range(NSLOT):
            f0 = _feature_base(slot // GROUP)
            h = NSLOT * pair + slot
            for i, p in enumerate(_bf16_pieces(2.0 ** (-(h + 1) / 2.0) * LOG2E, NPIECE)):
                qf[pair, slot, f0 + POS_LANE0 + 2 * i] = SEL_BLOCK * p
                qf[pair, slot, f0 + POS_LANE0 + 2 * i + 1] = p
                qf[pair, slot, f0 + CMP_LANE0 + i] = CMP_STRIDE * p
    pos = np.arange(S)
    fs = np.zeros((2, S, LANES), np.float32)
    fw = np.zeros((2, S, LANES), np.float32)
    fc = np.zeros((2, ncmp, LANES), np.float32)
    for gi in range(2):
        f0 = _feature_base(gi)
        fs[gi, pos, f0 + SEL_LANE0 + pos // SEL_BLOCK] = 1.0
        for i in range(NPIECE):
            for f in (fs, fw):
                f[gi, :, f0 + POS_LANE0 + 2 * i] = pos // SEL_BLOCK
                f[gi, :, f0 + POS_LANE0 + 2 * i + 1] = pos % SEL_BLOCK
            fc[gi, :, f0 + CMP_LANE0 + i] = np.arange(ncmp)
    return (jnp.asarray(qf), jnp.asarray(fs, BF16), jnp.asarray(fw, BF16), jnp.asarray(fc, BF16))


def _attn_kernel(q_ref, ks_ref, vs_ref, kw_ref, vw_ref, kc_ref, vc_ref, gt_ref, ovt_ref, qf_ref,
                 fs_ref, fw_ref, fc_ref, o_ref, qaug_scr, kaug_scr, vt_scr, bias_scr, m_scr,
                 l_scr, acc_scr, oc_scr, os_scr, s_scr, *, tq, seq):
    tk = tq
    qt = pl.program_id(2)
    q0 = qt * tq
    ncmp = kc_ref.shape[0]
    nsel = seq // SEL_BLOCK
    ntile = seq // tk

    @pl.when(qt == 0)
    def _():
        own0 = lax.broadcasted_iota(jnp.int32, (tk, LANES), 1) < HEAD_DIM
        for br, k_ref, f_ref, v_ref in ((0, ks_ref, fs_ref, vs_ref), (1, kw_ref, fw_ref, vw_ref)):
            for j in range(ntile):
                rows = slice(j * tk, (j + 1) * tk)
                k2 = k_ref[rows, :]
                kaug_scr[br, 0, j] = jnp.where(own0, k2, f_ref[0, rows, :])
                kaug_scr[br, 1, j] = jnp.where(own0, f_ref[1, rows, :], k2)
                vt_scr[br, j] = v_ref[rows, :].astype(F32).T.astype(BF16)
        d = (lax.broadcasted_iota(jnp.int32, (tk, tq), 1)
             - lax.broadcasted_iota(jnp.int32, (tk, tq), 0))
        bias_scr[0] = jnp.where(d < 0, 0.0, NEG)
        bias_scr[1] = jnp.zeros((tk, tq), F32)
        bias_scr[2] = jnp.where(d >= 0, 0.0, NEG)

    low = lax.broadcasted_iota(jnp.int32, (tq, LANES), 1) < HEAD_DIM
    qf = qf_ref[...]
    for r in range(GROUP):
        qc = q_ref[:, r * LANES:(r + 1) * LANES]
        f_a = jnp.broadcast_to(qf[r:r + 1, :], (tq, LANES)).astype(BF16)
        f_b = jnp.broadcast_to(qf[GROUP + r:GROUP + r + 1, :], (tq, LANES)).astype(BF16)
        qaug_scr[r] = jnp.where(low, qc, f_a)
        qaug_scr[GROUP + r] = jnp.where(low, f_b, qc)

    own0c = lax.broadcasted_iota(jnp.int32, (ncmp, LANES), 1) < HEAD_DIM
    kc2 = kc_ref[...]
    kcaug = [jnp.where(own0c, kc2, fc_ref[0]), jnp.where(own0c, fc_ref[1], kc2)]
    vct = vc_ref[...].astype(F32).T.astype(BF16)
    cend = lax.broadcasted_iota(jnp.int32, (ncmp, tq), 0) * CMP_STRIDE + (CMP_BLOCK - 1)
    mc = cend <= (q0 + lax.broadcasted_iota(jnp.int32, (ncmp, tq), 1))
    psum = [jnp.zeros((ncmp, tq), F32), jnp.zeros((ncmp, tq), F32)]
    for slot in range(NSLOT):
        gi = slot // GROUP
        sm = jnp.where(mc, _dot_nt(kcaug[gi], qaug_scr[slot]), NEG)
        mx = jnp.max(sm, axis=0, keepdims=True)
        e = jnp.where(mc, jnp.exp2(sm - mx), 0.0)
        l = jnp.sum(e, axis=0, keepdims=True)
        p = e / jnp.where(l > 0.0, l, 1.0)
        psum[gi] = psum[gi] + p
        oc_scr[slot] = _dot(vct[gi * HEAD_DIM:(gi + 1) * HEAD_DIM, :], p.astype(BF16))

    ovt = ovt_ref[...]
    jidx = lax.broadcasted_iota(jnp.int32, (nsel, tq), 0)
    cur = (q0 + lax.broadcasted_iota(jnp.int32, (nsel, tq), 1)) // SEL_BLOCK
    forced = (jidx == 0) | (jidx == cur) | (jidx == cur - 1)
    future = jidx > cur
    lane = lax.broadcasted_iota(jnp.int32, (tq, LANES), 1)
    for gi in range(2):
        ps = psum[gi]
        p1 = ps.astype(BF16)
        r1 = ps - p1.astype(F32)
        p2 = r1.astype(BF16)
        p3 = (r1 - p2.astype(F32)).astype(BF16)
        imp = _dot(ovt, p1) + _dot(ovt, p2) + _dot(ovt, p3)
        score = jnp.where(forced, FORCED_BONUS, jnp.where(future, -1.0, imp))
        selneg = jnp.full((nsel, tq), NEG, F32)
        for _ in range(min(N_SELECT, nsel)):
            mx = jnp.max(score, axis=0, keepdims=True)
            cand = jnp.where(score == mx, jidx, nsel)
            first = jnp.min(cand, axis=0, keepdims=True)
            hit = jidx == first
            selneg = jnp.where(hit, 0.0, selneg)
            score = jnp.where(hit, -3e38, score)
        f0 = _feature_base(gi) + SEL_LANE0
        parts = [selneg]
        if f0 > 0:
            parts.insert(0, jnp.zeros((f0, tq), F32))
        if LANES - f0 - nsel > 0:
            parts.append(jnp.zeros((LANES - f0 - nsel, tq), F32))
        selq = jnp.concatenate(parts, axis=0).T.astype(BF16)
        in_sel = (lane >= f0) & (lane < f0 + nsel)
        for r in range(GROUP):
            slot = gi * GROUP + r
            qaug_scr[slot] = jnp.where(in_sel, selq, qaug_scr[slot])

    def reset():
        m_scr[...] = jnp.full(m_scr.shape, NEG, F32)
        l_scr[...] = jnp.zeros(l_scr.shape, F32)
        acc_scr[...] = jnp.zeros(acc_scr.shape, F32)

    def tile(br, j, bias):
        vt = vt_scr[br, j]

        def scores(slot):
            return _dot_nt(kaug_scr[br, slot // GROUP, j], qaug_scr[slot])

        s_scr[0] = scores(0)
        for slot in range(NSLOT):
            if slot + 1 < NSLOT:
                s_scr[(slot + 1) % 2] = scores(slot + 1)
            s = s_scr[slot % 2]
            if bias is not None:
                s = s + bias
            gi = slot // GROUP
            m_prev = m_scr[slot]
            m_new = jnp.maximum(m_prev, jnp.max(s, axis=0, keepdims=True))
            alpha = jnp.exp2(m_prev - m_new)
            p = jnp.exp2(s - m_new)
            l_scr[slot] = alpha * l_scr[slot] + jnp.sum(p, axis=0, keepdims=True)
            acc_scr[slot] = alpha * acc_scr[slot] + _dot(
                vt[gi * HEAD_DIM:(gi + 1) * HEAD_DIM, :], p.astype(BF16))
            m_scr[slot] = m_new

    reset()

    def sel_body(j, carry):
        tile(0, j, None)
        return carry

    lax.fori_loop(0, qt, sel_body, 0)
    tile(0, qt, bias_scr[2])
    for slot in range(NSLOT):
        os_scr[slot] = acc_scr[slot] / l_scr[slot]

    reset()

    def win_body(j, carry):
        tile(1, j, bias_scr[j - qt + 2])
        return carry

    lax.fori_loop(jnp.maximum(qt - 2, 0), qt + 1, win_body, 0)

    gtt = gt_ref[...].T
    for r in range(GROUP):
        comb = []
        for gi in range(2):
            slot = gi * GROUP + r
            ow = acc_scr[slot] / l_scr[slot]
            comb.append(gtt[slot:slot + 1, :] * oc_scr[slot]
                        + gtt[NSLOT + slot:NSLOT + slot + 1, :] * os_scr[slot]
                        + gtt[2 * NSLOT + slot:2 * NSLOT + slot + 1, :] * ow)
        col = jnp.concatenate(comb, axis=0).T
        o_ref[:, r * LANES:(r + 1) * LANES] = col.astype(o_ref.dtype)


def _nsa_attention(q, kv, kvc, gates, *, tq=256):
    B, S, _ = q.shape
    npair = N_KV // 2
    ncmp = kvc.shape[2]
    nsel = S // SEL_BLOCK
    assert S % tq == 0 and WINDOW == 2 * tq and nsel <= SEL_LANE0 + POS_LANE0
    cstart = np.arange(ncmp) * CMP_STRIDE
    selj = np.arange(nsel)
    ovt = ((cstart[None, :] < (selj[:, None] + 1) * SEL_BLOCK)
           & (cstart[None, :] + CMP_BLOCK > selj[:, None] * SEL_BLOCK))
    ovt = jnp.asarray(ovt, BF16)
    qf, fs, fw, fc = _attn_tables(S, ncmp)

    def kvspec(cb):
        return pl.BlockSpec((None, S, LANES), lambda b, p, t: (b, 0, cb + p))

    def cspec(s):
        return pl.BlockSpec((None, None, ncmp, LANES), lambda b, p, t: (b, s, 0, p))

    const = lambda shape: pl.BlockSpec(shape, lambda b, p, t: (0,) * len(shape))
    qw = GROUP * LANES
    ntile = S // tq
    return pl.pallas_call(
        functools.partial(_attn_kernel, tq=tq, seq=S),
        grid=(B, npair, S // tq),
        in_specs=[
            pl.BlockSpec((None, tq, qw), lambda b, p, t: (b, t, p)),
            kvspec(4), kvspec(6), kvspec(8), kvspec(10),
            cspec(0), cspec(1),
            pl.BlockSpec((None, tq, LANES), lambda b, p, t: (b, t, p)),
            const((nsel, ncmp)),
            pl.BlockSpec((None, NSLOT, LANES), lambda b, p, t: (p, 0, 0)),
            const((2, S, LANES)), const((2, S, LANES)), const((2, ncmp, LANES)),
        ],
        out_specs=pl.BlockSpec((None, tq, qw), lambda b, p, t: (b, t, p)),
        out_shape=jax.ShapeDtypeStruct((B, S, npair * qw), BF16),
        scratch_shapes=[
            pltpu.VMEM((NSLOT, tq, LANES), BF16),
            pltpu.VMEM((2, 2, ntile, tq, LANES), BF16),
            pltpu.VMEM((2, ntile, LANES, tq), BF16),
            pltpu.VMEM((3, tq, tq), F32),
            pltpu.VMEM((NSLOT, 1, tq), F32),
            pltpu.VMEM((NSLOT, 1, tq), F32),
            pltpu.VMEM((NSLOT, HEAD_DIM, tq), F32),
            pltpu.VMEM((NSLOT, HEAD_DIM, tq), F32),
            pltpu.VMEM((NSLOT, HEAD_DIM, tq), F32),
            pltpu.VMEM((2, tq, tq), F32),
        ],
        compiler_params=_params(("arbitrary", "arbitrary", "arbitrary")),
        name="nsa_attention",
    )(q, kv, kv, kv, kv, kvc, kvc, gates, ovt, qf, fs, fw, fc)


def _mm_res_kernel(a_ref, w_ref, r_ref, o_ref):
    o_ref[...] = r_ref[...] + _dot(a_ref[...], w_ref[...])


def _mm_res(a, w, res, *, tm=512):
    B, S, K = a.shape
    N = w.shape[1]
    return pl.pallas_call(
        _mm_res_kernel,
        grid=(B, S // tm),
        in_specs=[pl.BlockSpec((None, tm, K), lambda b, t: (b, t, 0)),
                  pl.BlockSpec((K, N), lambda b, t: (0, 0)),
                  pl.BlockSpec((None, tm, N), lambda b, t: (b, t, 0))],
        out_specs=pl.BlockSpec((None, tm, N), lambda b, t: (b, t, 0)),
        out_shape=jax.ShapeDtypeStruct((B, S, N), F32),
        compiler_params=_params(("arbitrary", "arbitrary")),
        name="mm_res",
    )(a, w, res)


def _head_pair_perm():
    perm = np.zeros(N_HEADS * HEAD_DIM, np.int32)
    d = np.arange(HEAD_DIM)
    for pair in range(N_KV // 2):
        for r in range(GROUP):
            for gi in range(2):
                h = (2 * pair + gi) * GROUP + r
                n0 = pair * GROUP * LANES + r * LANES + gi * HEAD_DIM
                perm[n0 + d] = h * HEAD_DIM + d
    return perm


def _nsa_layer(h, g, w_in, cmp_pos, cmp_w1, cmp_b1, cmp_w2, w_out):
    B, S, D = h.shape
    qcols = N_HEADS * HEAD_DIM
    kvcols = 3 * 2 * N_KV * HEAD_DIM
    ngate = 3 * N_HEADS
    perm = _head_pair_perm()
    wq = w_in[:, :qcols][:, perm].astype(BF16)
    wkv = w_in[:, qcols:qcols + kvcols].astype(BF16)
    wg_src = w_in[:, qcols + kvcols:]
    wg = jnp.zeros((D, (N_KV // 2) * LANES), F32)
    for pair in range(N_KV // 2):
        for br in range(3):
            src = br * N_HEADS + pair * NSLOT
            dst = pair * LANES + br * NSLOT
            wg = wg.at[:, dst:dst + NSLOT].set(wg_src[:, src:src + NSLOT])
    wg = wg.astype(BF16)

    q, kv, gates = _nsa_proj(h, g, wq, wkv, wg)

    ccols = 2 * N_KV * HEAD_DIM
    nrow = S // CMP_STRIDE
    t2 = kv[:, :, :ccols].reshape(B, S, 2, N_KV, HEAD_DIM).transpose(0, 2, 3, 1, 4)
    t2 = t2.reshape(B, 2, N_KV, nrow, CMP_STRIDE * HEAD_DIM)
    pos = jnp.broadcast_to(cmp_pos.reshape(2, 1, CMP_BLOCK * HEAD_DIM),
                           (2, SUBLANES, CMP_BLOCK * HEAD_DIM))
    hid = cmp_w1.shape[2]
    w2pad = jnp.zeros((2, N_KV, hid, N_KV * HEAD_DIM), F32)
    for gidx in range(N_KV):
        w2pad = w2pad.at[:, gidx, :, gidx * HEAD_DIM:(gidx + 1) * HEAD_DIM].set(cmp_w2)
    kvc = _nsa_compress(t2, pos, cmp_w1.astype(BF16), cmp_b1.reshape(2, 1, hid),
                        w2pad.astype(BF16))

    o = _nsa_attention(q, kv, kvc, gates)
    return _mm_res(o, w_out[perm, :].astype(BF16), h)


def kernel(x, lru_norm_g, lru_w_in, lru_conv_w, lru_conv_b, lru_gate_w, lru_gate_b, lru_a_param,
           lru_w_out, nsa_norm_g, nsa_w_in, nsa_cmp_pos, nsa_cmp_w1, nsa_cmp_b1, nsa_cmp_w2,
           nsa_w_out, ffn_norm_g, ffn_w_in, ffn_conv_w, ffn_conv_b, ffn_w_out, final_norm_g):
    h = _lru_layer(x, lru_norm_g[0], lru_w_in[0], lru_conv_w[0], lru_conv_b[0], lru_gate_w[0],
                   lru_gate_b[0], lru_a_param[0], lru_w_out[0])
    h = _ffn_layer(h, ffn_norm_g[0], ffn_w_in[0], ffn_conv_w[0], ffn_conv_b[0], ffn_w_out[0],
                   final_norm_g, final_norm=False)
    h = _nsa_layer(h, nsa_norm_g[0], nsa_w_in[0], nsa_cmp_pos[0], nsa_cmp_w1[0], nsa_cmp_b1[0],
                   nsa_cmp_w2[0], nsa_w_out[0])
    h = _ffn_layer(h, ffn_norm_g[1], ffn_w_in[1], ffn_conv_w[1], ffn_conv_b[1], ffn_w_out[1],
                   final_norm_g, final_norm=True)
    return h
```

```python
import functools
import math

import ml_dtypes
import numpy as np
import jax
import jax.numpy as jnp
from jax import lax
from jax.experimental import pallas as pl
from jax.experimental.pallas import tpu as pltpu

BF16 = jnp.bfloat16
F32 = jnp.float32

EPS = 1e-6
LRU_BLOCK_W = 128
LRU_CONV = 4
LRU_C = 8.0
N_HEADS = 16
HEAD_DIM = 64
N_KV = 4
GROUP = N_HEADS // N_KV
CMP_BLOCK = 32
CMP_STRIDE = 16
SEL_BLOCK = 64
N_SELECT = 8
WINDOW = 512
FORCED_BONUS = 1e4
NEG = -1e30
FFN_CONV = 3

LANES = 128
SUBLANES = 8
VMEM_LIMIT = 56 * 1024 * 1024


def _gelu(x):
    c = math.sqrt(2.0 / math.pi)
    inner = x * (c + (c * 0.044715) * (x * x))
    return (0.5 * x) * (1.0 + jnp.tanh(inner))


def _sigmoid(x):
    return 0.5 * jnp.tanh(0.5 * x) + 0.5


def _rmsnorm(x, g):
    return x * lax.rsqrt(jnp.mean(x * x, axis=-1, keepdims=True) + EPS) * g


def _dot(a, b):
    return jnp.dot(a, b, preferred_element_type=F32)


def _dot_nt(a, b):
    return lax.dot_general(a, b, (((1,), (1,)), ((), ())), preferred_element_type=F32)


def _params(sem):
    return pltpu.CompilerParams(dimension_semantics=sem, vmem_limit_bytes=VMEM_LIMIT)


SCAN_CHUNK = SUBLANES * SUBLANES


def _lru_kernel(x_ref, g_ref, wy_ref, wx_ref, cw_ref, cb_ref, gw_ref, gb_ref, ap_ref, wo_ref,
                o_ref, xb_scr, a_scr, b_scr, hc_scr, u_scr, *, tm, nblk):
    t = pl.program_id(1)

    @pl.when(t == 0)
    def _():
        xb_scr[0:SUBLANES, :] = jnp.zeros((SUBLANES, xb_scr.shape[1]), F32)
        hc_scr[...] = jnp.zeros(hc_scr.shape, F32)

    x = x_ref[...]
    xn = _rmsnorm(x, g_ref[...]).astype(BF16)
    y = _gelu(_dot(xn, wy_ref[...]))
    xb = _dot(xn, wx_ref[...])

    xb_scr[SUBLANES:SUBLANES + tm, :] = xb
    cw = cw_ref[...]
    xc = (cw[3:4, :] * xb + cw[2:3, :] * xb_scr[7:7 + tm, :] + cw[1:2, :] * xb_scr[6:6 + tm, :]
          + cw[0:1, :] * xb_scr[5:5 + tm, :] + cb_ref[...])
    xb_scr[0:SUBLANES, :] = xb_scr[tm:tm + SUBLANES, :]

    z = -ap_ref[...]
    c8 = -LRU_C * (jnp.maximum(z, 0.0) + jnp.log1p(jnp.exp(-jnp.abs(z))))
    gb = gb_ref[...]

    sub = lax.broadcasted_iota(jnp.int32, (SUBLANES, LANES), 0)
    for n in range(nblk):
        lo, hi = n * LANES, (n + 1) * LANES
        xcn = xc[:, lo:hi]
        xcb = xcn.astype(BF16)
        r = _sigmoid(_dot(xcb, gw_ref[0, n]) + gb[0:1, lo:hi])
        ig = _sigmoid(_dot(xcb, gw_ref[1, n]) + gb[1:2, lo:hi])
        log_a = c8[:, lo:hi] * r
        a = jnp.exp(log_a)
        mult = jnp.sqrt(-jnp.tanh(log_a) * (a * a + 1.0))
        a_scr[n] = a
        b_scr[n] = mult * ig * xcn

        carry = hc_scr[n]
        for c in range(tm // SCAN_CHUNK):
            base = c * SCAN_CHUNK
            acum, hloc = [], []
            for j in range(SUBLANES):
                aj = a_scr.at[n][pl.ds(base + j, SUBLANES, stride=SUBLANES), :]
                bj = b_scr.at[n][pl.ds(base + j, SUBLANES, stride=SUBLANES), :]
                if j == 0:
                    acum.append(aj)
                    hloc.append(bj)
                else:
                    hloc.append(aj * hloc[-1] + bj)
                    acum.append(aj * acum[-1])
            p, e = acum[-1], hloc[-1]
            for d in (1, 2, 4):
                keep = sub >= d
                psh = pltpu.roll(p, d, 0)
                esh = pltpu.roll(e, d, 0)
                e = jnp.where(keep, p * esh + e, e)
                p = jnp.where(keep, p * psh, p)
            hend = e + p * carry
            cin = jnp.where(sub == 0, carry, pltpu.roll(hend, 1, 0))
            for j in range(SUBLANES):
                b_scr.at[n][pl.ds(base + j, SUBLANES, stride=SUBLANES), :] = (
                    hloc[j] + acum[j] * cin)
            carry = jnp.broadcast_to(hend[SUBLANES - 1:SUBLANES, :], (SUBLANES, LANES))
        hc_scr[n] = carry
        u_scr[:, lo:hi] = (b_scr[n] * y[:, lo:hi]).astype(BF16)

    o_ref[...] = x + _dot(u_scr[...], wo_ref[...])


def _lru_layer(h, g, w_in, conv_w, conv_b, gate_w, gate_b, a_param, w_out, *, tm=256):
    B, S, D = h.shape
    W = w_out.shape[0]
    nblk = W // LANES
    assert gate_w.shape == (2, nblk, LANES, LANES) and S % tm == 0 and tm % SCAN_CHUNK == 0
    wy = w_in[:, :W].astype(BF16)
    wx = w_in[:, W:].astype(BF16)
    const = lambda shape: pl.BlockSpec(shape, lambda b, t: (0,) * len(shape))
    return pl.pallas_call(
        functools.partial(_lru_kernel, tm=tm, nblk=nblk),
        grid=(B, S // tm),
        in_specs=[
            pl.BlockSpec((None, tm, D), lambda b, t: (b, t, 0)),
            const((1, D)), const((D, W)), const((D, W)), const((LRU_CONV, W)), const((1, W)),
            const((2, nblk, LANES, LANES)), const((2, W)), const((1, W)), const((W, D)),
        ],
        out_specs=pl.BlockSpec((None, tm, D), lambda b, t: (b, t, 0)),
        out_shape=jax.ShapeDtypeStruct((B, S, D), F32),
        scratch_shapes=[
            pltpu.VMEM((tm + SUBLANES, W), F32),
            pltpu.VMEM((nblk, tm, LANES), F32),
            pltpu.VMEM((nblk, tm, LANES), F32),
            pltpu.VMEM((nblk, SUBLANES, LANES), F32),
            pltpu.VMEM((tm, W), BF16),
        ],
        compiler_params=_params(("arbitrary", "arbitrary")),
        name="lru_layer",
    )(h, g.reshape(1, D), wy, wx, conv_w, conv_b.reshape(1, W), gate_w.astype(BF16), gate_b,
      a_param.reshape(1, W), w_out.astype(BF16))


def _ffn_kernel(*refs, tm, final_norm, has_mix):
    if has_mix:
        mix_ref, wmix_ref, *refs = refs
    (x_ref, g_ref, wa_ref, wb_ref, cw_ref, cb_ref, wo_ref, fg_ref, o_ref,
     xn_scr, acc_scr, a_scr, halo_scr) = refs
    t = pl.program_id(1)
    f = pl.program_id(2)
    nf = pl.num_programs(2)

    @pl.when(f == 0)
    def _():
        x = x_ref[...]
        if has_mix:
            x = x + _dot(mix_ref[...], wmix_ref[...])
        xn_scr[...] = _rmsnorm(x, g_ref[...]).astype(BF16)
        acc_scr[...] = x

    @pl.when(t == 0)
    def _():
        halo_scr[f] = jnp.zeros(halo_scr.shape[1:], F32)

    xn = xn_scr[...]
    a = _dot(xn, wa_ref[...])
    b = _dot(xn, wb_ref[...])
    a_scr[0:SUBLANES, :] = halo_scr[f]
    a_scr[SUBLANES:SUBLANES + tm, :] = a
    cw = cw_ref[...]
    ac = (cw[2:3, :] * a + cw[1:2, :] * a_scr[7:7 + tm, :] + cw[0:1, :] * a_scr[6:6 + tm, :]
          + cb_ref[...])
    halo_scr[f] = a_scr[tm:tm + SUBLANES, :]
    hmid = (_gelu(ac) * b).astype(BF16)
    acc_scr[...] += _dot(hmid, wo_ref[...])

    @pl.when(f == nf - 1)
    def _():
        out = acc_scr[...]
        if final_norm:
            out = _rmsnorm(out, fg_ref[...])
        o_ref[...] = out


def _ffn_layer(h, g, w_in, conv_w, conv_b, w_out, final_g, *, final_norm, mix=None, wmix=None,
               tm=1024, tf=512):
    B, S, D = h.shape
    F = w_out.shape[0]
    assert S % tm == 0 and F % tf == 0
    nf = F // tf
    wa = w_in[:, :F].astype(BF16)
    wb = w_in[:, F:].astype(BF16)
    has_mix = mix is not None
    mix_specs, mix_args = [], []
    if has_mix:
        K = mix.shape[2]
        mix_specs = [pl.BlockSpec((None, tm, K), lambda b, t, f: (b, t, 0)),
                     pl.BlockSpec((K, D), lambda b, t, f: (0, 0))]
        mix_args = [mix, wmix]
    return pl.pallas_call(
        functools.partial(_ffn_kernel, tm=tm, final_norm=final_norm, has_mix=has_mix),
        grid=(B, S // tm, nf),
        in_specs=mix_specs + [
            pl.BlockSpec((None, tm, D), lambda b, t, f: (b, t, 0)),
            pl.BlockSpec((1, D), lambda b, t, f: (0, 0)),
            pl.BlockSpec((D, tf), lambda b, t, f: (0, f)),
            pl.BlockSpec((D, tf), lambda b, t, f: (0, f)),
            pl.BlockSpec((FFN_CONV, tf), lambda b, t, f: (0, f)),
            pl.BlockSpec((1, tf), lambda b, t, f: (0, f)),
            pl.BlockSpec((tf, D), lambda b, t, f: (f, 0)),
            pl.BlockSpec((1, D), lambda b, t, f: (0, 0)),
        ],
        out_specs=pl.BlockSpec((None, tm, D), lambda b, t, f: (b, t, 0)),
        out_shape=jax.ShapeDtypeStruct((B, S, D), F32),
        scratch_shapes=[
            pltpu.VMEM((tm, D), BF16),
            pltpu.VMEM((tm, D), F32),
            pltpu.VMEM((tm + SUBLANES, tf), F32),
            pltpu.VMEM((nf, SUBLANES, tf), F32),
        ],
        compiler_params=_params(("arbitrary", "arbitrary", "arbitrary")),
        name="ffn_layer",
    )(*mix_args, h, g.reshape(1, D), wa, wb, conv_w, conv_b.reshape(1, F), w_out.astype(BF16),
      final_g.reshape(1, D))


def _nsa_proj_kernel(x_ref, g_ref, wq_ref, wkv_ref, wg_ref, q_ref, kv_ref, gt_ref):
    xn = _rmsnorm(x_ref[...], g_ref[...]).astype(BF16)
    q_ref[...] = (_dot(xn, wq_ref[...]) * (HEAD_DIM ** -0.5 * LOG2E)).astype(BF16)
    kv_ref[...] = _dot(xn, wkv_ref[...]).astype(BF16)
    gt_ref[...] = _sigmoid(_dot(xn, wg_ref[...]))


def _nsa_proj(h, g, wq, wkv, wg, *, tm=512):
    B, S, D = h.shape
    nq, nkv, ng = wq.shape[1], wkv.shape[1], wg.shape[1]
    const = lambda shape: pl.BlockSpec(shape, lambda b, t: (0,) * len(shape))
    row = lambda n: pl.BlockSpec((None, tm, n), lambda b, t: (b, t, 0))
    return pl.pallas_call(
        _nsa_proj_kernel,
        grid=(B, S // tm),
        in_specs=[row(D), const((1, D)), const((D, nq)), const((D, nkv)), const((D, ng))],
        out_specs=[row(nq), row(nkv), row(ng)],
        out_shape=[jax.ShapeDtypeStruct((B, S, nq), BF16),
                   jax.ShapeDtypeStruct((B, S, nkv), BF16),
                   jax.ShapeDtypeStruct((B, S, ng), F32)],
        compiler_params=_params(("arbitrary", "arbitrary")),
        name="nsa_proj",
    )(h, g.reshape(1, D), wq, wkv, wg)


def _cmp_kernel(t2_ref, pos_ref, w1_ref, b1_ref, w2_ref, o_ref, *, ngroups):
    half = w1_ref.shape[0] // 2
    w_top = w1_ref[0:half, :]
    w_bot = w1_ref[half:, :]
    pos = pos_ref[...].astype(BF16)
    bias = _dot(pos[:, :half], w_top) + _dot(pos[:, half:], w_bot) + b1_ref[...]
    bias = bias[0:1, :]
    nrow = t2_ref.shape[1]
    out = jnp.zeros(o_ref.shape, F32)
    for g in range(ngroups):
        t2 = t2_ref[g]
        u = _dot(t2, w_top)
        v = _dot(t2, w_bot)
        hid = u + pltpu.roll(v, nrow - 1, 0) + bias
        out = out + _dot(_gelu(hid).astype(BF16), w2_ref[g])
    o_ref[...] = out.astype(o_ref.dtype)


def _nsa_compress(t2, pos, w1, b1, w2pad):
    B, _, G, nrow, half = t2.shape
    H = w1.shape[2]
    nout = w2pad.shape[3]
    return pl.pallas_call(
        functools.partial(_cmp_kernel, ngroups=G),
        grid=(B, 2),
        in_specs=[
            pl.BlockSpec((None, None, G, nrow, half), lambda b, s: (b, s, 0, 0, 0)),
            pl.BlockSpec((None, SUBLANES, 2 * half), lambda b, s: (s, 0, 0)),
            pl.BlockSpec((None, 2 * half, H), lambda b, s: (s, 0, 0)),
            pl.BlockSpec((None, 1, H), lambda b, s: (s, 0, 0)),
            pl.BlockSpec((None, G, H, nout), lambda b, s: (s, 0, 0, 0)),
        ],
        out_specs=pl.BlockSpec((None, None, nrow, nout), lambda b, s: (b, s, 0, 0)),
        out_shape=jax.ShapeDtypeStruct((B, 2, nrow, nout), BF16),
        compiler_params=_params(("arbitrary", "arbitrary")),
        name="nsa_compress",
    )(t2, pos, w1, b1, w2pad)


NSLOT = 2 * GROUP
LOG2E = 1.4426950408889634
NPIECE = 4
SEL_LANE0 = 0
POS_LANE0 = 32
CMP_LANE0 = 40
QK_AHEAD = 3
ONES_ROWS = 16


def _bf16_pieces(x, n):
    out, r = [], np.float64(x)
    for _ in range(n):
        p = np.float64(np.asarray(r, np.float32).astype(ml_dtypes.bfloat16).astype(np.float32))
        out.append(p)
        r = r - p
    return out


def _feature_base(gi):
    return HEAD_DIM if gi == 0 else 0


def _attn_tables(S, ncmp):
    npair = N_KV // 2
    qf = np.zeros((npair, NSLOT, LANES), np.float32)
    for pair in range(npair):
        for slot in range(NSLOT):
            f0 = _feature_base(slot // GROUP)
            h = NSLOT * pair + slot
            for i, p in enumerate(_bf16_pieces(2.0 ** (-(h + 1) / 2.0) * LOG2E, NPIECE)):
                qf[pair, slot, f0 + POS_LANE0 + 2 * i] = SEL_BLOCK * p
                qf[pair, slot, f0 + POS_LANE0 + 2 * i + 1] = p
                qf[pair, slot, f0 + CMP_LANE0 + i] = CMP_STRIDE * p
    pos = np.arange(S)
    fs = np.zeros((2, S, LANES), np.float32)
    fw = np.zeros((2, S, LANES), np.float32)
    fc = np.zeros((2, ncmp, LANES), np.float32)
    for gi in range(2):
        f0 = _feature_base(gi)
        fs[gi, pos, f0 + SEL_LANE0 + pos // SEL_BLOCK] = 1.0
        for i in range(NPIECE):
            for f in (fs, fw):
                f[gi, :, f0 + POS_LANE0 + 2 * i] = pos // SEL_BLOCK
                f[gi, :, f0 + POS_LANE0 + 2 * i + 1] = pos % SEL_BLOCK
            fc[gi, :, f0 + CMP_LANE0 + i] = np.arange(ncmp)
    return (jnp.asarray(qf), jnp.asarray(fs, BF16), jnp.asarray(fw, BF16), jnp.asarray(fc, BF16))


def _attn_kernel(q_ref, ks_ref, vs_ref, kw_ref, vw_ref, kc_ref, vc_ref, gt_ref, ovt_ref, qf_ref,
                 fs_ref, fw_ref, fc_ref, o_ref, qaug_scr, kaug_scr, vt_scr, bias_scr, m_scr,
                 acc_scr, oc_scr, psum_scr, s_scr, p_scr, *, tq, seq):
    tk = tq
    qt = pl.program_id(2)
    q0 = qt * tq
    ncmp = kc_ref.shape[0]
    nsel = seq // SEL_BLOCK
    ntile = seq // tk

    @pl.when(qt == 0)
    def _():
        own0 = lax.broadcasted_iota(jnp.int32, (tk, LANES), 1) < HEAD_DIM
        for br, k_ref, f_ref, v_ref in ((0, ks_ref, fs_ref, vs_ref), (1, kw_ref, fw_ref, vw_ref)):
            for j in range(ntile):
                rows = slice(j * tk, (j + 1) * tk)
                k2 = k_ref[rows, :]
                kaug_scr[br, 0, j] = jnp.where(own0, k2, f_ref[0, rows, :])
                kaug_scr[br, 1, j] = jnp.where(own0, f_ref[1, rows, :], k2)
                vt = v_ref[rows, :].astype(F32).T.astype(BF16)
                ones = jnp.ones((ONES_ROWS, tk), BF16)
                for gi in range(2):
                    vt_scr[br, j, gi] = jnp.concatenate(
                        [vt[gi * HEAD_DIM:(gi + 1) * HEAD_DIM, :], ones], axis=0)
        d = (lax.broadcasted_iota(jnp.int32, (tk, tq), 1)
             - lax.broadcasted_iota(jnp.int32, (tk, tq), 0))
        bias_scr[0] = jnp.where(d < 0, 0.0, NEG)
        bias_scr[1] = jnp.zeros((tk, tq), F32)
        bias_scr[2] = jnp.where(d >= 0, 0.0, NEG)

    low = lax.broadcasted_iota(jnp.int32, (tq, LANES), 1) < HEAD_DIM
    qf = qf_ref[...]
    for r in range(GROUP):
        qc = q_ref[:, r * LANES:(r + 1) * LANES]
        f_a = jnp.broadcast_to(qf[r:r + 1, :], (tq, LANES)).astype(BF16)
        f_b = jnp.broadcast_to(qf[GROUP + r:GROUP + r + 1, :], (tq, LANES)).astype(BF16)
        qaug_scr[r] = jnp.where(low, qc, f_a)
        qaug_scr[GROUP + r] = jnp.where(low, f_b, qc)

    own0c = lax.broadcasted_iota(jnp.int32, (ncmp, LANES), 1) < HEAD_DIM
    kc2 = kc_ref[...]
    kcaug = [jnp.where(own0c, kc2, fc_ref[0]), jnp.where(own0c, fc_ref[1], kc2)]
    vct = vc_ref[...].astype(F32).T.astype(BF16)
    cend = lax.broadcasted_iota(jnp.int32, (ncmp, tq), 0) * CMP_STRIDE + (CMP_BLOCK - 1)
    mc = cend <= (q0 + lax.broadcasted_iota(jnp.int32, (ncmp, tq), 1))
    nbuf = s_scr.shape[0]
    assert NSLOT % nbuf == 0

    def issue_cmp_scores(slot):
        s_scr[slot % nbuf, 0:ncmp, :] = _dot_nt(kcaug[slot // GROUP], qaug_scr[slot])

    def cmp_values(slot):
        gi = slot // GROUP
        oc_scr[slot] = _dot(vct[gi * HEAD_DIM:(gi + 1) * HEAD_DIM, :], p_scr[slot % 2, 0:ncmp, :])

    for slot in range(QK_AHEAD):
        issue_cmp_scores(slot)
    psum = [None, None]
    for slot in range(NSLOT):
        gi = slot // GROUP
        sm = jnp.where(mc, s_scr[slot % nbuf, 0:ncmp, :], NEG)
        mx = jnp.max(sm, axis=0, keepdims=True)
        e = jnp.where(mc, jnp.exp2(sm - mx), 0.0)
        l = jnp.sum(e, axis=0, keepdims=True)
        p = e / jnp.where(l > 0.0, l, 1.0)
        psum[gi] = p if psum[gi] is None else psum[gi] + p
        p_scr[slot % 2, 0:ncmp, :] = p.astype(BF16)
        if slot + QK_AHEAD < NSLOT:
            issue_cmp_scores(slot + QK_AHEAD)
        if slot >= 1:
            cmp_values(slot - 1)
    cmp_values(NSLOT - 1)
    for gi in range(2):
        psum_scr[gi] = psum[gi]

    def normalized(br, slot):
        return (acc_scr[br, slot, 0:HEAD_DIM, :]
                / acc_scr[br, slot, HEAD_DIM:HEAD_DIM + 1, :])

    def issue_scores(br, j, slot):
        s_scr[slot % nbuf] = _dot_nt(kaug_scr[br, slot // GROUP, j], qaug_scr[slot])

    def tile(br, j, j_next, bias, side_work=None):
        def weighted_values(slot, alpha):
            acc_scr[br, slot] = alpha * acc_scr[br, slot] + _dot(
                vt_scr[br, j, slot // GROUP], p_scr[slot % 2])

        alphas = []
        for slot in range(NSLOT):
            s = s_scr[slot % nbuf]
            if bias is not None:
                s = s + bias
            m_prev = m_scr[br, slot]
            m_new = jnp.maximum(m_prev, jnp.max(s, axis=0, keepdims=True))
            alphas.append(jnp.exp2(m_prev - m_new))
            p_scr[slot % 2] = jnp.exp2(s - m_new).astype(BF16)
            m_scr[br, slot] = m_new
            if slot + QK_AHEAD < NSLOT:
                issue_scores(br, j, slot + QK_AHEAD)
            elif j_next is not None:
                issue_scores(br, j_next, slot + QK_AHEAD - NSLOT)
            if slot >= 1:
                weighted_values(slot - 1, alphas[slot - 1])
            if side_work is not None:
                side_work(slot)
        weighted_values(NSLOT - 1, alphas[NSLOT - 1])

    def sweep(br, j_lo, bias_fn, last_tile_side_work):
        for slot in range(QK_AHEAD):
            issue_scores(br, j_lo, slot)

        def body(j, carry):
            tile(br, j, j + 1, bias_fn(j))
            return carry

        lax.fori_loop(j_lo, qt, body, 0)
        tile(br, qt, None, bias_scr[2], last_tile_side_work)

    jidx = lax.broadcasted_iota(jnp.int32, (nsel, tq), 0)
    ntop = min(N_SELECT, nsel)
    topk = {}

    def topk_start():
        ovt = ovt_ref[...]
        cur = (q0 + lax.broadcasted_iota(jnp.int32, (nsel, tq), 1)) // SEL_BLOCK
        forced = (jidx == 0) | (jidx == cur) | (jidx == cur - 1)
        future = jidx > cur
        for gi in range(2):
            ps = psum_scr[gi]
            p1 = ps.astype(BF16)
            r1 = ps - p1.astype(F32)
            p2 = r1.astype(BF16)
            p3 = (r1 - p2.astype(F32)).astype(BF16)
            imp = _dot(ovt, p1) + _dot(ovt, p2) + _dot(ovt, p3)
            topk[gi] = (jnp.where(forced, FORCED_BONUS, jnp.where(future, -1.0, imp)),
                        jnp.full((nsel, tq), NEG, F32))

    def topk_round():
        for gi in range(2):
            score, selneg = topk[gi]
            mx = jnp.max(score, axis=0, keepdims=True)
            cand = jnp.where(score == mx, jidx, nsel)
            first = jnp.min(cand, axis=0, keepdims=True)
            hit = jidx == first
            topk[gi] = (jnp.where(hit, -3e38, score), jnp.where(hit, 0.0, selneg))

    def topk_finish():
        lane = lax.broadcasted_iota(jnp.int32, (tq, LANES), 1)
        for gi in range(2):
            f0 = _feature_base(gi) + SEL_LANE0
            parts = [topk[gi][1]]
            if f0 > 0:
                parts.insert(0, jnp.zeros((f0, tq), F32))
            if LANES - f0 - nsel > 0:
                parts.append(jnp.zeros((LANES - f0 - nsel, tq), F32))
            selq = jnp.concatenate(parts, axis=0).T.astype(BF16)
            in_sel = (lane >= f0) & (lane < f0 + nsel)
            for r in range(GROUP):
                slot = gi * GROUP + r
                qaug_scr[slot] = jnp.where(in_sel, selq, qaug_scr[slot])

    def topk_side_work(slot):
        if slot == 0:
            topk_start()
        for i in range(ntop):
            if i * NSLOT // ntop == slot:
                topk_round()

    m_scr[...] = jnp.full(m_scr.shape, NEG, F32)
    acc_scr[...] = jnp.zeros(acc_scr.shape, F32)
    sweep(1, jnp.maximum(qt - 2, 0), lambda j: bias_scr[j - qt + 2], topk_side_work)
    topk_finish()
    sweep(0, 0, lambda j: None, None)

    gtt = gt_ref[...].T
    for r in range(GROUP):
        comb = []
        for gi in range(2):
            slot = gi * GROUP + r
            comb.append(gtt[slot:slot + 1, :] * oc_scr[slot]
                        + gtt[NSLOT + slot:NSLOT + slot + 1, :] * normalized(0, slot)
                        + gtt[2 * NSLOT + slot:2 * NSLOT + slot + 1, :] * normalized(1, slot))
        col = jnp.concatenate(comb, axis=0).T
        o_ref[:, r * LANES:(r + 1) * LANES] = col.astype(o_ref.dtype)


def _nsa_attention(q, kv, kvc, gates, *, tq=256):
    B, S, _ = q.shape
    npair = N_KV // 2
    ncmp = kvc.shape[2]
    nsel = S // SEL_BLOCK
    assert S % tq == 0 and WINDOW == 2 * tq and nsel <= SEL_LANE0 + POS_LANE0
    cstart = np.arange(ncmp) * CMP_STRIDE
    selj = np.arange(nsel)
    ovt = ((cstart[None, :] < (selj[:, None] + 1) * SEL_BLOCK)
           & (cstart[None, :] + CMP_BLOCK > selj[:, None] * SEL_BLOCK))
    ovt = jnp.asarray(ovt, BF16)
    qf, fs, fw, fc = _attn_tables(S, ncmp)

    def kvspec(cb):
        return pl.BlockSpec((None, S, LANES), lambda b, p, t: (b, 0, cb + p))

    def cspec(s):
        return pl.BlockSpec((None, None, ncmp, LANES), lambda b, p, t: (b, s, 0, p))

    const = lambda shape: pl.BlockSpec(shape, lambda b, p, t: (0,) * len(shape))
    qw = GROUP * LANES
    ntile = S // tq
    return pl.pallas_call(
        functools.partial(_attn_kernel, tq=tq, seq=S),
        grid=(B, npair, S // tq),
        in_specs=[
            pl.BlockSpec((None, tq, qw), lambda b, p, t: (b, t, p)),
            kvspec(4), kvspec(6), kvspec(8), kvspec(10),
            cspec(0), cspec(1),
            pl.BlockSpec((None, tq, LANES), lambda b, p, t: (b, t, p)),
            const((nsel, ncmp)),
            pl.BlockSpec((None, NSLOT, LANES), lambda b, p, t: (p, 0, 0)),
            const((2, S, LANES)), const((2, S, LANES)), const((2, ncmp, LANES)),
        ],
        out_specs=pl.BlockSpec((None, tq, qw), lambda b, p, t: (b, t, p)),
        out_shape=jax.ShapeDtypeStruct((B, S, npair * qw), BF16),
        scratch_shapes=[
            pltpu.VMEM((NSLOT, tq, LANES), BF16),
            pltpu.VMEM((2, 2, ntile, tq, LANES), BF16),
            pltpu.VMEM((2, ntile, 2, HEAD_DIM + ONES_ROWS, tq), BF16),
            pltpu.VMEM((3, tq, tq), F32),
            pltpu.VMEM((2, NSLOT, 1, tq), F32),
            pltpu.VMEM((2, NSLOT, HEAD_DIM + ONES_ROWS, tq), F32),
            pltpu.VMEM((NSLOT, HEAD_DIM, tq), F32),
            pltpu.VMEM((2, ncmp, tq), F32),
            pltpu.VMEM((QK_AHEAD + 1, tq, tq), F32),
            pltpu.VMEM((2, tq, tq), BF16),
        ],
        compiler_params=_params(("arbitrary", "arbitrary", "arbitrary")),
        name="nsa_attention",
    )(q, kv, kv, kv, kv, kvc, kvc, gates, ovt, qf, fs, fw, fc)


def _head_pair_perm():
    perm = np.zeros(N_HEADS * HEAD_DIM, np.int32)
    d = np.arange(HEAD_DIM)
    for pair in range(N_KV // 2):
        for r in range(GROUP):
            for gi in range(2):
                h = (2 * pair + gi) * GROUP + r
                n0 = pair * GROUP * LANES + r * LANES + gi * HEAD_DIM
                perm[n0 + d] = h * HEAD_DIM + d
    return perm


def _nsa_layer(h, g, w_in, cmp_pos, cmp_w1, cmp_b1, cmp_w2, w_out):
    B, S, D = h.shape
    qcols = N_HEADS * HEAD_DIM
    kvcols = 3 * 2 * N_KV * HEAD_DIM
    ngate = 3 * N_HEADS
    perm = _head_pair_perm()
    wq = w_in[:, :qcols][:, perm].astype(BF16)
    wkv = w_in[:, qcols:qcols + kvcols].astype(BF16)
    wg_src = w_in[:, qcols + kvcols:]
    wg = jnp.zeros((D, (N_KV // 2) * LANES), F32)
    for pair in range(N_KV // 2):
        for br in range(3):
            src = br * N_HEADS + pair * NSLOT
            dst = pair * LANES + br * NSLOT
            wg = wg.at[:, dst:dst + NSLOT].set(wg_src[:, src:src + NSLOT])
    wg = wg.astype(BF16)

    q, kv, gates = _nsa_proj(h, g, wq, wkv, wg)

    ccols = 2 * N_KV * HEAD_DIM
    nrow = S // CMP_STRIDE
    t2 = kv[:, :, :ccols].reshape(B, S, 2, N_KV, HEAD_DIM).transpose(0, 2, 3, 1, 4)
    t2 = t2.reshape(B, 2, N_KV, nrow, CMP_STRIDE * HEAD_DIM)
    pos = jnp.broadcast_to(cmp_pos.reshape(2, 1, CMP_BLOCK * HEAD_DIM),
                           (2, SUBLANES, CMP_BLOCK * HEAD_DIM))
    hid = cmp_w1.shape[2]
    w2pad = jnp.zeros((2, N_KV, hid, N_KV * HEAD_DIM), F32)
    for gidx in range(N_KV):
        w2pad = w2pad.at[:, gidx, :, gidx * HEAD_DIM:(gidx + 1) * HEAD_DIM].set(cmp_w2)
    kvc = _nsa_compress(t2, pos, cmp_w1.astype(BF16), cmp_b1.reshape(2, 1, hid),
                        w2pad.astype(BF16))

    return _nsa_attention(q, kv, kvc, gates), w_out[perm, :].astype(BF16)


def kernel(x, lru_norm_g, lru_w_in, lru_conv_w, lru_conv_b, lru_gate_w, lru_gate_b, lru_a_param,
           lru_w_out, nsa_norm_g, nsa_w_in, nsa_cmp_pos, nsa_cmp_w1, nsa_cmp_b1, nsa_cmp_w2,
           nsa_w_out, ffn_norm_g, ffn_w_in, ffn_conv_w, ffn_conv_b, ffn_w_out, final_norm_g):
    h = _lru_layer(x, lru_norm_g[0], lru_w_in[0], lru_conv_w[0], lru_conv_b[0], lru_gate_w[0],
                   lru_gate_b[0], lru_a_param[0], lru_w_out[0])
    h = _ffn_layer(h, ffn_norm_g[0], ffn_w_in[0], ffn_conv_w[0], ffn_conv_b[0], ffn_w_out[0],
                   final_norm_g, final_norm=False)
    o, wo = _nsa_layer(h, nsa_norm_g[0], nsa_w_in[0], nsa_cmp_pos[0], nsa_cmp_w1[0],
                       nsa_cmp_b1[0], nsa_cmp_w2[0], nsa_w_out[0])
    return _ffn_layer(h, ffn_norm_g[1], ffn_w_in[1], ffn_conv_w[1], ffn_conv_b[1], ffn_w_out[1],
                      final_norm_g, final_norm=True, mix=o, wmix=wo)
```

```python
import functools
import math

import ml_dtypes
import numpy as np
import jax
import jax.numpy as jnp
from jax import lax
from jax.experimental import pallas as pl
from jax.experimental.pallas import tpu as pltpu

BF16 = jnp.bfloat16
F32 = jnp.float32

EPS = 1e-6
LRU_BLOCK_W = 128
LRU_CONV = 4
LRU_C = 8.0
N_HEADS = 16
HEAD_DIM = 64
N_KV = 4
GROUP = N_HEADS // N_KV
CMP_BLOCK = 32
CMP_STRIDE = 16
SEL_BLOCK = 64
N_SELECT = 8
WINDOW = 512
FORCED_BONUS = 1e4
NEG = -1e30
FFN_CONV = 3

LANES = 128
SUBLANES = 8
VMEM_LIMIT = 56 * 1024 * 1024


def _gelu(x):
    c = math.sqrt(2.0 / math.pi)
    inner = x * (c + (c * 0.044715) * (x * x))
    return (0.5 * x) * (1.0 + jnp.tanh(inner))


def _sigmoid(x):
    return 0.5 * jnp.tanh(0.5 * x) + 0.5


def _rmsnorm(x, g):
    return x * lax.rsqrt(jnp.mean(x * x, axis=-1, keepdims=True) + EPS) * g


def _dot(a, b):
    return jnp.dot(a, b, preferred_element_type=F32)


def _dot_nt(a, b):
    return lax.dot_general(a, b, (((1,), (1,)), ((), ())), preferred_element_type=F32)


def _params(sem):
    return pltpu.CompilerParams(dimension_semantics=sem, vmem_limit_bytes=VMEM_LIMIT)


SCAN_CHUNK = SUBLANES * SUBLANES
PROJ_W = 2 * LANES


def _lru_kernel(x_ref, g_ref, wy_ref, wx_ref, cw_ref, cb_ref, gw_ref, gb_ref, ap_ref, wo_ref,
                o_ref, tail_scr, a_scr, b_scr, hc_scr, u_scr, yx_scr, *, tm, nblk):
    t = pl.program_id(1)

    @pl.when(t == 0)
    def _():
        tail_scr[...] = jnp.zeros(tail_scr.shape, F32)
        hc_scr[...] = jnp.zeros(hc_scr.shape, F32)

    x = x_ref[...]
    xn = _rmsnorm(x, g_ref[...]).astype(BF16)
    cw = cw_ref[...]
    cb = cb_ref[...]

    z = -ap_ref[...]
    c8 = -LRU_C * (jnp.maximum(z, 0.0) + jnp.log1p(jnp.exp(-jnp.abs(z))))
    gb = gb_ref[...]

    blk_per_chunk = PROJ_W // LANES
    nchunk = nblk // blk_per_chunk

    def project(i):
        cols = slice(i * PROJ_W, (i + 1) * PROJ_W)
        yx_scr[i % 2, 0] = _dot(xn, wy_ref[:, cols])
        yx_scr[i % 2, 1] = _dot(xn, wx_ref[:, cols])

    sub_w = lax.broadcasted_iota(jnp.int32, (SUBLANES, PROJ_W), 0)

    def conv(i):
        cols = slice(i * PROJ_W, (i + 1) * PROJ_W)
        xb = yx_scr[i % 2, 1]
        tail = tail_scr[:, cols]
        xc = cw[LRU_CONV - 1:LRU_CONV, cols] * xb + cb[:, cols]
        for d in range(1, LRU_CONV):
            rolled = pltpu.roll(xb, d, 0)
            top = jnp.where(sub_w < d, pltpu.roll(tail, d, 0), rolled[0:SUBLANES, :])
            shifted = jnp.concatenate([top, rolled[SUBLANES:, :]], axis=0)
            xc = xc + cw[LRU_CONV - 1 - d:LRU_CONV - d, cols] * shifted
        tail_scr[:, cols] = xb[tm - SUBLANES:tm, :]
        return xc

    sub = lax.broadcasted_iota(jnp.int32, (SUBLANES, LANES), 0)
    out_split = (nchunk // 2 + 1) * PROJ_W
    project(0)
    for n in range(nblk):
        i, k = divmod(n, blk_per_chunk)
        if k == 0:
            if i + 1 < nchunk:
                project(i + 1)
            if i * PROJ_W == out_split:
                o_ref[...] = x + _dot(u_scr[:, 0:out_split], wo_ref[0:out_split, :])
            xc = conv(i)
        lo, hi = n * LANES, (n + 1) * LANES
        xcn = xc[:, k * LANES:(k + 1) * LANES]
        xcb = xcn.astype(BF16)
        r = _sigmoid(_dot(xcb, gw_ref[0, n]) + gb[0:1, lo:hi])
        ig = _sigmoid(_dot(xcb, gw_ref[1, n]) + gb[1:2, lo:hi])
        log_a = c8[:, lo:hi] * r
        a = jnp.exp(log_a)
        w = -jnp.tanh(log_a) * (a * a + 1.0)
        mult = jnp.where(w > 0.0, w * lax.rsqrt(w), 0.0)
        a_scr[n] = a
        b_scr[n] = mult * ig * xcn

        carry = hc_scr[n]
        for c in range(tm // SCAN_CHUNK):
            base = c * SCAN_CHUNK
            acum, hloc = [], []
            for j in range(SUBLANES):
                aj = a_scr.at[n][pl.ds(base + j, SUBLANES, stride=SUBLANES), :]
                bj = b_scr.at[n][pl.ds(base + j, SUBLANES, stride=SUBLANES), :]
                if j == 0:
                    acum.append(aj)
                    hloc.append(bj)
                else:
                    hloc.append(aj * hloc[-1] + bj)
                    acum.append(aj * acum[-1])
            p, e = acum[-1], hloc[-1]
            for d in (1, 2, 4):
                keep = sub >= d
                psh = pltpu.roll(p, d, 0)
                esh = pltpu.roll(e, d, 0)
                e = jnp.where(keep, p * esh + e, e)
                p = jnp.where(keep, p * psh, p)
            hend = e + p * carry
            cin = jnp.where(sub == 0, carry, pltpu.roll(hend, 1, 0))
            for j in range(SUBLANES):
                b_scr.at[n][pl.ds(base + j, SUBLANES, stride=SUBLANES), :] = (
                    hloc[j] + acum[j] * cin)
            carry = jnp.broadcast_to(hend[SUBLANES - 1:SUBLANES, :], (SUBLANES, LANES))
        hc_scr[n] = carry
        y = _gelu(yx_scr[i % 2, 0, :, k * LANES:(k + 1) * LANES])
        u_scr[:, lo:hi] = (b_scr[n] * y).astype(BF16)

    o_ref[...] += _dot(u_scr[:, out_split:], wo_ref[out_split:, :])


def _lru_layer(h, g, w_in, conv_w, conv_b, gate_w, gate_b, a_param, w_out, *, tm=256):
    B, S, D = h.shape
    W = w_out.shape[0]
    nblk = W // LANES
    assert gate_w.shape == (2, nblk, LANES, LANES) and S % tm == 0 and tm % SCAN_CHUNK == 0
    wy = w_in[:, :W].astype(BF16)
    wx = w_in[:, W:].astype(BF16)
    const = lambda shape: pl.BlockSpec(shape, lambda b, t: (0,) * len(shape))
    return pl.pallas_call(
        functools.partial(_lru_kernel, tm=tm, nblk=nblk),
        grid=(B, S // tm),
        in_specs=[
            pl.BlockSpec((None, tm, D), lambda b, t: (b, t, 0)),
            const((1, D)), const((D, W)), const((D, W)), const((LRU_CONV, W)), const((1, W)),
            const((2, nblk, LANES, LANES)), const((2, W)), const((1, W)), const((W, D)),
        ],
        out_specs=pl.BlockSpec((None, tm, D), lambda b, t: (b, t, 0)),
        out_shape=jax.ShapeDtypeStruct((B, S, D), F32),
        scratch_shapes=[
            pltpu.VMEM((SUBLANES, W), F32),
            pltpu.VMEM((nblk, tm, LANES), F32),
            pltpu.VMEM((nblk, tm, LANES), F32),
            pltpu.VMEM((nblk, SUBLANES, LANES), F32),
            pltpu.VMEM((tm, W), BF16),
            pltpu.VMEM((2, 2, tm, PROJ_W), F32),
        ],
        compiler_params=_params(("arbitrary", "arbitrary")),
        name="lru_layer",
    )(h, g.reshape(1, D), wy, wx, conv_w, conv_b.reshape(1, W), gate_w.astype(BF16), gate_b,
      a_param.reshape(1, W), w_out.astype(BF16))


def _ffn_kernel(*refs, tm, final_norm, has_mix):
    if has_mix:
        mix_ref, wmix_ref, *refs = refs
    (x_ref, g_ref, wa_ref, wb_ref, cw_ref, cb_ref, wo_ref, fg_ref, o_ref,
     xn_scr, acc_scr, a_scr, halo_scr) = refs
    t = pl.program_id(1)
    f = pl.program_id(2)
    nf = pl.num_programs(2)

    @pl.when(f == 0)
    def _():
        x = x_ref[...]
        if has_mix:
            x = x + _dot(mix_ref[...], wmix_ref[...])
        xn_scr[...] = _rmsnorm(x, g_ref[...]).astype(BF16)
        acc_scr[...] = x

    @pl.when(t == 0)
    def _():
        halo_scr[f] = jnp.zeros(halo_scr.shape[1:], F32)

    xn = xn_scr[...]
    a = _dot(xn, wa_ref[...])
    b = _dot(xn, wb_ref[...])
    a_scr[0:SUBLANES, :] = halo_scr[f]
    a_scr[SUBLANES:SUBLANES + tm, :] = a
    cw = cw_ref[...]
    ac = (cw[2:3, :] * a + cw[1:2, :] * a_scr[7:7 + tm, :] + cw[0:1, :] * a_scr[6:6 + tm, :]
          + cb_ref[...])
    halo_scr[f] = a_scr[tm:tm + SUBLANES, :]
    hmid = (_gelu(ac) * b).astype(BF16)
    acc_scr[...] += _dot(hmid, wo_ref[...])

    @pl.when(f == nf - 1)
    def _():
        out = acc_scr[...]
        if final_norm:
            out = _rmsnorm(out, fg_ref[...])
        o_ref[...] = out


def _ffn_layer(h, g, w_in, conv_w, conv_b, w_out, final_g, *, final_norm, mix=None, wmix=None,
               tm=1024, tf=512):
    B, S, D = h.shape
    F = w_out.shape[0]
    assert S % tm == 0 and F % tf == 0
    nf = F // tf
    wa = w_in[:, :F].astype(BF16)
    wb = w_in[:, F:].astype(BF16)
    has_mix = mix is not None
    mix_specs, mix_args = [], []
    if has_mix:
        K = mix.shape[2]
        mix_specs = [pl.BlockSpec((None, tm, K), lambda b, t, f: (b, t, 0)),
                     pl.BlockSpec((K, D), lambda b, t, f: (0, 0))]
        mix_args = [mix, wmix]
    return pl.pallas_call(
        functools.partial(_ffn_kernel, tm=tm, final_norm=final_norm, has_mix=has_mix),
        grid=(B, S // tm, nf),
        in_specs=mix_specs + [
            pl.BlockSpec((None, tm, D), lambda b, t, f: (b, t, 0)),
            pl.BlockSpec((1, D), lambda b, t, f: (0, 0)),
            pl.BlockSpec((D, tf), lambda b, t, f: (0, f)),
            pl.BlockSpec((D, tf), lambda b, t, f: (0, f)),
            pl.BlockSpec((FFN_CONV, tf), lambda b, t, f: (0, f)),
            pl.BlockSpec((1, tf), lambda b, t, f: (0, f)),
            pl.BlockSpec((tf, D), lambda b, t, f: (f, 0)),
            pl.BlockSpec((1, D), lambda b, t, f: (0, 0)),
        ],
        out_specs=pl.BlockSpec((None, tm, D), lambda b, t, f: (b, t, 0)),
        out_shape=jax.ShapeDtypeStruct((B, S, D), F32),
        scratch_shapes=[
            pltpu.VMEM((tm, D), BF16),
            pltpu.VMEM((tm, D), F32),
            pltpu.VMEM((tm + SUBLANES, tf), F32),
            pltpu.VMEM((nf, SUBLANES, tf), F32),
        ],
        compiler_params=_params(("arbitrary", "arbitrary", "arbitrary")),
        name="ffn_layer",
    )(*mix_args, h, g.reshape(1, D), wa, wb, conv_w, conv_b.reshape(1, F), w_out.astype(BF16),
      final_g.reshape(1, D))


def _nsa_proj_kernel(x_ref, g_ref, wq_ref, wkv_ref, wg_ref, q_ref, kv_ref, gt_ref):
    xn = _rmsnorm(x_ref[...], g_ref[...]).astype(BF16)
    q_ref[...] = (_dot(xn, wq_ref[...]) * (HEAD_DIM ** -0.5 * LOG2E)).astype(BF16)
    kv_ref[...] = _dot(xn, wkv_ref[...]).astype(BF16)
    gt_ref[...] = _sigmoid(_dot(xn, wg_ref[...]))


def _nsa_proj(h, g, wq, wkv, wg, *, tm=512):
    B, S, D = h.shape
    nq, nkv, ng = wq.shape[1], wkv.shape[1], wg.shape[1]
    const = lambda shape: pl.BlockSpec(shape, lambda b, t: (0,) * len(shape))
    row = lambda n: pl.BlockSpec((None, tm, n), lambda b, t: (b, t, 0))
    return pl.pallas_call(
        _nsa_proj_kernel,
        grid=(B, S // tm),
        in_specs=[row(D), const((1, D)), const((D, nq)), const((D, nkv)), const((D, ng))],
        out_specs=[row(nq), row(nkv), row(ng)],
        out_shape=[jax.ShapeDtypeStruct((B, S, nq), BF16),
                   jax.ShapeDtypeStruct((B, S, nkv), BF16),
                   jax.ShapeDtypeStruct((B, S, ng), F32)],
        compiler_params=_params(("arbitrary", "arbitrary")),
        name="nsa_proj",
    )(h, g.reshape(1, D), wq, wkv, wg)


def _cmp_kernel(t2_ref, pos_ref, w1_ref, b1_ref, w2_ref, o_ref, *, ngroups):
    half = w1_ref.shape[0] // 2
    w_top = w1_ref[0:half, :]
    w_bot = w1_ref[half:, :]
    pos = pos_ref[...].astype(BF16)
    bias = _dot(pos[:, :half], w_top) + _dot(pos[:, half:], w_bot) + b1_ref[...]
    bias = bias[0:1, :]
    nrow = t2_ref.shape[1]
    out = jnp.zeros(o_ref.shape, F32)
    for g in range(ngroups):
        t2 = t2_ref[g]
        u = _dot(t2, w_top)
        v = _dot(t2, w_bot)
        hid = u + pltpu.roll(v, nrow - 1, 0) + bias
        out = out + _dot(_gelu(hid).astype(BF16), w2_ref[g])
    o_ref[...] = out.astype(o_ref.dtype)


def _nsa_compress(t2, pos, w1, b1, w2pad):
    B, _, G, nrow, half = t2.shape
    H = w1.shape[2]
    nout = w2pad.shape[3]
    return pl.pallas_call(
        functools.partial(_cmp_kernel, ngroups=G),
        grid=(B, 2),
        in_specs=[
            pl.BlockSpec((None, None, G, nrow, half), lambda b, s: (b, s, 0, 0, 0)),
            pl.BlockSpec((None, SUBLANES, 2 * half), lambda b, s: (s, 0, 0)),
            pl.BlockSpec((None, 2 * half, H), lambda b, s: (s, 0, 0)),
            pl.BlockSpec((None, 1, H), lambda b, s: (s, 0, 0)),
            pl.BlockSpec((None, G, H, nout), lambda b, s: (s, 0, 0, 0)),
        ],
        out_specs=pl.BlockSpec((None, None, nrow, nout), lambda b, s: (b, s, 0, 0)),
        out_shape=jax.ShapeDtypeStruct((B, 2, nrow, nout), BF16),
        compiler_params=_params(("arbitrary", "arbitrary")),
        name="nsa_compress",
    )(t2, pos, w1, b1, w2pad)


NSLOT = 2 * GROUP
LOG2E = 1.4426950408889634
NPIECE = 4
SEL_LANE0 = 0
POS_LANE0 = 32
CMP_LANE0 = 40
QK_AHEAD = 3
ONES_ROWS = 16


def _bf16_pieces(x, n):
    out, r = [], np.float64(x)
    for _ in range(n):
        p = np.float64(np.asarray(r, np.float32).astype(ml_dtypes.bfloat16).astype(np.float32))
        out.append(p)
        r = r - p
    return out


def _feature_base(gi):
    return HEAD_DIM if gi == 0 else 0


def _attn_tables(S, ncmp):
    npair = N_KV // 2
    qf = np.zeros((npair, NSLOT, LANES), np.float32)
    for pair in range(npair):
        for slot in range(NSLOT):
            f0 = _feature_base(slot // GROUP)
            h = NSLOT * pair + slot
            for i, p in enumerate(_bf16_pieces(2.0 ** (-(h + 1) / 2.0) * LOG2E, NPIECE)):
                qf[pair, slot, f0 + POS_LANE0 + 2 * i] = SEL_BLOCK * p
                qf[pair, slot, f0 + POS_LANE0 + 2 * i + 1] = p
                qf[pair, slot, f0 + CMP_LANE0 + i] = CMP_STRIDE * p
    pos = np.arange(S)
    fs = np.zeros((2, S, LANES), np.float32)
    fw = np.zeros((2, S, LANES), np.float32)
    fc = np.zeros((2, ncmp, LANES), np.float32)
    for gi in range(2):
        f0 = _feature_base(gi)
        fs[gi, pos, f0 + SEL_LANE0 + pos // SEL_BLOCK] = 1.0
        for i in range(NPIECE):
            for f in (fs, fw):
                f[gi, :, f0 + POS_LANE0 + 2 * i] = pos // SEL_BLOCK
                f[gi, :, f0 + POS_LANE0 + 2 * i + 1] = pos % SEL_BLOCK
            fc[gi, :, f0 + CMP_LANE0 + i] = np.arange(ncmp)
    return (jnp.asarray(qf), jnp.asarray(fs, BF16), jnp.asarray(fw, BF16), jnp.asarray(fc, BF16))


def _attn_kernel(q_ref, ks_ref, vs_ref, kw_ref, vw_ref, kc_ref, vc_ref, gt_ref, ovt_ref, qf_ref,
                 fs_ref, fw_ref, fc_ref, o_ref, qaug_scr, kaug_scr, vt_scr, bias_scr, m_scr,
                 acc_scr, oc_scr, psum_scr, s_scr, p_scr, *, tq, seq):
    tk = tq
    qt = pl.program_id(2)
    q0 = qt * tq
    ncmp = kc_ref.shape[0]
    nsel = seq // SEL_BLOCK
    ntile = seq // tk

    @pl.when(qt == 0)
    def _():
        own0 = lax.broadcasted_iota(jnp.int32, (tk, LANES), 1) < HEAD_DIM
        for br, k_ref, f_ref, v_ref in ((0, ks_ref, fs_ref, vs_ref), (1, kw_ref, fw_ref, vw_ref)):
            for j in range(ntile):
                rows = slice(j * tk, (j + 1) * tk)
                k2 = k_ref[rows, :]
                kaug_scr[br, 0, j] = jnp.where(own0, k2, f_ref[0, rows, :])
                kaug_scr[br, 1, j] = jnp.where(own0, f_ref[1, rows, :], k2)
                vt = v_ref[rows, :].astype(F32).T.astype(BF16)
                ones = jnp.ones((ONES_ROWS, tk), BF16)
                for gi in range(2):
                    vt_scr[br, j, gi] = jnp.concatenate(
                        [vt[gi * HEAD_DIM:(gi + 1) * HEAD_DIM, :], ones], axis=0)
        d = (lax.broadcasted_iota(jnp.int32, (tk, tq), 1)
             - lax.broadcasted_iota(jnp.int32, (tk, tq), 0))
        bias_scr[0] = jnp.where(d < 0, 0.0, NEG)
        bias_scr[1] = jnp.zeros((tk, tq), F32)
        bias_scr[2] = jnp.where(d >= 0, 0.0, NEG)

    low = lax.broadcasted_iota(jnp.int32, (tq, LANES), 1) < HEAD_DIM
    qf = qf_ref[...]
    for r in range(GROUP):
        qc = q_ref[:, r * LANES:(r + 1) * LANES]
        f_a = jnp.broadcast_to(qf[r:r + 1, :], (tq, LANES)).astype(BF16)
        f_b = jnp.broadcast_to(qf[GROUP + r:GROUP + r + 1, :], (tq, LANES)).astype(BF16)
        qaug_scr[r] = jnp.where(low, qc, f_a)
        qaug_scr[GROUP + r] = jnp.where(low, f_b, qc)

    own0c = lax.broadcasted_iota(jnp.int32, (ncmp, LANES), 1) < HEAD_DIM
    kc2 = kc_ref[...]
    kcaug = [jnp.where(own0c, kc2, fc_ref[0]), jnp.where(own0c, fc_ref[1], kc2)]
    vct = vc_ref[...].astype(F32).T.astype(BF16)
    cend = lax.broadcasted_iota(jnp.int32, (ncmp, tq), 0) * CMP_STRIDE + (CMP_BLOCK - 1)
    mc = cend <= (q0 + lax.broadcasted_iota(jnp.int32, (ncmp, tq), 1))
    nbuf = s_scr.shape[0]
    assert NSLOT % nbuf == 0

    def issue_cmp_scores(slot):
        s_scr[slot % nbuf, 0:ncmp, :] = _dot_nt(kcaug[slot // GROUP], qaug_scr[slot])

    def issue_scores(br, j, slot):
        s_scr[slot % nbuf] = _dot_nt(kaug_scr[br, slot // GROUP, j], qaug_scr[slot])

    def cmp_values(slot):
        gi = slot // GROUP
        oc_scr[slot] = _dot(vct[gi * HEAD_DIM:(gi + 1) * HEAD_DIM, :], p_scr[slot % 2, 0:ncmp, :])

    win_lo = jnp.maximum(qt - 2, 0)
    for slot in range(QK_AHEAD):
        issue_cmp_scores(slot)
    psum = [None, None]
    for slot in range(NSLOT):
        gi = slot // GROUP
        sm = jnp.where(mc, s_scr[slot % nbuf, 0:ncmp, :], NEG)
        mx = jnp.max(sm, axis=0, keepdims=True)
        e = jnp.where(mc, jnp.exp2(sm - mx), 0.0)
        l = jnp.sum(e, axis=0, keepdims=True)
        p = e / jnp.where(l > 0.0, l, 1.0)
        psum[gi] = p if psum[gi] is None else psum[gi] + p
        p_scr[slot % 2, 0:ncmp, :] = p.astype(BF16)
        if slot + QK_AHEAD < NSLOT:
            issue_cmp_scores(slot + QK_AHEAD)
        else:
            issue_scores(1, win_lo, slot + QK_AHEAD - NSLOT)
        if slot >= 1:
            cmp_values(slot - 1)
    cmp_values(NSLOT - 1)
    for gi in range(2):
        psum_scr[gi] = psum[gi]

    def normalized(br, slot):
        return (acc_scr[br, slot, 0:HEAD_DIM, :]
                / acc_scr[br, slot, HEAD_DIM:HEAD_DIM + 1, :])

    def tile(br, j, ahead, bias, side_work=None):
        def weighted_values(slot, alpha):
            acc_scr[br, slot] = alpha * acc_scr[br, slot] + _dot(
                vt_scr[br, j, slot // GROUP], p_scr[slot % 2])

        alphas = []
        for slot in range(NSLOT):
            s = s_scr[slot % nbuf]
            if bias is not None:
                s = s + bias
            m_prev = m_scr[br, slot]
            m_new = jnp.maximum(m_prev, jnp.max(s, axis=0, keepdims=True))
            alphas.append(jnp.exp2(m_prev - m_new))
            p_scr[slot % 2] = jnp.exp2(s - m_new).astype(BF16)
            m_scr[br, slot] = m_new
            if slot + QK_AHEAD < NSLOT:
                issue_scores(br, j, slot + QK_AHEAD)
            elif ahead is not None:
                issue_scores(ahead[0], ahead[1], slot + QK_AHEAD - NSLOT)
            if slot >= 1:
                weighted_values(slot - 1, alphas[slot - 1])
            if side_work is not None:
                side_work(slot)
        weighted_values(NSLOT - 1, alphas[NSLOT - 1])

    def sweep(br, j_lo, bias_fn, last_ahead, last_side_work):
        def body(j, carry):
            tile(br, j, (br, j + 1), bias_fn(j))
            return carry

        lax.fori_loop(j_lo, qt, body, 0)
        tile(br, qt, last_ahead, bias_scr[2], last_side_work)

    jidx = lax.broadcasted_iota(jnp.int32, (nsel, tq), 0)
    ntop = min(N_SELECT, nsel)
    topk = {}

    def topk_start():
        ovt = ovt_ref[...]
        cur = (q0 + lax.broadcasted_iota(jnp.int32, (nsel, tq), 1)) // SEL_BLOCK
        forced = (jidx == 0) | (jidx == cur) | (jidx == cur - 1)
        future = jidx > cur
        for gi in range(2):
            ps = psum_scr[gi]
            p1 = ps.astype(BF16)
            r1 = ps - p1.astype(F32)
            p2 = r1.astype(BF16)
            p3 = (r1 - p2.astype(F32)).astype(BF16)
            imp = _dot(ovt, p1) + _dot(ovt, p2) + _dot(ovt, p3)
            topk[gi] = (jnp.where(forced, FORCED_BONUS, jnp.where(future, -1.0, imp)),
                        jnp.full((nsel, tq), NEG, F32))

    def topk_round():
        for gi in range(2):
            score, selneg = topk[gi]
            mx = jnp.max(score, axis=0, keepdims=True)
            cand = jnp.where(score == mx, jidx, nsel)
            first = jnp.min(cand, axis=0, keepdims=True)
            hit = jidx == first
            topk[gi] = (jnp.where(hit, -3e38, score), jnp.where(hit, 0.0, selneg))

    def topk_finish():
        lane = lax.broadcasted_iota(jnp.int32, (tq, LANES), 1)
        for gi in range(2):
            f0 = _feature_base(gi) + SEL_LANE0
            parts = [topk[gi][1]]
            if f0 > 0:
                parts.insert(0, jnp.zeros((f0, tq), F32))
            if LANES - f0 - nsel > 0:
                parts.append(jnp.zeros((LANES - f0 - nsel, tq), F32))
            selq = jnp.concatenate(parts, axis=0).T.astype(BF16)
            in_sel = (lane >= f0) & (lane < f0 + nsel)
            for r in range(GROUP):
                slot = gi * GROUP + r
                qaug_scr[slot] = jnp.where(in_sel, selq, qaug_scr[slot])

    def topk_side_work(slot):
        last = NSLOT - QK_AHEAD - 1
        if slot == 0:
            topk_start()
        for i in range(ntop):
            if i * (last + 1) // ntop == slot:
                topk_round()
        if slot == last:
            topk_finish()

    m_scr[...] = jnp.full(m_scr.shape, NEG, F32)
    acc_scr[...] = jnp.zeros(acc_scr.shape, F32)
    sweep(1, win_lo, lambda j: bias_scr[j - qt + 2], (0, 0), topk_side_work)
    sweep(0, 0, lambda j: None, None, None)

    gtt = gt_ref[...].T
    for r in range(GROUP):
        comb = []
        for gi in range(2):
            slot = gi * GROUP + r
            comb.append(gtt[slot:slot + 1, :] * oc_scr[slot]
                        + gtt[NSLOT + slot:NSLOT + slot + 1, :] * normalized(0, slot)
                        + gtt[2 * NSLOT + slot:2 * NSLOT + slot + 1, :] * normalized(1, slot))
        col = jnp.concatenate(comb, axis=0).T
        o_ref[:, r * LANES:(r + 1) * LANES] = col.astype(o_ref.dtype)


def _nsa_attention(q, kv, kvc, gates, *, tq=256):
    B, S, _ = q.shape
    npair = N_KV // 2
    ncmp = kvc.shape[2]
    nsel = S // SEL_BLOCK
    assert S % tq == 0 and WINDOW == 2 * tq and nsel <= SEL_LANE0 + POS_LANE0
    cstart = np.arange(ncmp) * CMP_STRIDE
    selj = np.arange(nsel)
    ovt = ((cstart[None, :] < (selj[:, None] + 1) * SEL_BLOCK)
           & (cstart[None, :] + CMP_BLOCK > selj[:, None] * SEL_BLOCK))
    ovt = jnp.asarray(ovt, BF16)
    qf, fs, fw, fc = _attn_tables(S, ncmp)

    def kvspec(cb):
        return pl.BlockSpec((None, S, LANES), lambda b, p, t: (b, 0, cb + p))

    def cspec(s):
        return pl.BlockSpec((None, None, ncmp, LANES), lambda b, p, t: (b, s, 0, p))

    const = lambda shape: pl.BlockSpec(shape, lambda b, p, t: (0,) * len(shape))
    qw = GROUP * LANES
    ntile = S // tq
    return pl.pallas_call(
        functools.partial(_attn_kernel, tq=tq, seq=S),
        grid=(B, npair, S // tq),
        in_specs=[
            pl.BlockSpec((None, tq, qw), lambda b, p, t: (b, t, p)),
            kvspec(4), kvspec(6), kvspec(8), kvspec(10),
            cspec(0), cspec(1),
            pl.BlockSpec((None, tq, LANES), lambda b, p, t: (b, t, p)),
            const((nsel, ncmp)),
            pl.BlockSpec((None, NSLOT, LANES), lambda b, p, t: (p, 0, 0)),
            const((2, S, LANES)), const((2, S, LANES)), const((2, ncmp, LANES)),
        ],
        out_specs=pl.BlockSpec((None, tq, qw), lambda b, p, t: (b, t, p)),
        out_shape=jax.ShapeDtypeStruct((B, S, npair * qw), BF16),
        scratch_shapes=[
            pltpu.VMEM((NSLOT, tq, LANES), BF16),
            pltpu.VMEM((2, 2, ntile, tq, LANES), BF16),
            pltpu.VMEM((2, ntile, 2, HEAD_DIM + ONES_ROWS, tq), BF16),
            pltpu.VMEM((3, tq, tq), F32),
            pltpu.VMEM((2, NSLOT, 1, tq), F32),
            pltpu.VMEM((2, NSLOT, HEAD_DIM + ONES_ROWS, tq), F32),
            pltpu.VMEM((NSLOT, HEAD_DIM, tq), F32),
            pltpu.VMEM((2, ncmp, tq), F32),
            pltpu.VMEM((QK_AHEAD + 1, tq, tq), F32),
            pltpu.VMEM((2, tq, tq), BF16),
        ],
        compiler_params=_params(("arbitrary", "arbitrary", "arbitrary")),
        name="nsa_attention",
    )(q, kv, kv, kv, kv, kvc, kvc, gates, ovt, qf, fs, fw, fc)


def _head_pair_perm():
    perm = np.zeros(N_HEADS * HEAD_DIM, np.int32)
    d = np.arange(HEAD_DIM)
    for pair in range(N_KV // 2):
        for r in range(GROUP):
            for gi in range(2):
                h = (2 * pair + gi) * GROUP + r
                n0 = pair * GROUP * LANES + r * LANES + gi * HEAD_DIM
                perm[n0 + d] = h * HEAD_DIM + d
    return perm


def _nsa_layer(h, g, w_in, cmp_pos, cmp_w1, cmp_b1, cmp_w2, w_out):
    B, S, D = h.shape
    qcols = N_HEADS * HEAD_DIM
    kvcols = 3 * 2 * N_KV * HEAD_DIM
    ngate = 3 * N_HEADS
    perm = _head_pair_perm()
    wq = w_in[:, :qcols][:, perm].astype(BF16)
    wkv = w_in[:, qcols:qcols + kvcols].astype(BF16)
    wg_src = w_in[:, qcols + kvcols:]
    wg = jnp.zeros((D, (N_KV // 2) * LANES), F32)
    for pair in range(N_KV // 2):
        for br in range(3):
            src = br * N_HEADS + pair * NSLOT
            dst = pair * LANES + br * NSLOT
            wg = wg.at[:, dst:dst + NSLOT].set(wg_src[:, src:src + NSLOT])
    wg = wg.astype(BF16)

    q, kv, gates = _nsa_proj(h, g, wq, wkv, wg)

    ccols = 2 * N_KV * HEAD_DIM
    nrow = S // CMP_STRIDE
    t2 = kv[:, :, :ccols].reshape(B, S, 2, N_KV, HEAD_DIM).transpose(0, 2, 3, 1, 4)
    t2 = t2.reshape(B, 2, N_KV, nrow, CMP_STRIDE * HEAD_DIM)
    pos = jnp.broadcast_to(cmp_pos.reshape(2, 1, CMP_BLOCK * HEAD_DIM),
                           (2, SUBLANES, CMP_BLOCK * HEAD_DIM))
    hid = cmp_w1.shape[2]
    w2pad = jnp.zeros((2, N_KV, hid, N_KV * HEAD_DIM), F32)
    for gidx in range(N_KV):
        w2pad = w2pad.at[:, gidx, :, gidx * HEAD_DIM:(gidx + 1) * HEAD_DIM].set(cmp_w2)
    kvc = _nsa_compress(t2, pos, cmp_w1.astype(BF16), cmp_b1.reshape(2, 1, hid),
                        w2pad.astype(BF16))

    return _nsa_attention(q, kv, kvc, gates), w_out[perm, :].astype(BF16)


def kernel(x, lru_norm_g, lru_w_in, lru_conv_w, lru_conv_b, lru_gate_w, lru_gate_b, lru_a_param,
           lru_w_out, nsa_norm_g, nsa_w_in, nsa_cmp_pos, nsa_cmp_w1, nsa_cmp_b1, nsa_cmp_w2,
           nsa_w_out, ffn_norm_g, ffn_w_in, ffn_conv_w, ffn_conv_b, ffn_w_out, final_norm_g):
    h = _lru_layer(x, lru_norm_g[0], lru_w_in[0], lru_conv_w[0], lru_conv_b[0], lru_gate_w[0],
                   lru_gate_b[0], lru_a_param[0], lru_w_out[0])
    h = _ffn_layer(h, ffn_norm_g[0], ffn_w_in[0], ffn_conv_w[0], ffn_conv_b[0], ffn_w_out[0],
                   final_norm_g, final_norm=False)
    o, wo = _nsa_layer(h, nsa_norm_g[0], nsa_w_in[0], nsa_cmp_pos[0], nsa_cmp_w1[0],
                       nsa_cmp_b1[0], nsa_cmp_w2[0], nsa_w_out[0])
    return _ffn_layer(h, ffn_norm_g[1], ffn_w_in[1], ffn_conv_w[1], ffn_conv_b[1], ffn_w_out[1],
                      final_norm_g, final_norm=True, mix=o, wmix=wo)
```

```python
import functools
import math

import ml_dtypes
import numpy as np
import jax
import jax.numpy as jnp
from jax import lax
from jax.experimental import pallas as pl
from jax.experimental.pallas import tpu as pltpu

BF16 = jnp.bfloat16
F32 = jnp.float32

EPS = 1e-6
LRU_BLOCK_W = 128
LRU_CONV = 4
LRU_C = 8.0
N_HEADS = 16
HEAD_DIM = 64
N_KV = 4
GROUP = N_HEADS // N_KV
CMP_BLOCK = 32
CMP_STRIDE = 16
SEL_BLOCK = 64
N_SELECT = 8
WINDOW = 512
FORCED_BONUS = 1e4
NEG = -1e30
FFN_CONV = 3

LANES = 128
SUBLANES = 8
VMEM_LIMIT = 56 * 1024 * 1024


def _gelu(x):
    c = math.sqrt(2.0 / math.pi)
    inner = x * (c + (c * 0.044715) * (x * x))
    return (0.5 * x) * (1.0 + jnp.tanh(inner))


def _sigmoid(x):
    return 0.5 * jnp.tanh(0.5 * x) + 0.5


def _rmsnorm(x, g):
    return x * lax.rsqrt(jnp.mean(x * x, axis=-1, keepdims=True) + EPS) * g


def _dot(a, b):
    return jnp.dot(a, b, preferred_element_type=F32)


def _params(sem):
    return pltpu.CompilerParams(dimension_semantics=sem, vmem_limit_bytes=VMEM_LIMIT)


SCAN_CHUNK = SUBLANES * SUBLANES
PROJ_W = 2 * LANES


def _lru_kernel(x_ref, g_ref, wy_ref, wx_ref, cw_ref, cb_ref, gw_ref, gb_ref, ap_ref, wo_ref,
                o_ref, tail_scr, a_scr, b_scr, hc_scr, u_scr, yx_scr, *, tm, nblk):
    t = pl.program_id(1)

    @pl.when(t == 0)
    def _():
        tail_scr[...] = jnp.zeros(tail_scr.shape, F32)
        hc_scr[...] = jnp.zeros(hc_scr.shape, F32)

    x = x_ref[...]
    xn = _rmsnorm(x, g_ref[...]).astype(BF16)
    cw = cw_ref[...]
    cb = cb_ref[...]

    z = -ap_ref[...]
    c8 = -LRU_C * (jnp.maximum(z, 0.0) + jnp.log1p(jnp.exp(-jnp.abs(z))))
    gb = gb_ref[...]

    blk_per_chunk = PROJ_W // LANES
    nchunk = nblk // blk_per_chunk

    def project(i):
        cols = slice(i * PROJ_W, (i + 1) * PROJ_W)
        yx_scr[i % 2, 0] = _dot(xn, wy_ref[:, cols])
        yx_scr[i % 2, 1] = _dot(xn, wx_ref[:, cols])

    sub_w = lax.broadcasted_iota(jnp.int32, (SUBLANES, PROJ_W), 0)

    def conv(i):
        cols = slice(i * PROJ_W, (i + 1) * PROJ_W)
        xb = yx_scr[i % 2, 1]
        tail = tail_scr[:, cols]
        xc = cw[LRU_CONV - 1:LRU_CONV, cols] * xb + cb[:, cols]
        for d in range(1, LRU_CONV):
            rolled = pltpu.roll(xb, d, 0)
            top = jnp.where(sub_w < d, pltpu.roll(tail, d, 0), rolled[0:SUBLANES, :])
            shifted = jnp.concatenate([top, rolled[SUBLANES:, :]], axis=0)
            xc = xc + cw[LRU_CONV - 1 - d:LRU_CONV - d, cols] * shifted
        tail_scr[:, cols] = xb[tm - SUBLANES:tm, :]
        return xc

    sub = lax.broadcasted_iota(jnp.int32, (SUBLANES, LANES), 0)
    out_split = (nchunk // 2 + 1) * PROJ_W
    project(0)
    for n in range(nblk):
        i, k = divmod(n, blk_per_chunk)
        if k == 0:
            if i + 1 < nchunk:
                project(i + 1)
            if i * PROJ_W == out_split:
                o_ref[...] = x + _dot(u_scr[:, 0:out_split], wo_ref[0:out_split, :])
            xc = conv(i)
        lo, hi = n * LANES, (n + 1) * LANES
        xcn = xc[:, k * LANES:(k + 1) * LANES]
        xcb = xcn.astype(BF16)
        r = _sigmoid(_dot(xcb, gw_ref[0, n]) + gb[0:1, lo:hi])
        ig = _sigmoid(_dot(xcb, gw_ref[1, n]) + gb[1:2, lo:hi])
        log_a = c8[:, lo:hi] * r
        a = jnp.exp(log_a)
        w = -jnp.tanh(log_a) * (a * a + 1.0)
        mult = jnp.where(w > 0.0, w * lax.rsqrt(w), 0.0)
        a_scr[n] = a
        b_scr[n] = mult * ig * xcn

        carry = hc_scr[n]
        for c in range(tm // SCAN_CHUNK):
            base = c * SCAN_CHUNK
            acum, hloc = [], []
            for j in range(SUBLANES):
                aj = a_scr.at[n][pl.ds(base + j, SUBLANES, stride=SUBLANES), :]
                bj = b_scr.at[n][pl.ds(base + j, SUBLANES, stride=SUBLANES), :]
                if j == 0:
                    acum.append(aj)
                    hloc.append(bj)
                else:
                    hloc.append(aj * hloc[-1] + bj)
                    acum.append(aj * acum[-1])
            p, e = acum[-1], hloc[-1]
            for d in (1, 2, 4):
                keep = sub >= d
                psh = pltpu.roll(p, d, 0)
                esh = pltpu.roll(e, d, 0)
                e = jnp.where(keep, p * esh + e, e)
                p = jnp.where(keep, p * psh, p)
            hend = e + p * carry
            cin = jnp.where(sub == 0, carry, pltpu.roll(hend, 1, 0))
            for j in range(SUBLANES):
                b_scr.at[n][pl.ds(base + j, SUBLANES, stride=SUBLANES), :] = (
                    hloc[j] + acum[j] * cin)
            carry = jnp.broadcast_to(hend[SUBLANES - 1:SUBLANES, :], (SUBLANES, LANES))
        hc_scr[n] = carry
        y = _gelu(yx_scr[i % 2, 0, :, k * LANES:(k + 1) * LANES])
        u_scr[:, lo:hi] = (b_scr[n] * y).astype(BF16)

    o_ref[...] += _dot(u_scr[:, out_split:], wo_ref[out_split:, :])


def _lru_layer(h, g, w_in, conv_w, conv_b, gate_w, gate_b, a_param, w_out, *, tm=512):
    B, S, D = h.shape
    W = w_out.shape[0]
    nblk = W // LANES
    assert gate_w.shape == (2, nblk, LANES, LANES) and S % tm == 0 and tm % SCAN_CHUNK == 0
    wy = w_in[:, :W].astype(BF16)
    wx = w_in[:, W:].astype(BF16)
    const = lambda shape: pl.BlockSpec(shape, lambda b, t: (0,) * len(shape))
    return pl.pallas_call(
        functools.partial(_lru_kernel, tm=tm, nblk=nblk),
        grid=(B, S // tm),
        in_specs=[
            pl.BlockSpec((None, tm, D), lambda b, t: (b, t, 0)),
            const((1, D)), const((D, W)), const((D, W)), const((LRU_CONV, W)), const((1, W)),
            const((2, nblk, LANES, LANES)), const((2, W)), const((1, W)), const((W, D)),
        ],
        out_specs=pl.BlockSpec((None, tm, D), lambda b, t: (b, t, 0)),
        out_shape=jax.ShapeDtypeStruct((B, S, D), F32),
        scratch_shapes=[
            pltpu.VMEM((SUBLANES, W), F32),
            pltpu.VMEM((nblk, tm, LANES), F32),
            pltpu.VMEM((nblk, tm, LANES), F32),
            pltpu.VMEM((nblk, SUBLANES, LANES), F32),
            pltpu.VMEM((tm, W), BF16),
            pltpu.VMEM((2, 2, tm, PROJ_W), F32),
        ],
        compiler_params=_params(("arbitrary", "arbitrary")),
        name="lru_layer",
    )(h, g.reshape(1, D), wy, wx, conv_w, conv_b.reshape(1, W), gate_w.astype(BF16), gate_b,
      a_param.reshape(1, W), w_out.astype(BF16))


def _ffn_kernel(*refs, tm, final_norm, has_mix):
    if has_mix:
        mix_ref, wmix_ref, *refs = refs
    (x_ref, g_ref, wa_ref, wb_ref, cw_ref, cb_ref, wo_ref, fg_ref, o_ref,
     xn_scr, acc_scr, a_scr, halo_scr) = refs
    t = pl.program_id(1)
    f = pl.program_id(2)
    nf = pl.num_programs(2)

    @pl.when(f == 0)
    def _():
        x = x_ref[...]
        if has_mix:
            x = x + _dot(mix_ref[...], wmix_ref[...])
        xn_scr[...] = _rmsnorm(x, g_ref[...]).astype(BF16)
        acc_scr[...] = x

    @pl.when(t == 0)
    def _():
        halo_scr[f] = jnp.zeros(halo_scr.shape[1:], F32)

    xn = xn_scr[...]
    a = _dot(xn, wa_ref[...])
    b = _dot(xn, wb_ref[...])
    a_scr[0:SUBLANES, :] = halo_scr[f]
    a_scr[SUBLANES:SUBLANES + tm, :] = a
    cw = cw_ref[...]
    ac = (cw[2:3, :] * a + cw[1:2, :] * a_scr[7:7 + tm, :] + cw[0:1, :] * a_scr[6:6 + tm, :]
          + cb_ref[...])
    halo_scr[f] = a_scr[tm:tm + SUBLANES, :]
    hmid = (_gelu(ac) * b).astype(BF16)
    acc_scr[...] += _dot(hmid, wo_ref[...])

    @pl.when(f == nf - 1)
    def _():
        out = acc_scr[...]
        if final_norm:
            out = _rmsnorm(out, fg_ref[...])
        o_ref[...] = out


def _ffn_layer(h, g, w_in, conv_w, conv_b, w_out, final_g, *, final_norm, mix=None, wmix=None,
               tm=1024, tf=512):
    B, S, D = h.shape
    F = w_out.shape[0]
    assert S % tm == 0 and F % tf == 0
    nf = F // tf
    wa = w_in[:, :F].astype(BF16)
    wb = w_in[:, F:].astype(BF16)
    has_mix = mix is not None
    mix_specs, mix_args = [], []
    if has_mix:
        K = mix.shape[2]
        mix_specs = [pl.BlockSpec((None, tm, K), lambda b, t, f: (b, t, 0)),
                     pl.BlockSpec((K, D), lambda b, t, f: (0, 0))]
        mix_args = [mix, wmix]
    return pl.pallas_call(
        functools.partial(_ffn_kernel, tm=tm, final_norm=final_norm, has_mix=has_mix),
        grid=(B, S // tm, nf),
        in_specs=mix_specs + [
            pl.BlockSpec((None, tm, D), lambda b, t, f: (b, t, 0)),
            pl.BlockSpec((1, D), lambda b, t, f: (0, 0)),
            pl.BlockSpec((D, tf), lambda b, t, f: (0, f)),
            pl.BlockSpec((D, tf), lambda b, t, f: (0, f)),
            pl.BlockSpec((FFN_CONV, tf), lambda b, t, f: (0, f)),
            pl.BlockSpec((1, tf), lambda b, t, f: (0, f)),
            pl.BlockSpec((tf, D), lambda b, t, f: (f, 0)),
            pl.BlockSpec((1, D), lambda b, t, f: (0, 0)),
        ],
        out_specs=pl.BlockSpec((None, tm, D), lambda b, t, f: (b, t, 0)),
        out_shape=jax.ShapeDtypeStruct((B, S, D), F32),
        scratch_shapes=[
            pltpu.VMEM((tm, D), BF16),
            pltpu.VMEM((tm, D), F32),
            pltpu.VMEM((tm + SUBLANES, tf), F32),
            pltpu.VMEM((nf, SUBLANES, tf), F32),
        ],
        compiler_params=_params(("arbitrary", "arbitrary", "arbitrary")),
        name="ffn_layer",
    )(*mix_args, h, g.reshape(1, D), wa, wb, conv_w, conv_b.reshape(1, F), w_out.astype(BF16),
      final_g.reshape(1, D))


def _nsa_proj_kernel(x_ref, g_ref, wq_ref, wkc_ref, wkv_ref, wg_ref, q_ref, kc_ref, kv_ref, gt_ref):
    xn = _rmsnorm(x_ref[...], g_ref[...]).astype(BF16)
    q_ref[...] = (_dot(xn, wq_ref[...]) * (HEAD_DIM ** -0.5 * LOG2E)).astype(BF16)
    kc = _dot(xn, wkc_ref[...]).astype(BF16)
    for j in range(kc_ref.shape[0]):
        kc_ref[j] = kc[:, j * LANES:(j + 1) * LANES]
    kv_ref[...] = _dot(xn, wkv_ref[...]).astype(BF16)
    gt_ref[...] = _sigmoid(_dot(xn, wg_ref[...]))


def _nsa_proj(h, g, wq, wkc, wkv, wg, *, tm=512):
    B, S, D = h.shape
    nq, nkc, nkv, ng = wq.shape[1], wkc.shape[1], wkv.shape[1], wg.shape[1]
    nslab = nkc // LANES
    const = lambda shape: pl.BlockSpec(shape, lambda b, t: (0,) * len(shape))
    row = lambda n: pl.BlockSpec((None, tm, n), lambda b, t: (b, t, 0))
    return pl.pallas_call(
        _nsa_proj_kernel,
        grid=(B, S // tm),
        in_specs=[row(D), const((1, D)), const((D, nq)), const((D, nkc)), const((D, nkv)),
                  const((D, ng))],
        out_specs=[row(nq), pl.BlockSpec((None, nslab, tm, LANES), lambda b, t: (b, 0, t, 0)),
                   row(nkv), row(ng)],
        out_shape=[jax.ShapeDtypeStruct((B, S, nq), BF16),
                   jax.ShapeDtypeStruct((B, nslab, S, LANES), BF16),
                   jax.ShapeDtypeStruct((B, S, nkv), BF16),
                   jax.ShapeDtypeStruct((B, S, ng), F32)],
        compiler_params=_params(("arbitrary", "arbitrary")),
        name="nsa_proj",
    )(h, g.reshape(1, D), wq, wkc, wkv, wg)


def _cmp_kernel(t2_ref, pos_ref, wt_ref, wb_ref, b1_ref, w2_ref, o_ref):
    nb, nrow, K = t2_ref.shape
    t2 = t2_ref[...].reshape(nb * nrow, K)
    w_top = wt_ref[...]
    w_bot = wb_ref[...]
    pos = pos_ref[...].astype(BF16)
    bias = _dot(pos[:, :K], w_top) + _dot(pos[:, K:], w_bot) + b1_ref[...]
    hid = _dot(t2, w_top) + pltpu.roll(_dot(t2, w_bot), nb * nrow - 1, 0) + bias[0:1, :]
    out = _dot(_gelu(hid).astype(BF16), w2_ref[...])
    o_ref[...] = out.reshape(nb, nrow, out.shape[1]).astype(o_ref.dtype)


def _nsa_compress(t2, pos, w_top, w_bot, b1, w2, *, nb=4):
    B, nslab, nrow, K = t2.shape
    npair = nslab // 2
    H2 = w_top.shape[2]
    nb = math.gcd(nb, B)
    per_kv = lambda shape: pl.BlockSpec((None,) + shape, lambda s, p, b: (s,) + (0,) * len(shape))
    return pl.pallas_call(
        _cmp_kernel,
        grid=(2, npair, B // nb),
        in_specs=[
            pl.BlockSpec((nb, None, nrow, K), lambda s, p, b: (b, s * npair + p, 0, 0)),
            per_kv((SUBLANES, 2 * K)), per_kv((K, H2)), per_kv((K, H2)), per_kv((1, H2)),
            per_kv((H2, LANES)),
        ],
        out_specs=pl.BlockSpec((nb, None, nrow, LANES), lambda s, p, b: (b, s, 0, p)),
        out_shape=jax.ShapeDtypeStruct((B, 2, nrow, npair * LANES), BF16),
        compiler_params=_params(("arbitrary", "arbitrary", "arbitrary")),
        name="nsa_compress",
    )(t2, pos, w_top, w_bot, b1, w2)


NSLOT = 2 * GROUP
LOG2E = 1.4426950408889634
NPIECE = 4
SEL_LANE0 = 0
POS_LANE0 = 32
CMP_LANE0 = 40
QK_AHEAD = 3
ONES_ROWS = 16


def _bf16_pieces(x, n):
    out, r = [], np.float64(x)
    for _ in range(n):
        p = np.float64(np.asarray(r, np.float32).astype(ml_dtypes.bfloat16).astype(np.float32))
        out.append(p)
        r = r - p
    return out


def _feature_base(gi):
    return HEAD_DIM if gi == 0 else 0


def _attn_tables(S, ncmp, tq):
    npair = N_KV // 2
    qf = np.zeros((npair, NSLOT, HEAD_DIM, tq), np.float32)
    for pair in range(npair):
        for slot in range(NSLOT):
            h = NSLOT * pair + slot
            for i, p in enumerate(_bf16_pieces(2.0 ** (-(h + 1) / 2.0) * LOG2E, NPIECE)):
                qf[pair, slot, POS_LANE0 + 2 * i] = SEL_BLOCK * p
                qf[pair, slot, POS_LANE0 + 2 * i + 1] = p
                qf[pair, slot, CMP_LANE0 + i] = CMP_STRIDE * p
    pos = np.arange(S)
    fs = np.zeros((2, S, LANES), np.float32)
    fw = np.zeros((2, S, LANES), np.float32)
    fc = np.zeros((2, ncmp, LANES), np.float32)
    for gi in range(2):
        f0 = _feature_base(gi)
        fs[gi, pos, f0 + SEL_LANE0 + pos // SEL_BLOCK] = 1.0
        for i in range(NPIECE):
            for f in (fs, fw):
                f[gi, :, f0 + POS_LANE0 + 2 * i] = pos // SEL_BLOCK
                f[gi, :, f0 + POS_LANE0 + 2 * i + 1] = pos % SEL_BLOCK
            fc[gi, :, f0 + CMP_LANE0 + i] = np.arange(ncmp)
    return (jnp.asarray(qf, BF16), jnp.asarray(fs, BF16), jnp.asarray(fw, BF16),
            jnp.asarray(fc, BF16))


def _attn_kernel(q_ref, ks_ref, vs_ref, kw_ref, vw_ref, kc_ref, vc_ref, gt_ref, ovt_ref, qf_ref,
                 fs_ref, fw_ref, fc_ref, o_ref, qaug_scr, kaug_scr, vt_scr, bias_scr, m_scr,
                 acc_scr, oc_scr, psum_scr, s_scr, p_scr, *, tq, seq):
    tk = tq
    qt = pl.program_id(2)
    q0 = qt * tq
    ncmp = kc_ref.shape[0]
    nsel = seq // SEL_BLOCK
    ntile = seq // tk

    @pl.when(qt == 0)
    def _():
        own0 = lax.broadcasted_iota(jnp.int32, (tk, LANES), 1) < HEAD_DIM
        for br, k_ref, f_ref, v_ref in ((0, ks_ref, fs_ref, vs_ref), (1, kw_ref, fw_ref, vw_ref)):
            for j in range(ntile):
                rows = slice(j * tk, (j + 1) * tk)
                k2 = k_ref[rows, :]
                kaug_scr[br, 0, j] = jnp.where(own0, k2, f_ref[0, rows, :])
                kaug_scr[br, 1, j] = jnp.where(own0, f_ref[1, rows, :], k2)
                vt = v_ref[rows, :].astype(F32).T.astype(BF16)
                ones = jnp.ones((ONES_ROWS, tk), BF16)
                for gi in range(2):
                    vt_scr[br, j, gi] = jnp.concatenate(
                        [vt[gi * HEAD_DIM:(gi + 1) * HEAD_DIM, :], ones], axis=0)
        d = (lax.broadcasted_iota(jnp.int32, (tk, tq), 1)
             - lax.broadcasted_iota(jnp.int32, (tk, tq), 0))
        bias_scr[0] = jnp.where(d < 0, 0.0, NEG)
        bias_scr[1] = jnp.zeros((tk, tq), F32)
        bias_scr[2] = jnp.where(d >= 0, 0.0, NEG)

    for r in range(GROUP):
        qct = q_ref[:, r * LANES:(r + 1) * LANES].astype(F32).T.astype(BF16)
        qaug_scr[r] = jnp.concatenate([qct[0:HEAD_DIM, :], qf_ref[r]], axis=0)
        qaug_scr[GROUP + r] = jnp.concatenate([qf_ref[GROUP + r], qct[HEAD_DIM:, :]], axis=0)

    own0c = lax.broadcasted_iota(jnp.int32, (ncmp, LANES), 1) < HEAD_DIM
    kc2 = kc_ref[...]
    kcaug = [jnp.where(own0c, kc2, fc_ref[0]), jnp.where(own0c, fc_ref[1], kc2)]
    vct = vc_ref[...].astype(F32).T.astype(BF16)
    cend = lax.broadcasted_iota(jnp.int32, (ncmp, tq), 0) * CMP_STRIDE + (CMP_BLOCK - 1)
    mc = cend <= (q0 + lax.broadcasted_iota(jnp.int32, (ncmp, tq), 1))
    nbuf = s_scr.shape[0]
    assert NSLOT % nbuf == 0

    def issue_cmp_scores(slot):
        s_scr[slot % nbuf, 0:ncmp, :] = _dot(kcaug[slot // GROUP], qaug_scr[slot])

    def issue_scores(br, j, slot):
        s_scr[slot % nbuf] = _dot(kaug_scr[br, slot // GROUP, j], qaug_scr[slot])

    def cmp_values(slot):
        gi = slot // GROUP
        oc_scr[slot] = _dot(vct[gi * HEAD_DIM:(gi + 1) * HEAD_DIM, :], p_scr[slot % 2, 0:ncmp, :])

    win_lo = jnp.maximum(qt - 2, 0)
    for slot in range(QK_AHEAD):
        issue_cmp_scores(slot)
    psum = [None, None]
    for slot in range(NSLOT):
        gi = slot // GROUP
        sm = jnp.where(mc, s_scr[slot % nbuf, 0:ncmp, :], NEG)
        mx = jnp.max(sm, axis=0, keepdims=True)
        e = jnp.where(mc, jnp.exp2(sm - mx), 0.0)
        l = jnp.sum(e, axis=0, keepdims=True)
        p = e / jnp.where(l > 0.0, l, 1.0)
        psum[gi] = p if psum[gi] is None else psum[gi] + p
        p_scr[slot % 2, 0:ncmp, :] = p.astype(BF16)
        if slot + QK_AHEAD < NSLOT:
            issue_cmp_scores(slot + QK_AHEAD)
        else:
            issue_scores(1, win_lo, slot + QK_AHEAD - NSLOT)
        if slot >= 1:
            cmp_values(slot - 1)
    cmp_values(NSLOT - 1)
    for gi in range(2):
        psum_scr[gi] = psum[gi]

    def normalized(br, slot):
        return (acc_scr[br, slot, 0:HEAD_DIM, :]
                / acc_scr[br, slot, HEAD_DIM:HEAD_DIM + 1, :])

    def tile(br, j, ahead, bias, side_work=None):
        def weighted_values(slot, alpha):
            acc_scr[br, slot] = alpha * acc_scr[br, slot] + _dot(
                vt_scr[br, j, slot // GROUP], p_scr[slot % 2])

        alphas = []
        for slot in range(NSLOT):
            s = s_scr[slot % nbuf]
            if bias is not None:
                s = s + bias
            m_prev = m_scr[br, slot]
            m_new = jnp.maximum(m_prev, jnp.max(s, axis=0, keepdims=True))
            alphas.append(jnp.exp2(m_prev - m_new))
            p_scr[slot % 2] = jnp.exp2(s - m_new).astype(BF16)
            m_scr[br, slot] = m_new
            if slot + QK_AHEAD < NSLOT:
                issue_scores(br, j, slot + QK_AHEAD)
            elif ahead is not None:
                issue_scores(ahead[0], ahead[1], slot + QK_AHEAD - NSLOT)
            if slot >= 1:
                weighted_values(slot - 1, alphas[slot - 1])
            if side_work is not None:
                side_work(slot)
        weighted_values(NSLOT - 1, alphas[NSLOT - 1])

    def sweep(br, j_lo, bias_fn, last_ahead, last_side_work):
        def body(j, carry):
            tile(br, j, (br, j + 1), bias_fn(j))
            return carry

        lax.fori_loop(j_lo, qt, body, 0)
        tile(br, qt, last_ahead, bias_scr[2], last_side_work)

    jidx = lax.broadcasted_iota(jnp.int32, (nsel, tq), 0)
    ntop = min(N_SELECT, nsel)
    topk = {}

    def topk_start():
        ovt = ovt_ref[...]
        cur = (q0 + lax.broadcasted_iota(jnp.int32, (nsel, tq), 1)) // SEL_BLOCK
        forced = (jidx == 0) | (jidx == cur) | (jidx == cur - 1)
        future = jidx > cur
        for gi in range(2):
            ps = psum_scr[gi]
            p1 = ps.astype(BF16)
            r1 = ps - p1.astype(F32)
            p2 = r1.astype(BF16)
            p3 = (r1 - p2.astype(F32)).astype(BF16)
            imp = _dot(ovt, p1) + _dot(ovt, p2) + _dot(ovt, p3)
            topk[gi] = (jnp.where(forced, FORCED_BONUS, jnp.where(future, -1.0, imp)),
                        jnp.full((nsel, tq), NEG, F32))

    def topk_round():
        for gi in range(2):
            score, selneg = topk[gi]
            mx = jnp.max(score, axis=0, keepdims=True)
            cand = jnp.where(score == mx, jidx, nsel)
            first = jnp.min(cand, axis=0, keepdims=True)
            hit = jidx == first
            topk[gi] = (jnp.where(hit, -3e38, score), jnp.where(hit, 0.0, selneg))

    def topk_finish():
        for gi in range(2):
            f0 = _feature_base(gi) + SEL_LANE0
            selneg = topk[gi][1].astype(BF16)
            for r in range(GROUP):
                qaug_scr[gi * GROUP + r, f0:f0 + nsel, :] = selneg

    def topk_side_work(slot):
        last = NSLOT - QK_AHEAD - 1
        if slot == 0:
            topk_start()
        for i in range(ntop):
            if i * (last + 1) // ntop == slot:
                topk_round()
        if slot == last:
            topk_finish()

    m_scr[...] = jnp.full(m_scr.shape, NEG, F32)
    acc_scr[...] = jnp.zeros(acc_scr.shape, F32)
    sweep(1, win_lo, lambda j: bias_scr[j - qt + 2], (0, 0), topk_side_work)
    sweep(0, 0, lambda j: None, None, None)

    gtt = gt_ref[...].T
    for r in range(GROUP):
        comb = []
        for gi in range(2):
            slot = gi * GROUP + r
            comb.append(gtt[slot:slot + 1, :] * oc_scr[slot]
                        + gtt[NSLOT + slot:NSLOT + slot + 1, :] * normalized(0, slot)
                        + gtt[2 * NSLOT + slot:2 * NSLOT + slot + 1, :] * normalized(1, slot))
        col = jnp.concatenate(comb, axis=0).T
        o_ref[:, r * LANES:(r + 1) * LANES] = col.astype(o_ref.dtype)


def _nsa_attention(q, kv, kvc, gates, *, tq=256):
    B, S, _ = q.shape
    npair = N_KV // 2
    ncmp = kvc.shape[2]
    nsel = S // SEL_BLOCK
    assert S % tq == 0 and WINDOW == 2 * tq and nsel <= SEL_LANE0 + POS_LANE0
    cstart = np.arange(ncmp) * CMP_STRIDE
    selj = np.arange(nsel)
    ovt = ((cstart[None, :] < (selj[:, None] + 1) * SEL_BLOCK)
           & (cstart[None, :] + CMP_BLOCK > selj[:, None] * SEL_BLOCK))
    ovt = jnp.asarray(ovt, BF16)
    qf, fs, fw, fc = _attn_tables(S, ncmp, tq)

    def kvspec(cb):
        return pl.BlockSpec((None, S, LANES), lambda b, p, t: (b, 0, cb + p))

    def cspec(s):
        return pl.BlockSpec((None, None, ncmp, LANES), lambda b, p, t: (b, s, 0, p))

    const = lambda shape: pl.BlockSpec(shape, lambda b, p, t: (0,) * len(shape))
    qw = GROUP * LANES
    ntile = S // tq
    return pl.pallas_call(
        functools.partial(_attn_kernel, tq=tq, seq=S),
        grid=(B, npair, S // tq),
        in_specs=[
            pl.BlockSpec((None, tq, qw), lambda b, p, t: (b, t, p)),
            kvspec(0), kvspec(2), kvspec(4), kvspec(6),
            cspec(0), cspec(1),
            pl.BlockSpec((None, tq, LANES), lambda b, p, t: (b, t, p)),
            const((nsel, ncmp)),
            pl.BlockSpec((None, NSLOT, HEAD_DIM, tq), lambda b, p, t: (p, 0, 0, 0)),
            const((2, S, LANES)), const((2, S, LANES)), const((2, ncmp, LANES)),
        ],
        out_specs=pl.BlockSpec((None, tq, qw), lambda b, p, t: (b, t, p)),
        out_shape=jax.ShapeDtypeStruct((B, S, npair * qw), BF16),
        scratch_shapes=[
            pltpu.VMEM((NSLOT, LANES, tq), BF16),
            pltpu.VMEM((2, 2, ntile, tq, LANES), BF16),
            pltpu.VMEM((2, ntile, 2, HEAD_DIM + ONES_ROWS, tq), BF16),
            pltpu.VMEM((3, tq, tq), F32),
            pltpu.VMEM((2, NSLOT, 1, tq), F32),
            pltpu.VMEM((2, NSLOT, HEAD_DIM + ONES_ROWS, tq), F32),
            pltpu.VMEM((NSLOT, HEAD_DIM, tq), F32),
            pltpu.VMEM((2, ncmp, tq), F32),
            pltpu.VMEM((QK_AHEAD + 1, tq, tq), F32),
            pltpu.VMEM((2, tq, tq), BF16),
        ],
        compiler_params=_params(("arbitrary", "arbitrary", "arbitrary")),
        name="nsa_attention",
    )(q, kv, kv, kv, kv, kvc, kvc, gates, ovt, qf, fs, fw, fc)


def _head_pair_perm():
    perm = np.zeros(N_HEADS * HEAD_DIM, np.int32)
    d = np.arange(HEAD_DIM)
    for pair in range(N_KV // 2):
        for r in range(GROUP):
            for gi in range(2):
                h = (2 * pair + gi) * GROUP + r
                n0 = pair * GROUP * LANES + r * LANES + gi * HEAD_DIM
                perm[n0 + d] = h * HEAD_DIM + d
    return perm


def _nsa_layer(h, g, w_in, cmp_pos, cmp_w1, cmp_b1, cmp_w2, w_out):
    B, S, D = h.shape
    qcols = N_HEADS * HEAD_DIM
    kvcols = 3 * 2 * N_KV * HEAD_DIM
    ngate = 3 * N_HEADS
    perm = _head_pair_perm()
    ccols = 2 * N_KV * HEAD_DIM
    wq = w_in[:, :qcols][:, perm].astype(BF16)
    wkc = w_in[:, qcols:qcols + ccols].astype(BF16)
    wkv = w_in[:, qcols + ccols:qcols + kvcols].astype(BF16)
    wg_src = w_in[:, qcols + kvcols:]
    wg = jnp.zeros((D, (N_KV // 2) * LANES), F32)
    for pair in range(N_KV // 2):
        for br in range(3):
            src = br * N_HEADS + pair * NSLOT
            dst = pair * LANES + br * NSLOT
            wg = wg.at[:, dst:dst + NSLOT].set(wg_src[:, src:src + NSLOT])
    wg = wg.astype(BF16)

    q, kc, kv, gates = _nsa_proj(h, g, wq, wkc, wkv, wg)

    nrow = S // CMP_STRIDE
    t2 = kc.reshape(B, kc.shape[1], nrow, CMP_STRIDE * LANES)
    hid = cmp_w1.shape[2]
    eye = jnp.eye(2, dtype=F32)
    w1r = cmp_w1.reshape(2, 2, CMP_STRIDE, HEAD_DIM, hid)
    w1p = jnp.einsum('stidh,gk->stigdkh', w1r, eye).reshape(2, 2, CMP_STRIDE * LANES, 2 * hid)
    posr = cmp_pos.reshape(2, 2, CMP_STRIDE, 1, HEAD_DIM)
    posp = jnp.broadcast_to(posr, (2, 2, CMP_STRIDE, 2, HEAD_DIM)).reshape(2, 1, -1)
    posp = jnp.broadcast_to(posp, (2, SUBLANES, posp.shape[2]))
    b1p = jnp.tile(cmp_b1, (1, 2)).reshape(2, 1, 2 * hid)
    w2p = jnp.einsum('shd,gk->sghkd', cmp_w2, eye).reshape(2, 2 * hid, LANES)
    kvc = _nsa_compress(t2, posp, w1p[:, 0].astype(BF16), w1p[:, 1].astype(BF16), b1p,
                        w2p.astype(BF16))

    return _nsa_attention(q, kv, kvc, gates), w_out[perm, :].astype(BF16)


def kernel(x, lru_norm_g, lru_w_in, lru_conv_w, lru_conv_b, lru_gate_w, lru_gate_b, lru_a_param,
           lru_w_out, nsa_norm_g, nsa_w_in, nsa_cmp_pos, nsa_cmp_w1, nsa_cmp_b1, nsa_cmp_w2,
           nsa_w_out, ffn_norm_g, ffn_w_in, ffn_conv_w, ffn_conv_b, ffn_w_out, final_norm_g):
    h = _lru_layer(x, lru_norm_g[0], lru_w_in[0], lru_conv_w[0], lru_conv_b[0], lru_gate_w[0],
                   lru_gate_b[0], lru_a_param[0], lru_w_out[0])
    h = _ffn_layer(h, ffn_norm_g[0], ffn_w_in[0], ffn_conv_w[0], ffn_conv_b[0], ffn_w_out[0],
                   final_norm_g, final_norm=False)
    o, wo = _nsa_layer(h, nsa_norm_g[0], nsa_w_in[0], nsa_cmp_pos[0], nsa_cmp_w1[0],
                       nsa_cmp_b1[0], nsa_cmp_w2[0], nsa_w_out[0])
    return _ffn_layer(h, ffn_norm_g[1], ffn_w_in[1], ffn_conv_w[1], ffn_conv_b[1], ffn_w_out[1],
                      final_norm_g, final_norm=True, mix=o, wmix=wo)
```

```python
import functools
import math

import ml_dtypes
import numpy as np
import jax
import jax.numpy as jnp
from jax import lax
from jax.experimental import pallas as pl
from jax.experimental.pallas import tpu as pltpu

BF16 = jnp.bfloat16
F32 = jnp.float32

EPS = 1e-6
LRU_BLOCK_W = 128
LRU_CONV = 4
LRU_C = 8.0
N_HEADS = 16
HEAD_DIM = 64
N_KV = 4
GROUP = N_HEADS // N_KV
CMP_BLOCK = 32
CMP_STRIDE = 16
SEL_BLOCK = 64
N_SELECT = 8
WINDOW = 512
FORCED_BONUS = 1e4
NEG = -1e30
FFN_CONV = 3

LANES = 128
SUBLANES = 8
VMEM_LIMIT = 56 * 1024 * 1024


def _gelu(x):
    c = math.sqrt(2.0 / math.pi)
    inner = x * (c + (c * 0.044715) * (x * x))
    return (0.5 * x) * (1.0 + jnp.tanh(inner))


def _sigmoid(x):
    return 0.5 * jnp.tanh(0.5 * x) + 0.5


def _rmsnorm(x, g):
    return x * lax.rsqrt(jnp.mean(x * x, axis=-1, keepdims=True) + EPS) * g


def _dot(a, b):
    return jnp.dot(a, b, preferred_element_type=F32)


def _params(sem):
    return pltpu.CompilerParams(dimension_semantics=sem, vmem_limit_bytes=VMEM_LIMIT)


SCAN_CHUNK = SUBLANES * SUBLANES
PROJ_W = 2 * LANES


def _lru_kernel(x_ref, g_ref, wy_ref, wx_ref, cw_ref, cb_ref, gw_ref, gb_ref, ap_ref, wo_ref,
                o_ref, tail_scr, a_scr, b_scr, hc_scr, u_scr, yx_scr, *, tm, nblk):
    t = pl.program_id(1)

    @pl.when(t == 0)
    def _():
        tail_scr[...] = jnp.zeros(tail_scr.shape, F32)
        hc_scr[...] = jnp.zeros(hc_scr.shape, F32)

    x = x_ref[...]
    xn = _rmsnorm(x, g_ref[...]).astype(BF16)
    cw = cw_ref[...]
    cb = cb_ref[...]

    z = -ap_ref[...]
    c8 = -LRU_C * (jnp.maximum(z, 0.0) + jnp.log1p(jnp.exp(-jnp.abs(z))))
    gb = gb_ref[...]

    blk_per_chunk = PROJ_W // LANES
    nchunk = nblk // blk_per_chunk

    def project(i):
        cols = slice(i * PROJ_W, (i + 1) * PROJ_W)
        yx_scr[i % 2, 0] = _dot(xn, wy_ref[:, cols])
        yx_scr[i % 2, 1] = _dot(xn, wx_ref[:, cols])

    sub_w = lax.broadcasted_iota(jnp.int32, (SUBLANES, PROJ_W), 0)

    def conv(i):
        cols = slice(i * PROJ_W, (i + 1) * PROJ_W)
        xb = yx_scr[i % 2, 1]
        tail = tail_scr[:, cols]
        xc = cw[LRU_CONV - 1:LRU_CONV, cols] * xb + cb[:, cols]
        for d in range(1, LRU_CONV):
            rolled = pltpu.roll(xb, d, 0)
            top = jnp.where(sub_w < d, pltpu.roll(tail, d, 0), rolled[0:SUBLANES, :])
            shifted = jnp.concatenate([top, rolled[SUBLANES:, :]], axis=0)
            xc = xc + cw[LRU_CONV - 1 - d:LRU_CONV - d, cols] * shifted
        tail_scr[:, cols] = xb[tm - SUBLANES:tm, :]
        return xc

    sub = lax.broadcasted_iota(jnp.int32, (SUBLANES, LANES), 0)
    out_split = (nchunk // 2 + 1) * PROJ_W
    project(0)
    for n in range(nblk):
        i, k = divmod(n, blk_per_chunk)
        if k == 0:
            if i + 1 < nchunk:
                project(i + 1)
            if i * PROJ_W == out_split:
                o_ref[...] = x + _dot(u_scr[:, 0:out_split], wo_ref[0:out_split, :])
            xc = conv(i)
        lo, hi = n * LANES, (n + 1) * LANES
        xcn = xc[:, k * LANES:(k + 1) * LANES]
        xcb = xcn.astype(BF16)
        r = _sigmoid(_dot(xcb, gw_ref[0, n]) + gb[0:1, lo:hi])
        ig = _sigmoid(_dot(xcb, gw_ref[1, n]) + gb[1:2, lo:hi])
        log_a = c8[:, lo:hi] * r
        a = jnp.exp(log_a)
        w = -jnp.tanh(log_a) * (a * a + 1.0)
        mult = jnp.where(w > 0.0, w * lax.rsqrt(w), 0.0)
        a_scr[n] = a
        b_scr[n] = mult * ig * xcn

        carry = hc_scr[n]
        for c in range(tm // SCAN_CHUNK):
            base = c * SCAN_CHUNK
            acum, hloc = [], []
            for j in range(SUBLANES):
                aj = a_scr.at[n][pl.ds(base + j, SUBLANES, stride=SUBLANES), :]
                bj = b_scr.at[n][pl.ds(base + j, SUBLANES, stride=SUBLANES), :]
                if j == 0:
                    acum.append(aj)
                    hloc.append(bj)
                else:
                    hloc.append(aj * hloc[-1] + bj)
                    acum.append(aj * acum[-1])
            p, e = acum[-1], hloc[-1]
            for d in (1, 2, 4):
                keep = sub >= d
                psh = pltpu.roll(p, d, 0)
                esh = pltpu.roll(e, d, 0)
                e = jnp.where(keep, p * esh + e, e)
                p = jnp.where(keep, p * psh, p)
            hend = e + p * carry
            cin = jnp.where(sub == 0, carry, pltpu.roll(hend, 1, 0))
            for j in range(SUBLANES):
                b_scr.at[n][pl.ds(base + j, SUBLANES, stride=SUBLANES), :] = (
                    hloc[j] + acum[j] * cin)
            carry = jnp.broadcast_to(hend[SUBLANES - 1:SUBLANES, :], (SUBLANES, LANES))
        hc_scr[n] = carry
        y = _gelu(yx_scr[i % 2, 0, :, k * LANES:(k + 1) * LANES])
        u_scr[:, lo:hi] = (b_scr[n] * y).astype(BF16)

    o_ref[...] += _dot(u_scr[:, out_split:], wo_ref[out_split:, :])


def _lru_layer(h, g, w_in, conv_w, conv_b, gate_w, gate_b, a_param, w_out, *, tm=512):
    B, S, D = h.shape
    W = w_out.shape[0]
    nblk = W // LANES
    assert gate_w.shape == (2, nblk, LANES, LANES) and S % tm == 0 and tm % SCAN_CHUNK == 0
    wy = w_in[:, :W].astype(BF16)
    wx = w_in[:, W:].astype(BF16)
    const = lambda shape: pl.BlockSpec(shape, lambda b, t: (0,) * len(shape))
    return pl.pallas_call(
        functools.partial(_lru_kernel, tm=tm, nblk=nblk),
        grid=(B, S // tm),
        in_specs=[
            pl.BlockSpec((None, tm, D), lambda b, t: (b, t, 0)),
            const((1, D)), const((D, W)), const((D, W)), const((LRU_CONV, W)), const((1, W)),
            const((2, nblk, LANES, LANES)), const((2, W)), const((1, W)), const((W, D)),
        ],
        out_specs=pl.BlockSpec((None, tm, D), lambda b, t: (b, t, 0)),
        out_shape=jax.ShapeDtypeStruct((B, S, D), F32),
        scratch_shapes=[
            pltpu.VMEM((SUBLANES, W), F32),
            pltpu.VMEM((nblk, tm, LANES), F32),
            pltpu.VMEM((nblk, tm, LANES), F32),
            pltpu.VMEM((nblk, SUBLANES, LANES), F32),
            pltpu.VMEM((tm, W), BF16),
            pltpu.VMEM((2, 2, tm, PROJ_W), F32),
        ],
        compiler_params=_params(("arbitrary", "arbitrary")),
        name="lru_layer",
    )(h, g.reshape(1, D), wy, wx, conv_w, conv_b.reshape(1, W), gate_w.astype(BF16), gate_b,
      a_param.reshape(1, W), w_out.astype(BF16))


def _ffn_kernel(*refs, tm, final_norm, has_mix):
    if has_mix:
        mix_ref, wmix_ref, *refs = refs
    (x_ref, g_ref, wa_ref, wb_ref, cw_ref, cb_ref, wo_ref, fg_ref, o_ref,
     xn_scr, acc_scr, a_scr, halo_scr) = refs
    t = pl.program_id(1)
    f = pl.program_id(2)
    nf = pl.num_programs(2)

    @pl.when(f == 0)
    def _():
        x = x_ref[...]
        if has_mix:
            x = x + _dot(mix_ref[...], wmix_ref[...])
        xn_scr[...] = _rmsnorm(x, g_ref[...]).astype(BF16)
        acc_scr[...] = x

    @pl.when(t == 0)
    def _():
        halo_scr[f] = jnp.zeros(halo_scr.shape[1:], F32)

    xn = xn_scr[...]
    a = _dot(xn, wa_ref[...])
    b = _dot(xn, wb_ref[...])
    a_scr[0:SUBLANES, :] = halo_scr[f]
    a_scr[SUBLANES:SUBLANES + tm, :] = a
    cw = cw_ref[...]
    ac = (cw[2:3, :] * a + cw[1:2, :] * a_scr[7:7 + tm, :] + cw[0:1, :] * a_scr[6:6 + tm, :]
          + cb_ref[...])
    halo_scr[f] = a_scr[tm:tm + SUBLANES, :]
    hmid = (_gelu(ac) * b).astype(BF16)
    acc_scr[...] += _dot(hmid, wo_ref[...])

    @pl.when(f == nf - 1)
    def _():
        out = acc_scr[...]
        if final_norm:
            out = _rmsnorm(out, fg_ref[...])
        o_ref[...] = out


def _ffn_layer(h, g, w_in, conv_w, conv_b, w_out, final_g, *, final_norm, mix=None, wmix=None,
               tm=1024, tf=512):
    B, S, D = h.shape
    F = w_out.shape[0]
    assert S % tm == 0 and F % tf == 0
    nf = F // tf
    wa = w_in[:, :F].astype(BF16)
    wb = w_in[:, F:].astype(BF16)
    has_mix = mix is not None
    mix_specs, mix_args = [], []
    if has_mix:
        K = mix.shape[2]
        mix_specs = [pl.BlockSpec((None, tm, K), lambda b, t, f: (b, t, 0)),
                     pl.BlockSpec((K, D), lambda b, t, f: (0, 0))]
        mix_args = [mix, wmix]
    return pl.pallas_call(
        functools.partial(_ffn_kernel, tm=tm, final_norm=final_norm, has_mix=has_mix),
        grid=(B, S // tm, nf),
        in_specs=mix_specs + [
            pl.BlockSpec((None, tm, D), lambda b, t, f: (b, t, 0)),
            pl.BlockSpec((1, D), lambda b, t, f: (0, 0)),
            pl.BlockSpec((D, tf), lambda b, t, f: (0, f)),
            pl.BlockSpec((D, tf), lambda b, t, f: (0, f)),
            pl.BlockSpec((FFN_CONV, tf), lambda b, t, f: (0, f)),
            pl.BlockSpec((1, tf), lambda b, t, f: (0, f)),
            pl.BlockSpec((tf, D), lambda b, t, f: (f, 0)),
            pl.BlockSpec((1, D), lambda b, t, f: (0, 0)),
        ],
        out_specs=pl.BlockSpec((None, tm, D), lambda b, t, f: (b, t, 0)),
        out_shape=jax.ShapeDtypeStruct((B, S, D), F32),
        scratch_shapes=[
            pltpu.VMEM((tm, D), BF16),
            pltpu.VMEM((tm, D), F32),
            pltpu.VMEM((tm + SUBLANES, tf), F32),
            pltpu.VMEM((nf, SUBLANES, tf), F32),
        ],
        compiler_params=_params(("arbitrary", "arbitrary", "arbitrary")),
        name="ffn_layer",
    )(*mix_args, h, g.reshape(1, D), wa, wb, conv_w, conv_b.reshape(1, F), w_out.astype(BF16),
      final_g.reshape(1, D))


def _nsa_proj_kernel(x_ref, g_ref, wq_ref, wkc_ref, wkv_ref, wg_ref, q_ref, kc_ref, kv_ref, gt_ref):
    xn = _rmsnorm(x_ref[...], g_ref[...]).astype(BF16)
    q_ref[...] = (_dot(xn, wq_ref[...]) * (HEAD_DIM ** -0.5 * LOG2E)).astype(BF16)
    kc = _dot(xn, wkc_ref[...]).astype(BF16)
    for j in range(kc_ref.shape[0]):
        kc_ref[j] = kc[:, j * LANES:(j + 1) * LANES]
    kv_ref[...] = _dot(xn, wkv_ref[...]).astype(BF16)
    gt_ref[...] = _sigmoid(_dot(xn, wg_ref[...]))


def _nsa_proj(h, g, wq, wkc, wkv, wg, *, tm=512):
    B, S, D = h.shape
    nq, nkc, nkv, ng = wq.shape[1], wkc.shape[1], wkv.shape[1], wg.shape[1]
    nslab = nkc // LANES
    const = lambda shape: pl.BlockSpec(shape, lambda b, t: (0,) * len(shape))
    row = lambda n: pl.BlockSpec((None, tm, n), lambda b, t: (b, t, 0))
    return pl.pallas_call(
        _nsa_proj_kernel,
        grid=(B, S // tm),
        in_specs=[row(D), const((1, D)), const((D, nq)), const((D, nkc)), const((D, nkv)),
                  const((D, ng))],
        out_specs=[row(nq), pl.BlockSpec((None, nslab, tm, LANES), lambda b, t: (b, 0, t, 0)),
                   row(nkv), row(ng)],
        out_shape=[jax.ShapeDtypeStruct((B, S, nq), BF16),
                   jax.ShapeDtypeStruct((B, nslab, S, LANES), BF16),
                   jax.ShapeDtypeStruct((B, S, nkv), BF16),
                   jax.ShapeDtypeStruct((B, S, ng), F32)],
        compiler_params=_params(("arbitrary", "arbitrary")),
        name="nsa_proj",
    )(h, g.reshape(1, D), wq, wkc, wkv, wg)


def _cmp_kernel(t2_ref, pos_ref, wt_ref, wb_ref, b1_ref, w2_ref, o_ref):
    nb, nrow, K = t2_ref.shape
    t2 = t2_ref[...].reshape(nb * nrow, K)
    w_top = wt_ref[...]
    w_bot = wb_ref[...]
    pos = pos_ref[...].astype(BF16)
    bias = _dot(pos[:, :K], w_top) + _dot(pos[:, K:], w_bot) + b1_ref[...]
    hid = _dot(t2, w_top) + pltpu.roll(_dot(t2, w_bot), nb * nrow - 1, 0) + bias[0:1, :]
    out = _dot(_gelu(hid).astype(BF16), w2_ref[...])
    o_ref[...] = out.reshape(nb, nrow, out.shape[1]).astype(o_ref.dtype)


def _nsa_compress(t2, pos, w_top, w_bot, b1, w2, *, nb=4):
    B, nslab, nrow, K = t2.shape
    npair = nslab // 2
    H2 = w_top.shape[2]
    nb = math.gcd(nb, B)
    per_kv = lambda shape: pl.BlockSpec((None,) + shape, lambda s, p, b: (s,) + (0,) * len(shape))
    return pl.pallas_call(
        _cmp_kernel,
        grid=(2, npair, B // nb),
        in_specs=[
            pl.BlockSpec((nb, None, nrow, K), lambda s, p, b: (b, s * npair + p, 0, 0)),
            per_kv((SUBLANES, 2 * K)), per_kv((K, H2)), per_kv((K, H2)), per_kv((1, H2)),
            per_kv((H2, LANES)),
        ],
        out_specs=pl.BlockSpec((nb, None, nrow, LANES), lambda s, p, b: (b, s, 0, p)),
        out_shape=jax.ShapeDtypeStruct((B, 2, nrow, npair * LANES), BF16),
        compiler_params=_params(("arbitrary", "arbitrary", "arbitrary")),
        name="nsa_compress",
    )(t2, pos, w_top, w_bot, b1, w2)


NSLOT = 2 * GROUP
LOG2E = 1.4426950408889634
NPIECE = 4
SEL_LANE0 = 0
POS_LANE0 = 32
CMP_LANE0 = 40
QK_AHEAD = 3
ONES_ROWS = 16


def _bf16_pieces(x, n):
    out, r = [], np.float64(x)
    for _ in range(n):
        p = np.float64(np.asarray(r, np.float32).astype(ml_dtypes.bfloat16).astype(np.float32))
        out.append(p)
        r = r - p
    return out


def _feature_base(gi):
    return HEAD_DIM if gi == 0 else 0


def _attn_tables(S, ncmp, tq):
    npair = N_KV // 2
    qf = np.zeros((npair, NSLOT, HEAD_DIM, tq), np.float32)
    for pair in range(npair):
        for slot in range(NSLOT):
            h = NSLOT * pair + slot
            for i, p in enumerate(_bf16_pieces(2.0 ** (-(h + 1) / 2.0) * LOG2E, NPIECE)):
                qf[pair, slot, POS_LANE0 + 2 * i] = SEL_BLOCK * p
                qf[pair, slot, POS_LANE0 + 2 * i + 1] = p
                qf[pair, slot, CMP_LANE0 + i] = CMP_STRIDE * p
    pos = np.arange(S)
    fs = np.zeros((2, S, LANES), np.float32)
    fw = np.zeros((2, S, LANES), np.float32)
    fc = np.zeros((2, ncmp, LANES), np.float32)
    for gi in range(2):
        f0 = _feature_base(gi)
        fs[gi, pos, f0 + SEL_LANE0 + pos // SEL_BLOCK] = 1.0
        for i in range(NPIECE):
            for f in (fs, fw):
                f[gi, :, f0 + POS_LANE0 + 2 * i] = pos // SEL_BLOCK
                f[gi, :, f0 + POS_LANE0 + 2 * i + 1] = pos % SEL_BLOCK
            fc[gi, :, f0 + CMP_LANE0 + i] = np.arange(ncmp)
    return (jnp.asarray(qf, BF16), jnp.asarray(fs, BF16), jnp.asarray(fw, BF16),
            jnp.asarray(fc, BF16))


def _attn_kernel(q_ref, ks_ref, vs_ref, kw_ref, vw_ref, kc_ref, vc_ref, gt_ref, ovt_ref, qf_ref,
                 fs_ref, fw_ref, fc_ref, o_ref, qaug_scr, kaug_scr, vt_scr, bias_scr, m_scr,
                 acc_scr, oc_scr, psum_scr, s_scr, p_scr, next_scr, *, tq, seq):
    tk = tq
    qt = pl.program_id(2)
    q0 = qt * tq
    ncmp = kc_ref.shape[0]
    nsel = seq // SEL_BLOCK
    ntile = seq // tk

    @pl.when(qt == 0)
    def _():
        own0 = lax.broadcasted_iota(jnp.int32, (tk, LANES), 1) < HEAD_DIM
        for br, k_ref, f_ref, v_ref in ((0, ks_ref, fs_ref, vs_ref), (1, kw_ref, fw_ref, vw_ref)):
            for j in range(ntile):
                rows = slice(j * tk, (j + 1) * tk)
                k2 = k_ref[rows, :]
                kaug_scr[br, 0, j] = jnp.where(own0, k2, f_ref[0, rows, :])
                kaug_scr[br, 1, j] = jnp.where(own0, f_ref[1, rows, :], k2)
                vt = v_ref[rows, :].astype(F32).T.astype(BF16)
                ones = jnp.ones((ONES_ROWS, tk), BF16)
                for gi in range(2):
                    vt_scr[br, j, gi] = jnp.concatenate(
                        [vt[gi * HEAD_DIM:(gi + 1) * HEAD_DIM, :], ones], axis=0)
        d = (lax.broadcasted_iota(jnp.int32, (tk, tq), 1)
             - lax.broadcasted_iota(jnp.int32, (tk, tq), 0))
        bias_scr[0] = jnp.where(d < 0, 0.0, NEG)
        bias_scr[1] = jnp.zeros((tk, tq), F32)
        bias_scr[2] = jnp.where(d >= 0, 0.0, NEG)

    for r in range(GROUP):
        qct = q_ref[:, r * LANES:(r + 1) * LANES].astype(F32).T.astype(BF16)
        qaug_scr[r] = jnp.concatenate([qct[0:HEAD_DIM, :], qf_ref[r]], axis=0)
        qaug_scr[GROUP + r] = jnp.concatenate([qf_ref[GROUP + r], qct[HEAD_DIM:, :]], axis=0)

    own0c = lax.broadcasted_iota(jnp.int32, (ncmp, LANES), 1) < HEAD_DIM
    kc2 = kc_ref[...]
    kcaug = [jnp.where(own0c, kc2, fc_ref[0]), jnp.where(own0c, fc_ref[1], kc2)]
    vct = vc_ref[...].astype(F32).T.astype(BF16)
    cend = lax.broadcasted_iota(jnp.int32, (ncmp, tq), 0) * CMP_STRIDE + (CMP_BLOCK - 1)
    mc = cend <= (q0 + lax.broadcasted_iota(jnp.int32, (ncmp, tq), 1))
    nbuf = s_scr.shape[0]
    assert NSLOT % nbuf == 0

    def issue_cmp_scores(slot):
        s_scr[slot % nbuf, 0:ncmp, :] = _dot(kcaug[slot // GROUP], qaug_scr[slot])

    def issue_scores(br, j, slot):
        s_scr[slot % nbuf] = _dot(kaug_scr[br, slot // GROUP, j], qaug_scr[slot])

    def cmp_values(slot):
        gi = slot // GROUP
        oc_scr[slot] = _dot(vct[gi * HEAD_DIM:(gi + 1) * HEAD_DIM, :], p_scr[slot % 2, 0:ncmp, :])

    win_lo = jnp.maximum(qt - 2, 0)
    for slot in range(QK_AHEAD):
        issue_cmp_scores(slot)
    psum = [None, None]
    for slot in range(NSLOT):
        gi = slot // GROUP
        sm = jnp.where(mc, s_scr[slot % nbuf, 0:ncmp, :], NEG)
        mx = jnp.max(sm, axis=0, keepdims=True)
        e = jnp.where(mc, jnp.exp2(sm - mx), 0.0)
        l = jnp.sum(e, axis=0, keepdims=True)
        p = e / jnp.where(l > 0.0, l, 1.0)
        psum[gi] = p if psum[gi] is None else psum[gi] + p
        p_scr[slot % 2, 0:ncmp, :] = p.astype(BF16)
        if slot + QK_AHEAD < NSLOT:
            issue_cmp_scores(slot + QK_AHEAD)
        else:
            issue_scores(1, win_lo, slot + QK_AHEAD - NSLOT)
        if slot >= 1:
            cmp_values(slot - 1)
    cmp_values(NSLOT - 1)
    for gi in range(2):
        psum_scr[gi] = psum[gi]

    def normalized(br, slot):
        return (acc_scr[br, slot, 0:HEAD_DIM, :]
                / acc_scr[br, slot, HEAD_DIM:HEAD_DIM + 1, :])

    def tile(br, j, ahead, bias, side_work=None):
        def weighted_values(slot, alpha):
            acc_scr[br, slot] = alpha * acc_scr[br, slot] + _dot(
                vt_scr[br, j, slot // GROUP], p_scr[slot % 2])

        alphas = []
        for slot in range(NSLOT):
            s = s_scr[slot % nbuf]
            if bias is not None:
                s = s + bias
            m_prev = m_scr[br, slot]
            m_new = jnp.maximum(m_prev, jnp.max(s, axis=0, keepdims=True))
            alphas.append(jnp.exp2(m_prev - m_new))
            p_scr[slot % 2] = jnp.exp2(s - m_new).astype(BF16)
            m_scr[br, slot] = m_new
            if slot + QK_AHEAD < NSLOT:
                issue_scores(br, j, slot + QK_AHEAD)
            elif ahead is not None:
                issue_scores(ahead[0], ahead[1], slot + QK_AHEAD - NSLOT)
            if slot >= 1:
                weighted_values(slot - 1, alphas[slot - 1])
            if side_work is not None:
                side_work(slot)
        weighted_values(NSLOT - 1, alphas[NSLOT - 1])

    def sweep(br, j_lo, bias_fn, last_ahead, last_side_work):
        def body(j, carry):
            tile(br, j, (br, j + 1), bias_fn(j))
            return carry

        lax.fori_loop(j_lo, qt, body, 0)
        tile(br, qt, last_ahead, bias_scr[2], last_side_work)

    jidx = lax.broadcasted_iota(jnp.int32, (nsel, tq), 0)
    ntop = min(N_SELECT, nsel)
    topk = {}

    def topk_start():
        ovt = ovt_ref[...]
        cur = (q0 + lax.broadcasted_iota(jnp.int32, (nsel, tq), 1)) // SEL_BLOCK
        forced = (jidx == 0) | (jidx == cur) | (jidx == cur - 1)
        future = jidx > cur
        for gi in range(2):
            ps = psum_scr[gi]
            p1 = ps.astype(BF16)
            r1 = ps - p1.astype(F32)
            p2 = r1.astype(BF16)
            p3 = (r1 - p2.astype(F32)).astype(BF16)
            imp = _dot(ovt, p1) + _dot(ovt, p2) + _dot(ovt, p3)
            topk[gi] = (jnp.where(forced, FORCED_BONUS, jnp.where(future, -1.0, imp)),
                        jnp.full((nsel, tq), NEG, F32))

    def topk_round():
        for gi in range(2):
            score, selneg = topk[gi]
            mx = jnp.max(score, axis=0, keepdims=True)
            cand = jnp.where(score == mx, jidx, nsel)
            first = jnp.min(cand, axis=0, keepdims=True)
            hit = jidx == first
            topk[gi] = (jnp.where(hit, -3e38, score), jnp.where(hit, 0.0, selneg))

    def topk_finish():
        for gi in range(2):
            f0 = _feature_base(gi) + SEL_LANE0
            selneg = topk[gi][1].astype(BF16)
            for r in range(GROUP):
                qaug_scr[gi * GROUP + r, f0:f0 + nsel, :] = selneg
        chosen = jnp.max(jnp.maximum(topk[0][1], topk[1][1]), axis=1, keepdims=True)
        blk_per_tile = tk // SEL_BLOCK
        nxt = qt
        count = jnp.int32(0)
        for j in reversed(range(ntile)):
            next_scr[j] = nxt
            hit = jnp.max(chosen[j * blk_per_tile:(j + 1) * blk_per_tile, :]) == 0.0
            if j == 0:
                hit = True
            take = hit & (j < qt)
            nxt = jnp.where(take, j, nxt)
            count = count + take.astype(jnp.int32)
        topk['count'] = count

    def topk_side_work(slot):
        last = NSLOT - QK_AHEAD - 1
        if slot == 0:
            topk_start()
        for i in range(ntop):
            if i * (last + 1) // ntop == slot:
                topk_round()
        if slot == last:
            topk_finish()

    m_scr[...] = jnp.full(m_scr.shape, NEG, F32)
    acc_scr[...] = jnp.zeros(acc_scr.shape, F32)
    sweep(1, win_lo, lambda j: bias_scr[j - qt + 2], (0, 0), topk_side_work)

    def chosen_tile(_, j):
        j_next = next_scr[j]
        tile(0, j, (0, j_next), None)
        return j_next

    lax.fori_loop(0, topk['count'], chosen_tile, jnp.int32(0))
    tile(0, qt, None, bias_scr[2])

    gtt = gt_ref[...].T
    for r in range(GROUP):
        comb = []
        for gi in range(2):
            slot = gi * GROUP + r
            comb.append(gtt[slot:slot + 1, :] * oc_scr[slot]
                        + gtt[NSLOT + slot:NSLOT + slot + 1, :] * normalized(0, slot)
                        + gtt[2 * NSLOT + slot:2 * NSLOT + slot + 1, :] * normalized(1, slot))
        col = jnp.concatenate(comb, axis=0).T
        o_ref[:, r * LANES:(r + 1) * LANES] = col.astype(o_ref.dtype)


def _nsa_attention(q, kv, kvc, gates, *, tq=256):
    B, S, _ = q.shape
    npair = N_KV // 2
    ncmp = kvc.shape[2]
    nsel = S // SEL_BLOCK
    assert S % tq == 0 and WINDOW == 2 * tq and nsel <= SEL_LANE0 + POS_LANE0
    cstart = np.arange(ncmp) * CMP_STRIDE
    selj = np.arange(nsel)
    ovt = ((cstart[None, :] < (selj[:, None] + 1) * SEL_BLOCK)
           & (cstart[None, :] + CMP_BLOCK > selj[:, None] * SEL_BLOCK))
    ovt = jnp.asarray(ovt, BF16)
    qf, fs, fw, fc = _attn_tables(S, ncmp, tq)

    def kvspec(cb):
        return pl.BlockSpec((None, S, LANES), lambda b, p, t: (b, 0, cb + p))

    def cspec(s):
        return pl.BlockSpec((None, None, ncmp, LANES), lambda b, p, t: (b, s, 0, p))

    const = lambda shape: pl.BlockSpec(shape, lambda b, p, t: (0,) * len(shape))
    qw = GROUP * LANES
    ntile = S // tq
    return pl.pallas_call(
        functools.partial(_attn_kernel, tq=tq, seq=S),
        grid=(B, npair, S // tq),
        in_specs=[
            pl.BlockSpec((None, tq, qw), lambda b, p, t: (b, t, p)),
            kvspec(0), kvspec(2), kvspec(4), kvspec(6),
            cspec(0), cspec(1),
            pl.BlockSpec((None, tq, LANES), lambda b, p, t: (b, t, p)),
            const((nsel, ncmp)),
            pl.BlockSpec((None, NSLOT, HEAD_DIM, tq), lambda b, p, t: (p, 0, 0, 0)),
            const((2, S, LANES)), const((2, S, LANES)), const((2, ncmp, LANES)),
        ],
        out_specs=pl.BlockSpec((None, tq, qw), lambda b, p, t: (b, t, p)),
        out_shape=jax.ShapeDtypeStruct((B, S, npair * qw), BF16),
        scratch_shapes=[
            pltpu.VMEM((NSLOT, LANES, tq), BF16),
            pltpu.VMEM((2, 2, ntile, tq, LANES), BF16),
            pltpu.VMEM((2, ntile, 2, HEAD_DIM + ONES_ROWS, tq), BF16),
            pltpu.VMEM((3, tq, tq), F32),
            pltpu.VMEM((2, NSLOT, 1, tq), F32),
            pltpu.VMEM((2, NSLOT, HEAD_DIM + ONES_ROWS, tq), F32),
            pltpu.VMEM((NSLOT, HEAD_DIM, tq), F32),
            pltpu.VMEM((2, ncmp, tq), F32),
            pltpu.VMEM((QK_AHEAD + 1, tq, tq), F32),
            pltpu.VMEM((2, tq, tq), BF16),
            pltpu.SMEM((ntile,), jnp.int32),
        ],
        compiler_params=_params(("arbitrary", "arbitrary", "arbitrary")),
        name="nsa_attention",
    )(q, kv, kv, kv, kv, kvc, kvc, gates, ovt, qf, fs, fw, fc)


def _head_pair_perm():
    perm = np.zeros(N_HEADS * HEAD_DIM, np.int32)
    d = np.arange(HEAD_DIM)
    for pair in range(N_KV // 2):
        for r in range(GROUP):
            for gi in range(2):
                h = (2 * pair + gi) * GROUP + r
                n0 = pair * GROUP * LANES + r * LANES + gi * HEAD_DIM
                perm[n0 + d] = h * HEAD_DIM + d
    return perm


def _nsa_layer(h, g, w_in, cmp_pos, cmp_w1, cmp_b1, cmp_w2, w_out):
    B, S, D = h.shape
    qcols = N_HEADS * HEAD_DIM
    kvcols = 3 * 2 * N_KV * HEAD_DIM
    ngate = 3 * N_HEADS
    perm = _head_pair_perm()
    ccols = 2 * N_KV * HEAD_DIM
    wq = w_in[:, :qcols][:, perm].astype(BF16)
    wkc = w_in[:, qcols:qcols + ccols].astype(BF16)
    wkv = w_in[:, qcols + ccols:qcols + kvcols].astype(BF16)
    wg_src = w_in[:, qcols + kvcols:]
    wg = jnp.zeros((D, (N_KV // 2) * LANES), F32)
    for pair in range(N_KV // 2):
        for br in range(3):
            src = br * N_HEADS + pair * NSLOT
            dst = pair * LANES + br * NSLOT
            wg = wg.at[:, dst:dst + NSLOT].set(wg_src[:, src:src + NSLOT])
    wg = wg.astype(BF16)

    q, kc, kv, gates = _nsa_proj(h, g, wq, wkc, wkv, wg)

    nrow = S // CMP_STRIDE
    t2 = kc.reshape(B, kc.shape[1], nrow, CMP_STRIDE * LANES)
    hid = cmp_w1.shape[2]
    eye = jnp.eye(2, dtype=F32)
    w1r = cmp_w1.reshape(2, 2, CMP_STRIDE, HEAD_DIM, hid)
    w1p = jnp.einsum('stidh,gk->stigdkh', w1r, eye).reshape(2, 2, CMP_STRIDE * LANES, 2 * hid)
    posr = cmp_pos.reshape(2, 2, CMP_STRIDE, 1, HEAD_DIM)
    posp = jnp.broadcast_to(posr, (2, 2, CMP_STRIDE, 2, HEAD_DIM)).reshape(2, 1, -1)
    posp = jnp.broadcast_to(posp, (2, SUBLANES, posp.shape[2]))
    b1p = jnp.tile(cmp_b1, (1, 2)).reshape(2, 1, 2 * hid)
    w2p = jnp.einsum('shd,gk->sghkd', cmp_w2, eye).reshape(2, 2 * hid, LANES)
    kvc = _nsa_compress(t2, posp, w1p[:, 0].astype(BF16), w1p[:, 1].astype(BF16), b1p,
                        w2p.astype(BF16))

    return _nsa_attention(q, kv, kvc, gates), w_out[perm, :].astype(BF16)


def kernel(x, lru_norm_g, lru_w_in, lru_conv_w, lru_conv_b, lru_gate_w, lru_gate_b, lru_a_param,
           lru_w_out, nsa_norm_g, nsa_w_in, nsa_cmp_pos, nsa_cmp_w1, nsa_cmp_b1, nsa_cmp_w2,
           nsa_w_out, ffn_norm_g, ffn_w_in, ffn_conv_w, ffn_conv_b, ffn_w_out, final_norm_g):
    h = _lru_layer(x, lru_norm_g[0], lru_w_in[0], lru_conv_w[0], lru_conv_b[0], lru_gate_w[0],
                   lru_gate_b[0], lru_a_param[0], lru_w_out[0])
    h = _ffn_layer(h, ffn_norm_g[0], ffn_w_in[0], ffn_conv_w[0], ffn_conv_b[0], ffn_w_out[0],
                   final_norm_g, final_norm=False)
    o, wo = _nsa_layer(h, nsa_norm_g[0], nsa_w_in[0], nsa_cmp_pos[0], nsa_cmp_w1[0],
                       nsa_cmp_b1[0], nsa_cmp_w2[0], nsa_w_out[0])
    return _ffn_layer(h, ffn_norm_g[1], ffn_w_in[1], ffn_conv_w[1], ffn_conv_b[1], ffn_w_out[1],
                      final_norm_g, final_norm=True, mix=o, wmix=wo)
```

```python
import functools
import math

import ml_dtypes
import numpy as np
import jax
import jax.numpy as jnp
from jax import lax
from jax.experimental import pallas as pl
from jax.experimental.pallas import tpu as pltpu

BF16 = jnp.bfloat16
F32 = jnp.float32

EPS = 1e-6
LRU_BLOCK_W = 128
LRU_CONV = 4
LRU_C = 8.0
N_HEADS = 16
HEAD_DIM = 64
N_KV = 4
GROUP = N_HEADS // N_KV
CMP_BLOCK = 32
CMP_STRIDE = 16
SEL_BLOCK = 64
N_SELECT = 8
WINDOW = 512
FORCED_BONUS = 1e4
NEG = -1e30
FFN_CONV = 3

LANES = 128
SUBLANES = 8
VMEM_LIMIT = 56 * 1024 * 1024


def _gelu(x):
    c = math.sqrt(2.0 / math.pi)
    inner = x * (c + (c * 0.044715) * (x * x))
    return (0.5 * x) * (1.0 + jnp.tanh(inner))


def _sigmoid(x):
    return 0.5 * jnp.tanh(0.5 * x) + 0.5


def _rmsnorm(x, g):
    return x * lax.rsqrt(jnp.mean(x * x, axis=-1, keepdims=True) + EPS) * g


def _dot(a, b):
    return jnp.dot(a, b, preferred_element_type=F32)


def _params(sem):
    return pltpu.CompilerParams(dimension_semantics=sem, vmem_limit_bytes=VMEM_LIMIT)


SCAN_CHUNK = SUBLANES * SUBLANES
PROJ_W = 2 * LANES


def _lru_kernel(x_ref, g_ref, wy_ref, wx_ref, cw_ref, cb_ref, gw_ref, gb_ref, ap_ref, wo_ref,
                o_ref, tail_scr, a_scr, b_scr, hc_scr, u_scr, yx_scr, *, tm, nblk):
    t = pl.program_id(1)

    @pl.when(t == 0)
    def _():
        tail_scr[...] = jnp.zeros(tail_scr.shape, F32)
        hc_scr[...] = jnp.zeros(hc_scr.shape, F32)

    x = x_ref[...]
    xn = _rmsnorm(x, g_ref[...]).astype(BF16)
    cw = cw_ref[...]
    cb = cb_ref[...]

    z = -ap_ref[...]
    c8 = -LRU_C * (jnp.maximum(z, 0.0) + jnp.log1p(jnp.exp(-jnp.abs(z))))
    gb = gb_ref[...]

    blk_per_chunk = PROJ_W // LANES
    nchunk = nblk // blk_per_chunk

    def project(i):
        cols = slice(i * PROJ_W, (i + 1) * PROJ_W)
        yx_scr[i % 2, 0] = _dot(xn, wy_ref[:, cols])
        yx_scr[i % 2, 1] = _dot(xn, wx_ref[:, cols])

    sub_w = lax.broadcasted_iota(jnp.int32, (SUBLANES, PROJ_W), 0)

    def conv(i):
        cols = slice(i * PROJ_W, (i + 1) * PROJ_W)
        xb = yx_scr[i % 2, 1]
        tail = tail_scr[:, cols]
        xc = cw[LRU_CONV - 1:LRU_CONV, cols] * xb + cb[:, cols]
        for d in range(1, LRU_CONV):
            rolled = pltpu.roll(xb, d, 0)
            top = jnp.where(sub_w < d, pltpu.roll(tail, d, 0), rolled[0:SUBLANES, :])
            shifted = jnp.concatenate([top, rolled[SUBLANES:, :]], axis=0)
            xc = xc + cw[LRU_CONV - 1 - d:LRU_CONV - d, cols] * shifted
        tail_scr[:, cols] = xb[tm - SUBLANES:tm, :]
        return xc

    sub = lax.broadcasted_iota(jnp.int32, (SUBLANES, LANES), 0)
    out_split = (nchunk // 2 + 1) * PROJ_W
    project(0)
    for n in range(nblk):
        i, k = divmod(n, blk_per_chunk)
        if k == 0:
            if i + 1 < nchunk:
                project(i + 1)
            if i * PROJ_W == out_split:
                o_ref[...] = x + _dot(u_scr[:, 0:out_split], wo_ref[0:out_split, :])
            xc = conv(i)
        lo, hi = n * LANES, (n + 1) * LANES
        xcn = xc[:, k * LANES:(k + 1) * LANES]
        xcb = xcn.astype(BF16)
        r = _sigmoid(_dot(xcb, gw_ref[0, n]) + gb[0:1, lo:hi])
        ig = _sigmoid(_dot(xcb, gw_ref[1, n]) + gb[1:2, lo:hi])
        log_a = c8[:, lo:hi] * r
        a = jnp.exp(log_a)
        w = -jnp.tanh(log_a) * (a * a + 1.0)
        mult = jnp.where(w > 0.0, w * lax.rsqrt(w), 0.0)
        a_scr[n] = a
        b_scr[n] = mult * ig * xcn

        carry = hc_scr[n]
        for c in range(tm // SCAN_CHUNK):
            base = c * SCAN_CHUNK
            acum, hloc = [], []
            for j in range(SUBLANES):
                aj = a_scr.at[n][pl.ds(base + j, SUBLANES, stride=SUBLANES), :]
                bj = b_scr.at[n][pl.ds(base + j, SUBLANES, stride=SUBLANES), :]
                if j == 0:
                    acum.append(aj)
                    hloc.append(bj)
                else:
                    hloc.append(aj * hloc[-1] + bj)
                    acum.append(aj * acum[-1])
            p, e = acum[-1], hloc[-1]
            for d in (1, 2, 4):
                keep = sub >= d
                psh = pltpu.roll(p, d, 0)
                esh = pltpu.roll(e, d, 0)
                e = jnp.where(keep, p * esh + e, e)
                p = jnp.where(keep, p * psh, p)
            hend = e + p * carry
            cin = jnp.where(sub == 0, carry, pltpu.roll(hend, 1, 0))
            for j in range(SUBLANES):
                b_scr.at[n][pl.ds(base + j, SUBLANES, stride=SUBLANES), :] = (
                    hloc[j] + acum[j] * cin)
            carry = jnp.broadcast_to(hend[SUBLANES - 1:SUBLANES, :], (SUBLANES, LANES))
        hc_scr[n] = carry
        y = _gelu(yx_scr[i % 2, 0, :, k * LANES:(k + 1) * LANES])
        u_scr[:, lo:hi] = (b_scr[n] * y).astype(BF16)

    o_ref[...] += _dot(u_scr[:, out_split:], wo_ref[out_split:, :])


def _lru_layer(h, g, w_in, conv_w, conv_b, gate_w, gate_b, a_param, w_out, *, tm=512):
    B, S, D = h.shape
    W = w_out.shape[0]
    nblk = W // LANES
    assert gate_w.shape == (2, nblk, LANES, LANES) and S % tm == 0 and tm % SCAN_CHUNK == 0
    wy = w_in[:, :W].astype(BF16)
    wx = w_in[:, W:].astype(BF16)
    const = lambda shape: pl.BlockSpec(shape, lambda b, t: (0,) * len(shape))
    return pl.pallas_call(
        functools.partial(_lru_kernel, tm=tm, nblk=nblk),
        grid=(B, S // tm),
        in_specs=[
            pl.BlockSpec((None, tm, D), lambda b, t: (b, t, 0)),
            const((1, D)), const((D, W)), const((D, W)), const((LRU_CONV, W)), const((1, W)),
            const((2, nblk, LANES, LANES)), const((2, W)), const((1, W)), const((W, D)),
        ],
        out_specs=pl.BlockSpec((None, tm, D), lambda b, t: (b, t, 0)),
        out_shape=jax.ShapeDtypeStruct((B, S, D), F32),
        scratch_shapes=[
            pltpu.VMEM((SUBLANES, W), F32),
            pltpu.VMEM((nblk, tm, LANES), F32),
            pltpu.VMEM((nblk, tm, LANES), F32),
            pltpu.VMEM((nblk, SUBLANES, LANES), F32),
            pltpu.VMEM((tm, W), BF16),
            pltpu.VMEM((2, 2, tm, PROJ_W), F32),
        ],
        compiler_params=_params(("arbitrary", "arbitrary")),
        name="lru_layer",
    )(h, g.reshape(1, D), wy, wx, conv_w, conv_b.reshape(1, W), gate_w.astype(BF16), gate_b,
      a_param.reshape(1, W), w_out.astype(BF16))


def _ffn_kernel(*refs, tm, final_norm, has_mix):
    if has_mix:
        mix_ref, wmix_ref, *refs = refs
    (x_ref, g_ref, wa_ref, wb_ref, cw_ref, cb_ref, wo_ref, fg_ref, o_ref,
     xn_scr, acc_scr, a_scr, halo_scr) = refs
    t = pl.program_id(1)
    f = pl.program_id(2)
    nf = pl.num_programs(2)

    @pl.when(f == 0)
    def _():
        x = x_ref[...]
        if has_mix:
            x = x + _dot(mix_ref[...], wmix_ref[...])
        xn_scr[...] = _rmsnorm(x, g_ref[...]).astype(BF16)
        acc_scr[...] = x

    @pl.when(t == 0)
    def _():
        halo_scr[f] = jnp.zeros(halo_scr.shape[1:], F32)

    xn = xn_scr[...]
    a = _dot(xn, wa_ref[...])
    b = _dot(xn, wb_ref[...])
    a_scr[0:SUBLANES, :] = halo_scr[f]
    a_scr[SUBLANES:SUBLANES + tm, :] = a
    cw = cw_ref[...]
    ac = (cw[2:3, :] * a + cw[1:2, :] * a_scr[7:7 + tm, :] + cw[0:1, :] * a_scr[6:6 + tm, :]
          + cb_ref[...])
    halo_scr[f] = a_scr[tm:tm + SUBLANES, :]
    hmid = (_gelu(ac) * b).astype(BF16)
    acc_scr[...] += _dot(hmid, wo_ref[...])

    @pl.when(f == nf - 1)
    def _():
        out = acc_scr[...]
        if final_norm:
            out = _rmsnorm(out, fg_ref[...])
        o_ref[...] = out


def _ffn_layer(h, g, w_in, conv_w, conv_b, w_out, final_g, *, final_norm, mix=None, wmix=None,
               tm=1024, tf=512):
    B, S, D = h.shape
    F = w_out.shape[0]
    assert S % tm == 0 and F % tf == 0
    nf = F // tf
    wa = w_in[:, :F].astype(BF16)
    wb = w_in[:, F:].astype(BF16)
    has_mix = mix is not None
    mix_specs, mix_args = [], []
    if has_mix:
        K = mix.shape[2]
        mix_specs = [pl.BlockSpec((None, tm, K), lambda b, t, f: (b, t, 0)),
                     pl.BlockSpec((K, D), lambda b, t, f: (0, 0))]
        mix_args = [mix, wmix]
    return pl.pallas_call(
        functools.partial(_ffn_kernel, tm=tm, final_norm=final_norm, has_mix=has_mix),
        grid=(B, S // tm, nf),
        in_specs=mix_specs + [
            pl.BlockSpec((None, tm, D), lambda b, t, f: (b, t, 0)),
            pl.BlockSpec((1, D), lambda b, t, f: (0, 0)),
            pl.BlockSpec((D, tf), lambda b, t, f: (0, f)),
            pl.BlockSpec((D, tf), lambda b, t, f: (0, f)),
            pl.BlockSpec((FFN_CONV, tf), lambda b, t, f: (0, f)),
            pl.BlockSpec((1, tf), lambda b, t, f: (0, f)),
            pl.BlockSpec((tf, D), lambda b, t, f: (f, 0)),
            pl.BlockSpec((1, D), lambda b, t, f: (0, 0)),
        ],
        out_specs=pl.BlockSpec((None, tm, D), lambda b, t, f: (b, t, 0)),
        out_shape=jax.ShapeDtypeStruct((B, S, D), F32),
        scratch_shapes=[
            pltpu.VMEM((tm, D), BF16),
            pltpu.VMEM((tm, D), F32),
            pltpu.VMEM((tm + SUBLANES, tf), F32),
            pltpu.VMEM((nf, SUBLANES, tf), F32),
        ],
        compiler_params=_params(("arbitrary", "arbitrary", "arbitrary")),
        name="ffn_layer",
    )(*mix_args, h, g.reshape(1, D), wa, wb, conv_w, conv_b.reshape(1, F), w_out.astype(BF16),
      final_g.reshape(1, D))


def _nsa_proj_kernel(x_ref, g_ref, wq_ref, wkc_ref, wkv_ref, wg_ref, q_ref, t2_ref, kv_ref, gt_ref,
                     kc_scr):
    xn = _rmsnorm(x_ref[...], g_ref[...]).astype(BF16)
    q_ref[...] = (_dot(xn, wq_ref[...]) * (HEAD_DIM ** -0.5 * LOG2E)).astype(BF16)
    kc = _dot(xn, wkc_ref[...])
    nrow = t2_ref.shape[1]
    for j in range(kc_scr.shape[0]):
        kc_scr[j] = kc[:, j * LANES:(j + 1) * LANES]
        for i in range(CMP_STRIDE):
            t2_ref[j, :, i * LANES:(i + 1) * LANES] = (
                kc_scr.at[j][pl.ds(i, nrow, stride=CMP_STRIDE), :].astype(BF16))
    kv_ref[...] = _dot(xn, wkv_ref[...]).astype(BF16)
    gt_ref[...] = _sigmoid(_dot(xn, wg_ref[...]))


def _nsa_proj(h, g, wq, wkc, wkv, wg, *, tm=512):
    B, S, D = h.shape
    nq, nkc, nkv, ng = wq.shape[1], wkc.shape[1], wkv.shape[1], wg.shape[1]
    nslab = nkc // LANES
    const = lambda shape: pl.BlockSpec(shape, lambda b, t: (0,) * len(shape))
    row = lambda n: pl.BlockSpec((None, tm, n), lambda b, t: (b, t, 0))
    return pl.pallas_call(
        _nsa_proj_kernel,
        grid=(B, S // tm),
        in_specs=[row(D), const((1, D)), const((D, nq)), const((D, nkc)), const((D, nkv)),
                  const((D, ng))],
        out_specs=[row(nq),
                   pl.BlockSpec((None, nslab, tm // CMP_STRIDE, CMP_STRIDE * LANES),
                                lambda b, t: (b, 0, t, 0)),
                   row(nkv), row(ng)],
        out_shape=[jax.ShapeDtypeStruct((B, S, nq), BF16),
                   jax.ShapeDtypeStruct((B, nslab, S // CMP_STRIDE, CMP_STRIDE * LANES), BF16),
                   jax.ShapeDtypeStruct((B, S, nkv), BF16),
                   jax.ShapeDtypeStruct((B, S, ng), F32)],
        scratch_shapes=[pltpu.VMEM((nslab, tm, LANES), F32)],
        compiler_params=_params(("arbitrary", "arbitrary")),
        name="nsa_proj",
    )(h, g.reshape(1, D), wq, wkc, wkv, wg)


def _cmp_kernel(t2_ref, pos_ref, wt_ref, wb_ref, b1_ref, w2_ref, o_ref):
    nb, nrow, K = t2_ref.shape
    t2 = t2_ref[...].reshape(nb * nrow, K)
    w_top = wt_ref[...]
    w_bot = wb_ref[...]
    pos = pos_ref[...].astype(BF16)
    bias = _dot(pos[:, :K], w_top) + _dot(pos[:, K:], w_bot) + b1_ref[...]
    hid = _dot(t2, w_top) + pltpu.roll(_dot(t2, w_bot), nb * nrow - 1, 0) + bias[0:1, :]
    out = _dot(_gelu(hid).astype(BF16), w2_ref[...])
    o_ref[...] = out.reshape(nb, nrow, out.shape[1]).astype(o_ref.dtype)


def _nsa_compress(t2, pos, w_top, w_bot, b1, w2, *, nb=4):
    B, nslab, nrow, K = t2.shape
    npair = nslab // 2
    H2 = w_top.shape[2]
    nb = math.gcd(nb, B)
    per_kv = lambda shape: pl.BlockSpec((None,) + shape, lambda s, p, b: (s,) + (0,) * len(shape))
    return pl.pallas_call(
        _cmp_kernel,
        grid=(2, npair, B // nb),
        in_specs=[
            pl.BlockSpec((nb, None, nrow, K), lambda s, p, b: (b, s * npair + p, 0, 0)),
            per_kv((SUBLANES, 2 * K)), per_kv((K, H2)), per_kv((K, H2)), per_kv((1, H2)),
            per_kv((H2, LANES)),
        ],
        out_specs=pl.BlockSpec((nb, None, nrow, LANES), lambda s, p, b: (b, s, 0, p)),
        out_shape=jax.ShapeDtypeStruct((B, 2, nrow, npair * LANES), BF16),
        compiler_params=_params(("arbitrary", "arbitrary", "arbitrary")),
        name="nsa_compress",
    )(t2, pos, w_top, w_bot, b1, w2)


NSLOT = 2 * GROUP
LOG2E = 1.4426950408889634
NPIECE = 4
SEL_LANE0 = 0
POS_LANE0 = 32
CMP_LANE0 = 40
QK_AHEAD = 3
SCORE_BUFS = 4
ONES_ROWS = 16


def _bf16_pieces(x, n):
    out, r = [], np.float64(x)
    for _ in range(n):
        p = np.float64(np.asarray(r, np.float32).astype(ml_dtypes.bfloat16).astype(np.float32))
        out.append(p)
        r = r - p
    return out


def _feature_base(gi):
    return HEAD_DIM if gi == 0 else 0


def _attn_tables(S, ncmp, tq):
    npair = N_KV // 2
    qf = np.zeros((npair, NSLOT, HEAD_DIM, tq), np.float32)
    for pair in range(npair):
        for slot in range(NSLOT):
            h = NSLOT * pair + slot
            for i, p in enumerate(_bf16_pieces(2.0 ** (-(h + 1) / 2.0) * LOG2E, NPIECE)):
                qf[pair, slot, POS_LANE0 + 2 * i] = SEL_BLOCK * p
                qf[pair, slot, POS_LANE0 + 2 * i + 1] = p
                qf[pair, slot, CMP_LANE0 + i] = CMP_STRIDE * p
    pos = np.arange(S)
    fs = np.zeros((2, S, LANES), np.float32)
    fw = np.zeros((2, S, LANES), np.float32)
    fc = np.zeros((2, ncmp, LANES), np.float32)
    for gi in range(2):
        f0 = _feature_base(gi)
        fs[gi, pos, f0 + SEL_LANE0 + pos // SEL_BLOCK] = 1.0
        for i in range(NPIECE):
            for f in (fs, fw):
                f[gi, :, f0 + POS_LANE0 + 2 * i] = pos // SEL_BLOCK
                f[gi, :, f0 + POS_LANE0 + 2 * i + 1] = pos % SEL_BLOCK
            fc[gi, :, f0 + CMP_LANE0 + i] = np.arange(ncmp)
    return (jnp.asarray(qf, BF16), jnp.asarray(fs, BF16), jnp.asarray(fw, BF16),
            jnp.asarray(fc, BF16))


def _attn_kernel(q_ref, ks_ref, vs_ref, kw_ref, vw_ref, kc_ref, vc_ref, gt_ref, ovt_ref, qf_ref,
                 fs_ref, fw_ref, fc_ref, o_ref, qaug_scr, kaug_scr, vt_scr, bias_scr, m_scr,
                 acc_scr, oc_scr, psum_scr, s_scr, p_scr, next_scr, *, tq, seq):
    tk = tq
    qt = pl.program_id(2)
    q0 = qt * tq
    ncmp = kc_ref.shape[0]
    nsel = seq // SEL_BLOCK
    ntile = seq // tk

    @pl.when(qt == 0)
    def _():
        own0 = lax.broadcasted_iota(jnp.int32, (tk, LANES), 1) < HEAD_DIM
        for br, k_ref, f_ref, v_ref in ((0, ks_ref, fs_ref, vs_ref), (1, kw_ref, fw_ref, vw_ref)):
            for j in range(ntile):
                rows = slice(j * tk, (j + 1) * tk)
                k2 = k_ref[rows, :]
                kaug_scr[br, 0, j] = jnp.where(own0, k2, f_ref[0, rows, :])
                kaug_scr[br, 1, j] = jnp.where(own0, f_ref[1, rows, :], k2)
                vt = v_ref[rows, :].astype(F32).T.astype(BF16)
                ones = jnp.ones((ONES_ROWS, tk), BF16)
                for gi in range(2):
                    vt_scr[br, j, gi] = jnp.concatenate(
                        [vt[gi * HEAD_DIM:(gi + 1) * HEAD_DIM, :], ones], axis=0)
        d = (lax.broadcasted_iota(jnp.int32, (tk, tq), 1)
             - lax.broadcasted_iota(jnp.int32, (tk, tq), 0))
        bias_scr[0] = jnp.where(d < 0, 0.0, NEG)
        bias_scr[1] = jnp.zeros((tk, tq), F32)
        bias_scr[2] = jnp.where(d >= 0, 0.0, NEG)

    m_scr[...] = jnp.full(m_scr.shape, NEG, F32)
    acc_scr[...] = jnp.zeros(acc_scr.shape, F32)

    for r in range(GROUP):
        qct = q_ref[:, r * LANES:(r + 1) * LANES].astype(F32).T.astype(BF16)
        qaug_scr[r] = jnp.concatenate([qct[0:HEAD_DIM, :], qf_ref[r]], axis=0)
        qaug_scr[GROUP + r] = jnp.concatenate([qf_ref[GROUP + r], qct[HEAD_DIM:, :]], axis=0)

    own0c = lax.broadcasted_iota(jnp.int32, (ncmp, LANES), 1) < HEAD_DIM
    kc2 = kc_ref[...]
    kcaug = [jnp.where(own0c, kc2, fc_ref[0]), jnp.where(own0c, fc_ref[1], kc2)]
    vct = vc_ref[...].astype(F32).T.astype(BF16)
    cend = lax.broadcasted_iota(jnp.int32, (ncmp, tq), 0) * CMP_STRIDE + (CMP_BLOCK - 1)
    cmp_bias = jnp.where(cend <= (q0 + lax.broadcasted_iota(jnp.int32, (ncmp, tq), 1)), 0.0, NEG)
    nbuf = s_scr.shape[0]
    assert NSLOT % nbuf == 0

    def issue_cmp_scores(slot):
        s_scr[slot % nbuf, 0:ncmp, :] = _dot(kcaug[slot // GROUP], qaug_scr[slot])

    def issue_scores(br, j, slot):
        s_scr[slot % nbuf] = _dot(kaug_scr[br, slot // GROUP, j], qaug_scr[slot])

    def cmp_values(slot):
        gi = slot // GROUP
        oc_scr[slot] = _dot(vct[gi * HEAD_DIM:(gi + 1) * HEAD_DIM, :], p_scr[slot % 2, 0:ncmp, :])

    win_lo = jnp.maximum(qt - 2, 0)
    for slot in range(QK_AHEAD):
        issue_cmp_scores(slot)
    psum = [None, None]
    for slot in range(NSLOT):
        gi = slot // GROUP
        sm = s_scr[slot % nbuf, 0:ncmp, :] + cmp_bias
        mx = jnp.max(sm, axis=0, keepdims=True)
        e = jnp.exp2(sm - mx)
        l = jnp.sum(e, axis=0, keepdims=True)
        p = e * jnp.where(mx > 0.5 * NEG, 1.0 / l, 0.0)
        psum[gi] = p if psum[gi] is None else psum[gi] + p
        p_scr[slot % 2, 0:ncmp, :] = p.astype(BF16)
        if slot + QK_AHEAD < NSLOT:
            issue_cmp_scores(slot + QK_AHEAD)
        else:
            issue_scores(1, win_lo, slot + QK_AHEAD - NSLOT)
        if slot >= 1:
            cmp_values(slot - 1)
    cmp_values(NSLOT - 1)
    for gi in range(2):
        psum_scr[gi] = psum[gi]

    def normalized(br, slot):
        return (acc_scr[br, slot, 0:HEAD_DIM, :]
                / acc_scr[br, slot, HEAD_DIM:HEAD_DIM + 1, :])

    def tile(br, j, ahead, bias, side_work=None):
        def weighted_values(slot, alpha):
            acc_scr[br, slot] = alpha * acc_scr[br, slot] + _dot(
                vt_scr[br, j, slot // GROUP], p_scr[slot % 2])

        alphas = []
        for slot in range(NSLOT):
            s = s_scr[slot % nbuf]
            if bias is not None:
                s = s + bias
            m_prev = m_scr[br, slot]
            m_new = jnp.maximum(m_prev, jnp.max(s, axis=0, keepdims=True))
            alphas.append(jnp.exp2(m_prev - m_new))
            p_scr[slot % 2] = jnp.exp2(s - m_new).astype(BF16)
            m_scr[br, slot] = m_new
            if slot + QK_AHEAD < NSLOT:
                issue_scores(br, j, slot + QK_AHEAD)
            elif ahead is not None:
                issue_scores(ahead[0], ahead[1], slot + QK_AHEAD - NSLOT)
            if slot >= 1:
                weighted_values(slot - 1, alphas[slot - 1])
            if side_work is not None:
                side_work(slot)
        weighted_values(NSLOT - 1, alphas[NSLOT - 1])

    def sweep(br, j_lo, bias_fn, last_ahead, last_side_work):
        def body(j, carry):
            tile(br, j, (br, j + 1), bias_fn(j))
            return carry

        lax.fori_loop(j_lo, qt, body, 0)
        tile(br, qt, last_ahead, bias_scr[2], last_side_work)

    jidx = lax.broadcasted_iota(jnp.int32, (nsel, tq), 0)
    ntop = min(N_SELECT, nsel)
    topk = {}

    def topk_start():
        ovt = ovt_ref[...]
        cur = (q0 + lax.broadcasted_iota(jnp.int32, (nsel, tq), 1)) // SEL_BLOCK
        forced = (jidx == 0) | (jidx == cur) | (jidx == cur - 1)
        future = jidx > cur
        for gi in range(2):
            ps = psum_scr[gi]
            p1 = ps.astype(BF16)
            r1 = ps - p1.astype(F32)
            p2 = r1.astype(BF16)
            p3 = (r1 - p2.astype(F32)).astype(BF16)
            imp = _dot(ovt, p1) + _dot(ovt, p2) + _dot(ovt, p3)
            topk[gi] = (jnp.where(forced, FORCED_BONUS, jnp.where(future, -1.0, imp)),
                        jnp.full((nsel, tq), NEG, F32))

    def topk_round():
        for gi in range(2):
            score, selneg = topk[gi]
            mx = jnp.max(score, axis=0, keepdims=True)
            cand = jnp.where(score == mx, jidx, nsel)
            first = jnp.min(cand, axis=0, keepdims=True)
            hit = jidx == first
            topk[gi] = (jnp.where(hit, -3e38, score), jnp.where(hit, 0.0, selneg))

    def topk_finish():
        for gi in range(2):
            f0 = _feature_base(gi) + SEL_LANE0
            selneg = topk[gi][1].astype(BF16)
            for r in range(GROUP):
                qaug_scr[gi * GROUP + r, f0:f0 + nsel, :] = selneg
        chosen = jnp.max(jnp.maximum(topk[0][1], topk[1][1]), axis=1, keepdims=True)
        blk_per_tile = tk // SEL_BLOCK
        assert blk_per_tile < 16 and ntile <= 8
        tile_of_blk = lax.broadcasted_iota(jnp.int32, (nsel, 1), 0) // blk_per_tile
        digits = jnp.sum(jnp.where(chosen == 0.0, jnp.left_shift(1, 4 * tile_of_blk), 0))
        nxt = qt
        count = jnp.int32(0)
        for j in reversed(range(ntile)):
            next_scr[j] = nxt
            hit = (jnp.right_shift(digits, 4 * j) & 15) != 0
            if j == 0:
                hit = True
            take = hit & (j < qt)
            nxt = jnp.where(take, j, nxt)
            count = count + take.astype(jnp.int32)
        topk['count'] = count

    def topk_side_work(slot):
        last = NSLOT - QK_AHEAD - 1
        if slot == 0:
            topk_start()
        for i in range(ntop):
            if i * (last + 1) // ntop == slot:
                topk_round()
        if slot == last:
            topk_finish()

    sweep(1, win_lo, lambda j: bias_scr[j - qt + 2], (0, 0), topk_side_work)

    def chosen_tile(_, j):
        j_next = next_scr[j]
        tile(0, j, (0, j_next), None)
        return j_next

    lax.fori_loop(0, topk['count'], chosen_tile, jnp.int32(0))
    tile(0, qt, None, bias_scr[2])

    gtt = gt_ref[...].T
    for r in range(GROUP):
        comb = []
        for gi in range(2):
            slot = gi * GROUP + r
            comb.append(gtt[slot:slot + 1, :] * oc_scr[slot]
                        + gtt[NSLOT + slot:NSLOT + slot + 1, :] * normalized(0, slot)
                        + gtt[2 * NSLOT + slot:2 * NSLOT + slot + 1, :] * normalized(1, slot))
        col = jnp.concatenate(comb, axis=0).T
        o_ref[:, r * LANES:(r + 1) * LANES] = col.astype(o_ref.dtype)


def _nsa_attention(q, kv, kvc, gates, *, tq=256):
    B, S, _ = q.shape
    npair = N_KV // 2
    ncmp = kvc.shape[2]
    nsel = S // SEL_BLOCK
    assert S % tq == 0 and WINDOW == 2 * tq and nsel <= SEL_LANE0 + POS_LANE0
    cstart = np.arange(ncmp) * CMP_STRIDE
    selj = np.arange(nsel)
    ovt = ((cstart[None, :] < (selj[:, None] + 1) * SEL_BLOCK)
           & (cstart[None, :] + CMP_BLOCK > selj[:, None] * SEL_BLOCK))
    ovt = jnp.asarray(ovt, BF16)
    qf, fs, fw, fc = _attn_tables(S, ncmp, tq)

    def kvspec(cb):
        return pl.BlockSpec((None, S, LANES), lambda b, p, t: (b, 0, cb + p))

    def cspec(s):
        return pl.BlockSpec((None, None, ncmp, LANES), lambda b, p, t: (b, s, 0, p))

    const = lambda shape: pl.BlockSpec(shape, lambda b, p, t: (0,) * len(shape))
    qw = GROUP * LANES
    ntile = S // tq
    return pl.pallas_call(
        functools.partial(_attn_kernel, tq=tq, seq=S),
        grid=(B, npair, S // tq),
        in_specs=[
            pl.BlockSpec((None, tq, qw), lambda b, p, t: (b, t, p)),
            kvspec(0), kvspec(2), kvspec(4), kvspec(6),
            cspec(0), cspec(1),
            pl.BlockSpec((None, tq, LANES), lambda b, p, t: (b, t, p)),
            const((nsel, ncmp)),
            pl.BlockSpec((None, NSLOT, HEAD_DIM, tq), lambda b, p, t: (p, 0, 0, 0)),
            const((2, S, LANES)), const((2, S, LANES)), const((2, ncmp, LANES)),
        ],
        out_specs=pl.BlockSpec((None, tq, qw), lambda b, p, t: (b, t, p)),
        out_shape=jax.ShapeDtypeStruct((B, S, npair * qw), BF16),
        scratch_shapes=[
            pltpu.VMEM((NSLOT, LANES, tq), BF16),
            pltpu.VMEM((2, 2, ntile, tq, LANES), BF16),
            pltpu.VMEM((2, ntile, 2, HEAD_DIM + ONES_ROWS, tq), BF16),
            pltpu.VMEM((3, tq, tq), F32),
            pltpu.VMEM((2, NSLOT, 1, tq), F32),
            pltpu.VMEM((2, NSLOT, HEAD_DIM + ONES_ROWS, tq), F32),
            pltpu.VMEM((NSLOT, HEAD_DIM, tq), F32),
            pltpu.VMEM((2, ncmp, tq), F32),
            pltpu.VMEM((SCORE_BUFS, tq, tq), F32),
            pltpu.VMEM((2, tq, tq), BF16),
            pltpu.SMEM((ntile,), jnp.int32),
        ],
        compiler_params=_params(("arbitrary", "arbitrary", "arbitrary")),
        name="nsa_attention",
    )(q, kv, kv, kv, kv, kvc, kvc, gates, ovt, qf, fs, fw, fc)


def _head_pair_perm():
    perm = np.zeros(N_HEADS * HEAD_DIM, np.int32)
    d = np.arange(HEAD_DIM)
    for pair in range(N_KV // 2):
        for r in range(GROUP):
            for gi in range(2):
                h = (2 * pair + gi) * GROUP + r
                n0 = pair * GROUP * LANES + r * LANES + gi * HEAD_DIM
                perm[n0 + d] = h * HEAD_DIM + d
    return perm


def _nsa_layer(h, g, w_in, cmp_pos, cmp_w1, cmp_b1, cmp_w2, w_out):
    B, S, D = h.shape
    qcols = N_HEADS * HEAD_DIM
    kvcols = 3 * 2 * N_KV * HEAD_DIM
    ngate = 3 * N_HEADS
    perm = _head_pair_perm()
    ccols = 2 * N_KV * HEAD_DIM
    wq = w_in[:, :qcols][:, perm].astype(BF16)
    wkc = w_in[:, qcols:qcols + ccols].astype(BF16)
    wkv = w_in[:, qcols + ccols:qcols + kvcols].astype(BF16)
    wg_src = w_in[:, qcols + kvcols:]
    wg = jnp.zeros((D, (N_KV // 2) * LANES), F32)
    for pair in range(N_KV // 2):
        for br in range(3):
            src = br * N_HEADS + pair * NSLOT
            dst = pair * LANES + br * NSLOT
            wg = wg.at[:, dst:dst + NSLOT].set(wg_src[:, src:src + NSLOT])
    wg = wg.astype(BF16)

    q, t2, kv, gates = _nsa_proj(h, g, wq, wkc, wkv, wg)

    hid = cmp_w1.shape[2]
    eye = jnp.eye(2, dtype=BF16)
    w1r = cmp_w1.astype(BF16).reshape(2, 2, CMP_STRIDE, HEAD_DIM, hid)
    w1p = jnp.einsum('stidh,gk->stigdkh', w1r, eye).reshape(2, 2, CMP_STRIDE * LANES, 2 * hid)
    posr = cmp_pos.reshape(2, 2, CMP_STRIDE, 1, HEAD_DIM)
    posp = jnp.broadcast_to(posr, (2, 2, CMP_STRIDE, 2, HEAD_DIM)).reshape(2, 1, -1)
    posp = jnp.broadcast_to(posp, (2, SUBLANES, posp.shape[2]))
    b1p = jnp.tile(cmp_b1, (1, 2)).reshape(2, 1, 2 * hid)
    w2p = jnp.einsum('shd,gk->sghkd', cmp_w2.astype(BF16), eye).reshape(2, 2 * hid, LANES)
    kvc = _nsa_compress(t2, posp, w1p[:, 0], w1p[:, 1], b1p, w2p)

    return _nsa_attention(q, kv, kvc, gates), w_out[perm, :].astype(BF16)


def kernel(x, lru_norm_g, lru_w_in, lru_conv_w, lru_conv_b, lru_gate_w, lru_gate_b, lru_a_param,
           lru_w_out, nsa_norm_g, nsa_w_in, nsa_cmp_pos, nsa_cmp_w1, nsa_cmp_b1, nsa_cmp_w2,
           nsa_w_out, ffn_norm_g, ffn_w_in, ffn_conv_w, ffn_conv_b, ffn_w_out, final_norm_g):
    h = _lru_layer(x, lru_norm_g[0], lru_w_in[0], lru_conv_w[0], lru_conv_b[0], lru_gate_w[0],
                   lru_gate_b[0], lru_a_param[0], lru_w_out[0])
    h = _ffn_layer(h, ffn_norm_g[0], ffn_w_in[0], ffn_conv_w[0], ffn_conv_b[0], ffn_w_out[0],
                   final_norm_g, final_norm=False)
    o, wo = _nsa_layer(h, nsa_norm_g[0], nsa_w_in[0], nsa_cmp_pos[0], nsa_cmp_w1[0],
                       nsa_cmp_b1[0], nsa_cmp_w2[0], nsa_w_out[0])
    return _ffn_layer(h, ffn_norm_g[1], ffn_w_in[1], ffn_conv_w[1], ffn_conv_b[1], ffn_w_out[1],
                      final_norm_g, final_norm=True, mix=o, wmix=wo)
```

```python
import functools
import math

import ml_dtypes
import numpy as np
import jax
import jax.numpy as jnp
from jax import lax
from jax.experimental import pallas as pl
from jax.experimental.pallas import tpu as pltpu

BF16 = jnp.bfloat16
F32 = jnp.float32

EPS = 1e-6
LRU_BLOCK_W = 128
LRU_CONV = 4
LRU_C = 8.0
N_HEADS = 16
HEAD_DIM = 64
N_KV = 4
GROUP = N_HEADS // N_KV
CMP_BLOCK = 32
CMP_STRIDE = 16
SEL_BLOCK = 64
N_SELECT = 8
WINDOW = 512
FORCED_BONUS = 1e4
NEG = -1e30
FFN_CONV = 3

LANES = 128
SUBLANES = 8
VMEM_LIMIT = 56 * 1024 * 1024


def _gelu(x):
    c = math.sqrt(2.0 / math.pi)
    inner = x * (c + (c * 0.044715) * (x * x))
    return (0.5 * x) * (1.0 + jnp.tanh(inner))


def _sigmoid(x):
    return 0.5 * jnp.tanh(0.5 * x) + 0.5


def _rmsnorm(x, g):
    return x * lax.rsqrt(jnp.mean(x * x, axis=-1, keepdims=True) + EPS) * g


def _dot(a, b):
    return jnp.dot(a, b, preferred_element_type=F32)


def _params(sem):
    return pltpu.CompilerParams(dimension_semantics=sem, vmem_limit_bytes=VMEM_LIMIT)


SCAN_CHUNK = SUBLANES * SUBLANES
PROJ_W = 2 * LANES


def _lru_kernel(x_ref, g_ref, wy_ref, wx_ref, cw_ref, cb_ref, gw_ref, gb_ref, ap_ref, wo_ref,
                o_ref, tail_scr, a_scr, b_scr, hc_scr, u_scr, yx_scr, *, tm, nblk):
    t = pl.program_id(1)

    @pl.when(t == 0)
    def _():
        tail_scr[...] = jnp.zeros(tail_scr.shape, F32)
        hc_scr[...] = jnp.zeros(hc_scr.shape, F32)

    x = x_ref[...]
    xn = _rmsnorm(x, g_ref[...]).astype(BF16)
    cw = cw_ref[...]
    cb = cb_ref[...]

    z = -ap_ref[...]
    c8 = -LRU_C * (jnp.maximum(z, 0.0) + jnp.log1p(jnp.exp(-jnp.abs(z))))
    gb = gb_ref[...]

    blk_per_chunk = PROJ_W // LANES
    nchunk = nblk // blk_per_chunk

    def project(i, which):
        cols = slice(i * PROJ_W, (i + 1) * PROJ_W)
        w_ref = wy_ref if which == 0 else wx_ref
        yx_scr[i % 2, which] = _dot(xn, w_ref[:, cols])

    sub_w = lax.broadcasted_iota(jnp.int32, (SUBLANES, PROJ_W), 0)

    def conv(i):
        cols = slice(i * PROJ_W, (i + 1) * PROJ_W)
        xb = yx_scr[i % 2, 1]
        tail = tail_scr[:, cols]
        xc = cw[LRU_CONV - 1:LRU_CONV, cols] * xb + cb[:, cols]
        for d in range(1, LRU_CONV):
            rolled = pltpu.roll(xb, d, 0)
            top = jnp.where(sub_w < d, pltpu.roll(tail, d, 0), rolled[0:SUBLANES, :])
            shifted = jnp.concatenate([top, rolled[SUBLANES:, :]], axis=0)
            xc = xc + cw[LRU_CONV - 1 - d:LRU_CONV - d, cols] * shifted
        tail_scr[:, cols] = xb[tm - SUBLANES:tm, :]
        return xc

    sub = lax.broadcasted_iota(jnp.int32, (SUBLANES, LANES), 0)
    out_split = (nchunk - 1) * PROJ_W
    out_half = out_split // 2

    def out_rows_a():
        o_ref[...] = x + _dot(u_scr[:, 0:out_half], wo_ref[0:out_half, :])

    def out_rows_b():
        o_ref[...] += _dot(u_scr[:, out_half:out_split], wo_ref[out_half:out_split, :])

    queued = [functools.partial(project, i, which)
              for i in range(1, nchunk) for which in (0, 1)] + [out_rows_a, out_rows_b]
    assert len(queued) == nblk
    project(0, 0)
    project(0, 1)
    for n in range(nblk):
        i, k = divmod(n, blk_per_chunk)
        if k == 0:
            xc = conv(i)
        lo, hi = n * LANES, (n + 1) * LANES
        xcn = xc[:, k * LANES:(k + 1) * LANES]
        xcb = xcn.astype(BF16)
        zr = _dot(xcb, gw_ref[0, n])
        zi = _dot(xcb, gw_ref[1, n])
        if n < len(queued):
            queued[n]()
        r = _sigmoid(zr + gb[0:1, lo:hi])
        ig = _sigmoid(zi + gb[1:2, lo:hi])
        log_a = c8[:, lo:hi] * r
        a = jnp.exp(log_a)
        w = -jnp.tanh(log_a) * (a * a + 1.0)
        mult = jnp.where(w > 0.0, w * lax.rsqrt(w), 0.0)
        a_scr[n] = a
        b_scr[n] = mult * ig * xcn

        carry = hc_scr[n]
        for c in range(tm // SCAN_CHUNK):
            base = c * SCAN_CHUNK
            acum, hloc = [], []
            for j in range(SUBLANES):
                aj = a_scr.at[n][pl.ds(base + j, SUBLANES, stride=SUBLANES), :]
                bj = b_scr.at[n][pl.ds(base + j, SUBLANES, stride=SUBLANES), :]
                if j == 0:
                    acum.append(aj)
                    hloc.append(bj)
                else:
                    hloc.append(aj * hloc[-1] + bj)
                    acum.append(aj * acum[-1])
            p, e = acum[-1], hloc[-1]
            for d in (1, 2, 4):
                keep = sub >= d
                psh = pltpu.roll(p, d, 0)
                esh = pltpu.roll(e, d, 0)
                e = jnp.where(keep, p * esh + e, e)
                p = jnp.where(keep, p * psh, p)
            hend = e + p * carry
            cin = jnp.where(sub == 0, carry, pltpu.roll(hend, 1, 0))
            for j in range(SUBLANES):
                b_scr.at[n][pl.ds(base + j, SUBLANES, stride=SUBLANES), :] = (
                    hloc[j] + acum[j] * cin)
            carry = jnp.broadcast_to(hend[SUBLANES - 1:SUBLANES, :], (SUBLANES, LANES))
        hc_scr[n] = carry
        y = _gelu(yx_scr[i % 2, 0, :, k * LANES:(k + 1) * LANES])
        u_scr[:, lo:hi] = (b_scr[n] * y).astype(BF16)

    o_ref[...] += _dot(u_scr[:, out_split:], wo_ref[out_split:, :])


def _lru_layer(h, g, w_in, conv_w, conv_b, gate_w, gate_b, a_param, w_out, *, tm=512):
    B, S, D = h.shape
    W = w_out.shape[0]
    nblk = W // LANES
    assert gate_w.shape == (2, nblk, LANES, LANES) and S % tm == 0 and tm % SCAN_CHUNK == 0
    wy = w_in[:, :W].astype(BF16)
    wx = w_in[:, W:].astype(BF16)
    const = lambda shape: pl.BlockSpec(shape, lambda b, t: (0,) * len(shape))
    return pl.pallas_call(
        functools.partial(_lru_kernel, tm=tm, nblk=nblk),
        grid=(B, S // tm),
        in_specs=[
            pl.BlockSpec((None, tm, D), lambda b, t: (b, t, 0)),
            const((1, D)), const((D, W)), const((D, W)), const((LRU_CONV, W)), const((1, W)),
            const((2, nblk, LANES, LANES)), const((2, W)), const((1, W)), const((W, D)),
        ],
        out_specs=pl.BlockSpec((None, tm, D), lambda b, t: (b, t, 0)),
        out_shape=jax.ShapeDtypeStruct((B, S, D), F32),
        scratch_shapes=[
            pltpu.VMEM((SUBLANES, W), F32),
            pltpu.VMEM((nblk, tm, LANES), F32),
            pltpu.VMEM((nblk, tm, LANES), F32),
            pltpu.VMEM((nblk, SUBLANES, LANES), F32),
            pltpu.VMEM((tm, W), BF16),
            pltpu.VMEM((2, 2, tm, PROJ_W), F32),
        ],
        compiler_params=_params(("arbitrary", "arbitrary")),
        name="lru_layer",
    )(h, g.reshape(1, D), wy, wx, conv_w, conv_b.reshape(1, W), gate_w.astype(BF16), gate_b,
      a_param.reshape(1, W), w_out.astype(BF16))


def _ffn_kernel(*refs, tm, final_norm, has_mix):
    if has_mix:
        mix_ref, wmix_ref, *refs = refs
    (x_ref, g_ref, wa_ref, wb_ref, cw_ref, cb_ref, wo_ref, fg_ref, o_ref,
     xn_scr, acc_scr, a_scr, halo_scr) = refs
    t = pl.program_id(1)
    f = pl.program_id(2)
    nf = pl.num_programs(2)

    @pl.when(f == 0)
    def _():
        x = x_ref[...]
        if has_mix:
            x = x + _dot(mix_ref[...], wmix_ref[...])
        xn_scr[...] = _rmsnorm(x, g_ref[...]).astype(BF16)
        acc_scr[...] = x

    @pl.when(t == 0)
    def _():
        halo_scr[f] = jnp.zeros(halo_scr.shape[1:], F32)

    xn = xn_scr[...]
    a = _dot(xn, wa_ref[...])
    b = _dot(xn, wb_ref[...])
    a_scr[0:SUBLANES, :] = halo_scr[f]
    a_scr[SUBLANES:SUBLANES + tm, :] = a
    cw = cw_ref[...]
    ac = (cw[2:3, :] * a + cw[1:2, :] * a_scr[7:7 + tm, :] + cw[0:1, :] * a_scr[6:6 + tm, :]
          + cb_ref[...])
    halo_scr[f] = a_scr[tm:tm + SUBLANES, :]
    hmid = (_gelu(ac) * b).astype(BF16)
    acc_scr[...] += _dot(hmid, wo_ref[...])

    @pl.when(f == nf - 1)
    def _():
        out = acc_scr[...]
        if final_norm:
            out = _rmsnorm(out, fg_ref[...])
        o_ref[...] = out


def _ffn_layer(h, g, w_in, conv_w, conv_b, w_out, final_g, *, final_norm, mix=None, wmix=None,
               tm=1024, tf=512):
    B, S, D = h.shape
    F = w_out.shape[0]
    assert S % tm == 0 and F % tf == 0
    nf = F // tf
    wa = w_in[:, :F].astype(BF16)
    wb = w_in[:, F:].astype(BF16)
    has_mix = mix is not None
    mix_specs, mix_args = [], []
    if has_mix:
        K = mix.shape[2]
        mix_specs = [pl.BlockSpec((None, tm, K), lambda b, t, f: (b, t, 0)),
                     pl.BlockSpec((K, D), lambda b, t, f: (0, 0))]
        mix_args = [mix, wmix]
    return pl.pallas_call(
        functools.partial(_ffn_kernel, tm=tm, final_norm=final_norm, has_mix=has_mix),
        grid=(B, S // tm, nf),
        in_specs=mix_specs + [
            pl.BlockSpec((None, tm, D), lambda b, t, f: (b, t, 0)),
            pl.BlockSpec((1, D), lambda b, t, f: (0, 0)),
            pl.BlockSpec((D, tf), lambda b, t, f: (0, f)),
            pl.BlockSpec((D, tf), lambda b, t, f: (0, f)),
            pl.BlockSpec((FFN_CONV, tf), lambda b, t, f: (0, f)),
            pl.BlockSpec((1, tf), lambda b, t, f: (0, f)),
            pl.BlockSpec((tf, D), lambda b, t, f: (f, 0)),
            pl.BlockSpec((1, D), lambda b, t, f: (0, 0)),
        ],
        out_specs=pl.BlockSpec((None, tm, D), lambda b, t, f: (b, t, 0)),
        out_shape=jax.ShapeDtypeStruct((B, S, D), F32),
        scratch_shapes=[
            pltpu.VMEM((tm, D), BF16),
            pltpu.VMEM((tm, D), F32),
            pltpu.VMEM((tm + SUBLANES, tf), F32),
            pltpu.VMEM((nf, SUBLANES, tf), F32),
        ],
        compiler_params=_params(("arbitrary", "arbitrary", "arbitrary")),
        name="ffn_layer",
    )(*mix_args, h, g.reshape(1, D), wa, wb, conv_w, conv_b.reshape(1, F), w_out.astype(BF16),
      final_g.reshape(1, D))


def _nsa_proj_kernel(x_ref, g_ref, wq_ref, wkc_ref, wkv_ref, wg_ref, q_ref, t2_ref, kv_ref, gt_ref,
                     kc_scr):
    xn = _rmsnorm(x_ref[...], g_ref[...]).astype(BF16)
    q_ref[...] = (_dot(xn, wq_ref[...]) * (HEAD_DIM ** -0.5 * LOG2E)).astype(BF16)
    kc = _dot(xn, wkc_ref[...])
    nrow = t2_ref.shape[1]
    for j in range(kc_scr.shape[0]):
        kc_scr[j] = kc[:, j * LANES:(j + 1) * LANES]
        for i in range(CMP_STRIDE):
            t2_ref[j, :, i * LANES:(i + 1) * LANES] = (
                kc_scr.at[j][pl.ds(i, nrow, stride=CMP_STRIDE), :].astype(BF16))
    kv_ref[...] = _dot(xn, wkv_ref[...]).astype(BF16)
    gt_ref[...] = _sigmoid(_dot(xn, wg_ref[...]))


def _nsa_proj(h, g, wq, wkc, wkv, wg, *, tm=1024):
    B, S, D = h.shape
    nq, nkc, nkv, ng = wq.shape[1], wkc.shape[1], wkv.shape[1], wg.shape[1]
    nslab = nkc // LANES
    const = lambda shape: pl.BlockSpec(shape, lambda b, t: (0,) * len(shape))
    row = lambda n: pl.BlockSpec((None, tm, n), lambda b, t: (b, t, 0))
    return pl.pallas_call(
        _nsa_proj_kernel,
        grid=(B, S // tm),
        in_specs=[row(D), const((1, D)), const((D, nq)), const((D, nkc)), const((D, nkv)),
                  const((D, ng))],
        out_specs=[row(nq),
                   pl.BlockSpec((None, nslab, tm // CMP_STRIDE, CMP_STRIDE * LANES),
                                lambda b, t: (b, 0, t, 0)),
                   row(nkv), row(ng)],
        out_shape=[jax.ShapeDtypeStruct((B, S, nq), BF16),
                   jax.ShapeDtypeStruct((B, nslab, S // CMP_STRIDE, CMP_STRIDE * LANES), BF16),
                   jax.ShapeDtypeStruct((B, S, nkv), BF16),
                   jax.ShapeDtypeStruct((B, S, ng), F32)],
        scratch_shapes=[pltpu.VMEM((nslab, tm, LANES), F32)],
        compiler_params=_params(("arbitrary", "arbitrary")),
        name="nsa_proj",
    )(h, g.reshape(1, D), wq, wkc, wkv, wg)


def _cmp_kernel(t2_ref, pos_ref, wt_ref, wb_ref, b1_ref, w2_ref, o_ref):
    nb, nrow, K = t2_ref.shape
    t2 = t2_ref[...].reshape(nb * nrow, K)
    w_top = wt_ref[...]
    w_bot = wb_ref[...]
    pos = pos_ref[...].astype(BF16)
    bias = _dot(pos[:, :K], w_top) + _dot(pos[:, K:], w_bot) + b1_ref[...]
    hid = _dot(t2, w_top) + pltpu.roll(_dot(t2, w_bot), nb * nrow - 1, 0) + bias[0:1, :]
    out = _dot(_gelu(hid).astype(BF16), w2_ref[...])
    o_ref[...] = out.reshape(nb, nrow, out.shape[1]).astype(o_ref.dtype)


def _nsa_compress(t2, pos, w_top, w_bot, b1, w2, *, nb=4):
    B, nslab, nrow, K = t2.shape
    npair = nslab // 2
    H2 = w_top.shape[2]
    nb = math.gcd(nb, B)
    per_kv = lambda shape: pl.BlockSpec((None,) + shape, lambda s, p, b: (s,) + (0,) * len(shape))
    return pl.pallas_call(
        _cmp_kernel,
        grid=(2, npair, B // nb),
        in_specs=[
            pl.BlockSpec((nb, None, nrow, K), lambda s, p, b: (b, s * npair + p, 0, 0)),
            per_kv((SUBLANES, 2 * K)), per_kv((K, H2)), per_kv((K, H2)), per_kv((1, H2)),
            per_kv((H2, LANES)),
        ],
        out_specs=pl.BlockSpec((nb, None, nrow, LANES), lambda s, p, b: (b, s, 0, p)),
        out_shape=jax.ShapeDtypeStruct((B, 2, nrow, npair * LANES), BF16),
        compiler_params=_params(("arbitrary", "arbitrary", "arbitrary")),
        name="nsa_compress",
    )(t2, pos, w_top, w_bot, b1, w2)


NSLOT = 2 * GROUP
LOG2E = 1.4426950408889634
NPIECE = 4
SEL_LANE0 = 0
POS_LANE0 = 32
CMP_LANE0 = 40
QK_AHEAD = 3
SCORE_BUFS = 4
ONES_ROWS = 16


def _bf16_pieces(x, n):
    out, r = [], np.float64(x)
    for _ in range(n):
        p = np.float64(np.asarray(r, np.float32).astype(ml_dtypes.bfloat16).astype(np.float32))
        out.append(p)
        r = r - p
    return out


def _feature_base(gi):
    return HEAD_DIM if gi == 0 else 0


def _attn_tables(S, ncmp, tq):
    npair = N_KV // 2
    qf = np.zeros((npair, NSLOT, HEAD_DIM, tq), np.float32)
    for pair in range(npair):
        for slot in range(NSLOT):
            h = NSLOT * pair + slot
            for i, p in enumerate(_bf16_pieces(2.0 ** (-(h + 1) / 2.0) * LOG2E, NPIECE)):
                qf[pair, slot, POS_LANE0 + 2 * i] = SEL_BLOCK * p
                qf[pair, slot, POS_LANE0 + 2 * i + 1] = p
                qf[pair, slot, CMP_LANE0 + i] = CMP_STRIDE * p
    pos = np.arange(S)
    fs = np.zeros((2, S, LANES), np.float32)
    fw = np.zeros((2, S, LANES), np.float32)
    fc = np.zeros((2, ncmp, LANES), np.float32)
    for gi in range(2):
        f0 = _feature_base(gi)
        fs[gi, pos, f0 + SEL_LANE0 + pos // SEL_BLOCK] = 1.0
        for i in range(NPIECE):
            for f in (fs, fw):
                f[gi, :, f0 + POS_LANE0 + 2 * i] = pos // SEL_BLOCK
                f[gi, :, f0 + POS_LANE0 + 2 * i + 1] = pos % SEL_BLOCK
            fc[gi, :, f0 + CMP_LANE0 + i] = np.arange(ncmp)
    return (jnp.asarray(qf, BF16), jnp.asarray(fs, BF16), jnp.asarray(fw, BF16),
            jnp.asarray(fc, BF16))


def _attn_kernel(q_ref, ks_ref, vs_ref, kw_ref, vw_ref, kc_ref, vc_ref, gt_ref, ovt_ref, qf_ref,
                 fs_ref, fw_ref, fc_ref, o_ref, qaug_scr, kaug_scr, vt_scr, bias_scr, m_scr,
                 acc_scr, oc_scr, psum_scr, s_scr, p_scr, next_scr, *, tq, seq):
    tk = tq
    qt = pl.program_id(2)
    q0 = qt * tq
    ncmp = kc_ref.shape[0]
    nsel = seq // SEL_BLOCK
    ntile = seq // tk

    @pl.when(qt == 0)
    def _():
        own0 = lax.broadcasted_iota(jnp.int32, (tk, LANES), 1) < HEAD_DIM
        for br, k_ref, f_ref, v_ref in ((0, ks_ref, fs_ref, vs_ref), (1, kw_ref, fw_ref, vw_ref)):
            for j in range(ntile):
                rows = slice(j * tk, (j + 1) * tk)
                k2 = k_ref[rows, :]
                kaug_scr[br, 0, j] = jnp.where(own0, k2, f_ref[0, rows, :])
                kaug_scr[br, 1, j] = jnp.where(own0, f_ref[1, rows, :], k2)
                vt = v_ref[rows, :].astype(F32).T.astype(BF16)
                ones = jnp.ones((ONES_ROWS, tk), BF16)
                for gi in range(2):
                    vt_scr[br, j, gi] = jnp.concatenate(
                        [vt[gi * HEAD_DIM:(gi + 1) * HEAD_DIM, :], ones], axis=0)
        d = (lax.broadcasted_iota(jnp.int32, (tk, tq), 1)
             - lax.broadcasted_iota(jnp.int32, (tk, tq), 0))
        bias_scr[0] = jnp.where(d < 0, 0.0, NEG)
        bias_scr[1] = jnp.zeros((tk, tq), F32)
        bias_scr[2] = jnp.where(d >= 0, 0.0, NEG)

    m_scr[...] = jnp.full(m_scr.shape, NEG, F32)
    acc_scr[...] = jnp.zeros(acc_scr.shape, F32)

    for r in range(GROUP):
        qct = q_ref[:, r * LANES:(r + 1) * LANES].astype(F32).T.astype(BF16)
        qaug_scr[r] = jnp.concatenate([qct[0:HEAD_DIM, :], qf_ref[r]], axis=0)
        qaug_scr[GROUP + r] = jnp.concatenate([qf_ref[GROUP + r], qct[HEAD_DIM:, :]], axis=0)

    own0c = lax.broadcasted_iota(jnp.int32, (ncmp, LANES), 1) < HEAD_DIM
    kc2 = kc_ref[...]
    kcaug = [jnp.where(own0c, kc2, fc_ref[0]), jnp.where(own0c, fc_ref[1], kc2)]
    vct = vc_ref[...].astype(F32).T.astype(BF16)
    cend = lax.broadcasted_iota(jnp.int32, (ncmp, tq), 0) * CMP_STRIDE + (CMP_BLOCK - 1)
    cmp_bias = jnp.where(cend <= (q0 + lax.broadcasted_iota(jnp.int32, (ncmp, tq), 1)), 0.0, NEG)
    nbuf = s_scr.shape[0]
    assert NSLOT % nbuf == 0

    def issue_cmp_scores(slot):
        s_scr[slot % nbuf, 0:ncmp, :] = _dot(kcaug[slot // GROUP], qaug_scr[slot])

    def issue_scores(br, j, slot):
        s_scr[slot % nbuf] = _dot(kaug_scr[br, slot // GROUP, j], qaug_scr[slot])

    def cmp_values(slot):
        gi = slot // GROUP
        oc_scr[slot] = _dot(vct[gi * HEAD_DIM:(gi + 1) * HEAD_DIM, :], p_scr[slot % 2, 0:ncmp, :])

    win_lo = jnp.maximum(qt - 2, 0)
    for slot in range(QK_AHEAD):
        issue_cmp_scores(slot)
    psum = [None, None]
    for slot in range(NSLOT):
        gi = slot // GROUP
        sm = s_scr[slot % nbuf, 0:ncmp, :] + cmp_bias
        mx = jnp.max(sm, axis=0, keepdims=True)
        e = jnp.exp2(sm - mx)
        l = jnp.sum(e, axis=0, keepdims=True)
        p = e * jnp.where(mx > 0.5 * NEG, 1.0 / l, 0.0)
        psum[gi] = p if psum[gi] is None else psum[gi] + p
        p_scr[slot % 2, 0:ncmp, :] = p.astype(BF16)
        if slot + QK_AHEAD < NSLOT:
            issue_cmp_scores(slot + QK_AHEAD)
        else:
            issue_scores(1, win_lo, slot + QK_AHEAD - NSLOT)
        if slot >= 1:
            cmp_values(slot - 1)
    cmp_values(NSLOT - 1)
    for gi in range(2):
        psum_scr[gi] = psum[gi]

    def normalized(br, slot):
        return (acc_scr[br, slot, 0:HEAD_DIM, :]
                / acc_scr[br, slot, HEAD_DIM:HEAD_DIM + 1, :])

    def tile(br, j, ahead, bias, side_work=None):
        def weighted_values(slot, alpha):
            acc_scr[br, slot] = alpha * acc_scr[br, slot] + _dot(
                vt_scr[br, j, slot // GROUP], p_scr[slot % 2])

        alphas = []
        for slot in range(NSLOT):
            s = s_scr[slot % nbuf]
            if bias is not None:
                s = s + bias
            m_prev = m_scr[br, slot]
            m_new = jnp.maximum(m_prev, jnp.max(s, axis=0, keepdims=True))
            alphas.append(jnp.exp2(m_prev - m_new))
            p_scr[slot % 2] = jnp.exp2(s - m_new).astype(BF16)
            m_scr[br, slot] = m_new
            if slot + QK_AHEAD < NSLOT:
                issue_scores(br, j, slot + QK_AHEAD)
            elif ahead is not None:
                issue_scores(ahead[0], ahead[1], slot + QK_AHEAD - NSLOT)
            if slot >= 1:
                weighted_values(slot - 1, alphas[slot - 1])
            if side_work is not None:
                side_work(slot)
        weighted_values(NSLOT - 1, alphas[NSLOT - 1])

    def sweep(br, j_lo, bias_fn, last_ahead, last_side_work):
        def body(j, carry):
            tile(br, j, (br, j + 1), bias_fn(j))
            return carry

        lax.fori_loop(j_lo, qt, body, 0)
        tile(br, qt, last_ahead, bias_scr[2], last_side_work)

    jidx = lax.broadcasted_iota(jnp.int32, (nsel, tq), 0)
    ntop = min(N_SELECT, nsel)
    topk = {}

    def topk_start():
        ovt = ovt_ref[...]
        cur = (q0 + lax.broadcasted_iota(jnp.int32, (nsel, tq), 1)) // SEL_BLOCK
        forced = (jidx == 0) | (jidx == cur) | (jidx == cur - 1)
        future = jidx > cur
        for gi in range(2):
            ps = psum_scr[gi]
            p1 = ps.astype(BF16)
            r1 = ps - p1.astype(F32)
            p2 = r1.astype(BF16)
            p3 = (r1 - p2.astype(F32)).astype(BF16)
            imp = _dot(ovt, p1) + _dot(ovt, p2) + _dot(ovt, p3)
            topk[gi] = (jnp.where(forced, FORCED_BONUS, jnp.where(future, -1.0, imp)),
                        jnp.full((nsel, tq), NEG, F32))

    def topk_round():
        for gi in range(2):
            score, selneg = topk[gi]
            mx = jnp.max(score, axis=0, keepdims=True)
            cand = jnp.where(score == mx, jidx, nsel)
            first = jnp.min(cand, axis=0, keepdims=True)
            hit = jidx == first
            topk[gi] = (jnp.where(hit, -3e38, score), jnp.where(hit, 0.0, selneg))

    def topk_finish():
        for gi in range(2):
            f0 = _feature_base(gi) + SEL_LANE0
            selneg = topk[gi][1].astype(BF16)
            for r in range(GROUP):
                qaug_scr[gi * GROUP + r, f0:f0 + nsel, :] = selneg
        chosen = jnp.max(jnp.maximum(topk[0][1], topk[1][1]), axis=1, keepdims=True)
        blk_per_tile = tk // SEL_BLOCK
        assert blk_per_tile < 16 and ntile <= 8
        tile_of_blk = lax.broadcasted_iota(jnp.int32, (nsel, 1), 0) // blk_per_tile
        digits = jnp.sum(jnp.where(chosen == 0.0, jnp.left_shift(1, 4 * tile_of_blk), 0))
        nxt = qt
        count = jnp.int32(0)
        for j in reversed(range(ntile)):
            next_scr[j] = nxt
            hit = (jnp.right_shift(digits, 4 * j) & 15) != 0
            if j == 0:
                hit = True
            take = hit & (j < qt)
            nxt = jnp.where(take, j, nxt)
            count = count + take.astype(jnp.int32)
        topk['count'] = count

    def topk_side_work(slot):
        last = NSLOT - QK_AHEAD - 1
        if slot == 0:
            topk_start()
        for i in range(ntop):
            if i * (last + 1) // ntop == slot:
                topk_round()
        if slot == last:
            topk_finish()

    sweep(1, win_lo, lambda j: bias_scr[j - qt + 2], (0, 0), topk_side_work)

    def chosen_tile(_, j):
        j_next = next_scr[j]
        tile(0, j, (0, j_next), None)
        return j_next

    lax.fori_loop(0, topk['count'], chosen_tile, jnp.int32(0))
    tile(0, qt, None, bias_scr[2])

    gtt = gt_ref[...].T
    for r in range(GROUP):
        comb = []
        for gi in range(2):
            slot = gi * GROUP + r
            comb.append(gtt[slot:slot + 1, :] * oc_scr[slot]
                        + gtt[NSLOT + slot:NSLOT + slot + 1, :] * normalized(0, slot)
                        + gtt[2 * NSLOT + slot:2 * NSLOT + slot + 1, :] * normalized(1, slot))
        col = jnp.concatenate(comb, axis=0).T
        o_ref[:, r * LANES:(r + 1) * LANES] = col.astype(o_ref.dtype)


def _nsa_attention(q, kv, kvc, gates, *, tq=256):
    B, S, _ = q.shape
    npair = N_KV // 2
    ncmp = kvc.shape[2]
    nsel = S // SEL_BLOCK
    assert S % tq == 0 and WINDOW == 2 * tq and nsel <= SEL_LANE0 + POS_LANE0
    cstart = np.arange(ncmp) * CMP_STRIDE
    selj = np.arange(nsel)
    ovt = ((cstart[None, :] < (selj[:, None] + 1) * SEL_BLOCK)
           & (cstart[None, :] + CMP_BLOCK > selj[:, None] * SEL_BLOCK))
    ovt = jnp.asarray(ovt, BF16)
    qf, fs, fw, fc = _attn_tables(S, ncmp, tq)

    def kvspec(cb):
        return pl.BlockSpec((None, S, LANES), lambda b, p, t: (b, 0, cb + p))

    def cspec(s):
        return pl.BlockSpec((None, None, ncmp, LANES), lambda b, p, t: (b, s, 0, p))

    const = lambda shape: pl.BlockSpec(shape, lambda b, p, t: (0,) * len(shape))
    qw = GROUP * LANES
    ntile = S // tq
    return pl.pallas_call(
        functools.partial(_attn_kernel, tq=tq, seq=S),
        grid=(B, npair, S // tq),
        in_specs=[
            pl.BlockSpec((None, tq, qw), lambda b, p, t: (b, t, p)),
            kvspec(0), kvspec(2), kvspec(4), kvspec(6),
            cspec(0), cspec(1),
            pl.BlockSpec((None, tq, LANES), lambda b, p, t: (b, t, p)),
            const((nsel, ncmp)),
            pl.BlockSpec((None, NSLOT, HEAD_DIM, tq), lambda b, p, t: (p, 0, 0, 0)),
            const((2, S, LANES)), const((2, S, LANES)), const((2, ncmp, LANES)),
        ],
        out_specs=pl.BlockSpec((None, tq, qw), lambda b, p, t: (b, t, p)),
        out_shape=jax.ShapeDtypeStruct((B, S, npair * qw), BF16),
        scratch_shapes=[
            pltpu.VMEM((NSLOT, LANES, tq), BF16),
            pltpu.VMEM((2, 2, ntile, tq, LANES), BF16),
            pltpu.VMEM((2, ntile, 2, HEAD_DIM + ONES_ROWS, tq), BF16),
            pltpu.VMEM((3, tq, tq), F32),
            pltpu.VMEM((2, NSLOT, 1, tq), F32),
            pltpu.VMEM((2, NSLOT, HEAD_DIM + ONES_ROWS, tq), F32),
            pltpu.VMEM((NSLOT, HEAD_DIM, tq), F32),
            pltpu.VMEM((2, ncmp, tq), F32),
            pltpu.VMEM((SCORE_BUFS, tq, tq), F32),
            pltpu.VMEM((2, tq, tq), BF16),
            pltpu.SMEM((ntile,), jnp.int32),
        ],
        compiler_params=_params(("arbitrary", "arbitrary", "arbitrary")),
        name="nsa_attention",
    )(q, kv, kv, kv, kv, kvc, kvc, gates, ovt, qf, fs, fw, fc)


def _head_pair_perm():
    perm = np.zeros(N_HEADS * HEAD_DIM, np.int32)
    d = np.arange(HEAD_DIM)
    for pair in range(N_KV // 2):
        for r in range(GROUP):
            for gi in range(2):
                h = (2 * pair + gi) * GROUP + r
                n0 = pair * GROUP * LANES + r * LANES + gi * HEAD_DIM
                perm[n0 + d] = h * HEAD_DIM + d
    return perm


def _nsa_layer(h, g, w_in, cmp_pos, cmp_w1, cmp_b1, cmp_w2, w_out):
    B, S, D = h.shape
    qcols = N_HEADS * HEAD_DIM
    kvcols = 3 * 2 * N_KV * HEAD_DIM
    ngate = 3 * N_HEADS
    perm = _head_pair_perm()
    ccols = 2 * N_KV * HEAD_DIM
    wq = w_in[:, :qcols][:, perm].astype(BF16)
    wkc = w_in[:, qcols:qcols + ccols].astype(BF16)
    wkv = w_in[:, qcols + ccols:qcols + kvcols].astype(BF16)
    wg_src = w_in[:, qcols + kvcols:]
    wg = jnp.zeros((D, (N_KV // 2) * LANES), F32)
    for pair in range(N_KV // 2):
        for br in range(3):
            src = br * N_HEADS + pair * NSLOT
            dst = pair * LANES + br * NSLOT
            wg = wg.at[:, dst:dst + NSLOT].set(wg_src[:, src:src + NSLOT])
    wg = wg.astype(BF16)

    q, t2, kv, gates = _nsa_proj(h, g, wq, wkc, wkv, wg)

    hid = cmp_w1.shape[2]
    eye = jnp.eye(2, dtype=BF16)
    w1r = cmp_w1.astype(BF16).reshape(2, 2, CMP_STRIDE, HEAD_DIM, hid)
    w1p = jnp.einsum('stidh,gk->stigdkh', w1r, eye).reshape(2, 2, CMP_STRIDE * LANES, 2 * hid)
    posr = cmp_pos.reshape(2, 2, CMP_STRIDE, 1, HEAD_DIM)
    posp = jnp.broadcast_to(posr, (2, 2, CMP_STRIDE, 2, HEAD_DIM)).reshape(2, 1, -1)
    posp = jnp.broadcast_to(posp, (2, SUBLANES, posp.shape[2]))
    b1p = jnp.tile(cmp_b1, (1, 2)).reshape(2, 1, 2 * hid)
    w2p = jnp.einsum('shd,gk->sghkd', cmp_w2.astype(BF16), eye).reshape(2, 2 * hid, LANES)
    kvc = _nsa_compress(t2, posp, w1p[:, 0], w1p[:, 1], b1p, w2p)

    return _nsa_attention(q, kv, kvc, gates), w_out[perm, :].astype(BF16)


def kernel(x, lru_norm_g, lru_w_in, lru_conv_w, lru_conv_b, lru_gate_w, lru_gate_b, lru_a_param,
           lru_w_out, nsa_norm_g, nsa_w_in, nsa_cmp_pos, nsa_cmp_w1, nsa_cmp_b1, nsa_cmp_w2,
           nsa_w_out, ffn_norm_g, ffn_w_in, ffn_conv_w, ffn_conv_b, ffn_w_out, final_norm_g):
    h = _lru_layer(x, lru_norm_g[0], lru_w_in[0], lru_conv_w[0], lru_conv_b[0], lru_gate_w[0],
                   lru_gate_b[0], lru_a_param[0], lru_w_out[0])
    h = _ffn_layer(h, ffn_norm_g[0], ffn_w_in[0], ffn_conv_w[0], ffn_conv_b[0], ffn_w_out[0],
                   final_norm_g, final_norm=False)
    o, wo = _nsa_layer(h, nsa_norm_g[0], nsa_w_in[0], nsa_cmp_pos[0], nsa_cmp_w1[0],
                       nsa_cmp_b1[0], nsa_cmp_w2[0], nsa_w_out[0])
    return _ffn_layer(h, ffn_norm_g[1], ffn_w_in[1], ffn_conv_w[1], ffn_conv_b[1], ffn_w_out[1],
                      final_norm_g, final_norm=True, mix=o, wmix=wo)
```

```python
import functools
import math

import ml_dtypes
import numpy as np
import jax
import jax.numpy as jnp
from jax import lax
from jax.experimental import pallas as pl
from jax.experimental.pallas import tpu as pltpu

BF16 = jnp.bfloat16
F32 = jnp.float32

EPS = 1e-6
LRU_BLOCK_W = 128
LRU_CONV = 4
LRU_C = 8.0
N_HEADS = 16
HEAD_DIM = 64
N_KV = 4
GROUP = N_HEADS // N_KV
CMP_BLOCK = 32
CMP_STRIDE = 16
SEL_BLOCK = 64
N_SELECT = 8
WINDOW = 512
FORCED_BONUS = 1e4
NEG = -1e30
FFN_CONV = 3

LANES = 128
SUBLANES = 8
VMEM_LIMIT = 56 * 1024 * 1024


def _gelu(x):
    c = math.sqrt(2.0 / math.pi)
    inner = x * (c + (c * 0.044715) * (x * x))
    return (0.5 * x) * (1.0 + jnp.tanh(inner))


def _sigmoid(x):
    return 0.5 * jnp.tanh(0.5 * x) + 0.5


def _rmsnorm(x, g):
    return x * lax.rsqrt(jnp.mean(x * x, axis=-1, keepdims=True) + EPS) * g


def _dot(a, b):
    return jnp.dot(a, b, preferred_element_type=F32)


def _params(sem):
    return pltpu.CompilerParams(dimension_semantics=sem, vmem_limit_bytes=VMEM_LIMIT)


SCAN_CHUNK = SUBLANES * SUBLANES
PROJ_W = 2 * LANES


def _lru_kernel(x_ref, g_ref, wy_ref, wx_ref, cw_ref, cb_ref, gw_ref, gb_ref, ap_ref, wo_ref,
                o_ref, tail_scr, a_scr, b_scr, hc_scr, u_scr, yx_scr, *, tm, nblk):
    t = pl.program_id(1)

    @pl.when(t == 0)
    def _():
        tail_scr[...] = jnp.zeros(tail_scr.shape, F32)
        hc_scr[...] = jnp.zeros(hc_scr.shape, F32)

    x = x_ref[...]
    xn = _rmsnorm(x, g_ref[...]).astype(BF16)
    cw = cw_ref[...]
    cb = cb_ref[...]

    z = -ap_ref[...]
    c8 = -LRU_C * (jnp.maximum(z, 0.0) + jnp.log1p(jnp.exp(-jnp.abs(z))))
    gb = gb_ref[...]

    blk_per_chunk = PROJ_W // LANES
    nchunk = nblk // blk_per_chunk

    def project(i, which):
        cols = slice(i * PROJ_W, (i + 1) * PROJ_W)
        w_ref = wy_ref if which == 0 else wx_ref
        yx_scr[i % 2, which] = _dot(xn, w_ref[:, cols])

    sub_w = lax.broadcasted_iota(jnp.int32, (SUBLANES, PROJ_W), 0)

    def conv(i):
        cols = slice(i * PROJ_W, (i + 1) * PROJ_W)
        xb = yx_scr[i % 2, 1]
        tail = tail_scr[:, cols]
        xc = cw[LRU_CONV - 1:LRU_CONV, cols] * xb + cb[:, cols]
        for d in range(1, LRU_CONV):
            rolled = pltpu.roll(xb, d, 0)
            top = jnp.where(sub_w < d, pltpu.roll(tail, d, 0), rolled[0:SUBLANES, :])
            shifted = jnp.concatenate([top, rolled[SUBLANES:, :]], axis=0)
            xc = xc + cw[LRU_CONV - 1 - d:LRU_CONV - d, cols] * shifted
        tail_scr[:, cols] = xb[tm - SUBLANES:tm, :]
        return xc

    sub = lax.broadcasted_iota(jnp.int32, (SUBLANES, LANES), 0)
    out_split = (nchunk - 1) * PROJ_W
    out_half = out_split // 2

    def out_rows_a():
        o_ref[...] = x + _dot(u_scr[:, 0:out_half], wo_ref[0:out_half, :])

    def out_rows_b():
        o_ref[...] += _dot(u_scr[:, out_half:out_split], wo_ref[out_half:out_split, :])

    queued = [functools.partial(project, i, which)
              for i in range(1, nchunk) for which in (0, 1)] + [out_rows_a, out_rows_b]
    assert len(queued) == nblk
    project(0, 0)
    project(0, 1)
    for n in range(nblk):
        i, k = divmod(n, blk_per_chunk)
        if k == 0:
            xc = conv(i)
        lo, hi = n * LANES, (n + 1) * LANES
        xcn = xc[:, k * LANES:(k + 1) * LANES]
        xcb = xcn.astype(BF16)
        zr = _dot(xcb, gw_ref[0, n])
        zi = _dot(xcb, gw_ref[1, n])
        if n < len(queued):
            queued[n]()
        r = _sigmoid(zr + gb[0:1, lo:hi])
        ig = _sigmoid(zi + gb[1:2, lo:hi])
        log_a = c8[:, lo:hi] * r
        a = jnp.exp(log_a)
        w = -jnp.tanh(log_a) * (a * a + 1.0)
        mult = jnp.where(w > 0.0, w * lax.rsqrt(w), 0.0)
        a_scr[n] = a
        b_scr[n] = mult * ig * xcn

        carry = hc_scr[n]
        for c in range(tm // SCAN_CHUNK):
            base = c * SCAN_CHUNK
            acum, hloc = [], []
            for j in range(SUBLANES):
                aj = a_scr.at[n][pl.ds(base + j, SUBLANES, stride=SUBLANES), :]
                bj = b_scr.at[n][pl.ds(base + j, SUBLANES, stride=SUBLANES), :]
                if j == 0:
                    acum.append(aj)
                    hloc.append(bj)
                else:
                    hloc.append(aj * hloc[-1] + bj)
                    acum.append(aj * acum[-1])
            p, e = acum[-1], hloc[-1]
            for d in (1, 2, 4):
                keep = sub >= d
                psh = pltpu.roll(p, d, 0)
                esh = pltpu.roll(e, d, 0)
                e = jnp.where(keep, p * esh + e, e)
                p = jnp.where(keep, p * psh, p)
            hend = e + p * carry
            cin = jnp.where(sub == 0, carry, pltpu.roll(hend, 1, 0))
            for j in range(SUBLANES):
                b_scr.at[n][pl.ds(base + j, SUBLANES, stride=SUBLANES), :] = (
                    hloc[j] + acum[j] * cin)
            carry = jnp.broadcast_to(hend[SUBLANES - 1:SUBLANES, :], (SUBLANES, LANES))
        hc_scr[n] = carry
        y = _gelu(yx_scr[i % 2, 0, :, k * LANES:(k + 1) * LANES])
        u_scr[:, lo:hi] = (b_scr[n] * y).astype(BF16)

    o_ref[...] += _dot(u_scr[:, out_split:], wo_ref[out_split:, :])


def _lru_layer(h, g, w_in, conv_w, conv_b, gate_w, gate_b, a_param, w_out, *, tm=512):
    B, S, D = h.shape
    W = w_out.shape[0]
    nblk = W // LANES
    assert gate_w.shape == (2, nblk, LANES, LANES) and S % tm == 0 and tm % SCAN_CHUNK == 0
    wy = w_in[:, :W].astype(BF16)
    wx = w_in[:, W:].astype(BF16)
    const = lambda shape: pl.BlockSpec(shape, lambda b, t: (0,) * len(shape))
    return pl.pallas_call(
        functools.partial(_lru_kernel, tm=tm, nblk=nblk),
        grid=(B, S // tm),
        in_specs=[
            pl.BlockSpec((None, tm, D), lambda b, t: (b, t, 0)),
            const((1, D)), const((D, W)), const((D, W)), const((LRU_CONV, W)), const((1, W)),
            const((2, nblk, LANES, LANES)), const((2, W)), const((1, W)), const((W, D)),
        ],
        out_specs=pl.BlockSpec((None, tm, D), lambda b, t: (b, t, 0)),
        out_shape=jax.ShapeDtypeStruct((B, S, D), F32),
        scratch_shapes=[
            pltpu.VMEM((SUBLANES, W), F32),
            pltpu.VMEM((nblk, tm, LANES), F32),
            pltpu.VMEM((nblk, tm, LANES), F32),
            pltpu.VMEM((nblk, SUBLANES, LANES), F32),
            pltpu.VMEM((tm, W), BF16),
            pltpu.VMEM((2, 2, tm, PROJ_W), F32),
        ],
        compiler_params=_params(("arbitrary", "arbitrary")),
        name="lru_layer",
    )(h, g.reshape(1, D), wy, wx, conv_w, conv_b.reshape(1, W), gate_w.astype(BF16), gate_b,
      a_param.reshape(1, W), w_out.astype(BF16))


def _ffn_kernel(*refs, tm, final_norm, has_mix):
    if has_mix:
        mix_ref, wmix_ref, *refs = refs
    (x_ref, g_ref, wa_ref, wb_ref, cw_ref, cb_ref, wo_ref, fg_ref, o_ref,
     xn_scr, acc_scr, a_scr, halo_scr) = refs
    t = pl.program_id(1)
    f = pl.program_id(2)
    nf = pl.num_programs(2)

    @pl.when(f == 0)
    def _():
        x = x_ref[...]
        if has_mix:
            x = x + _dot(mix_ref[...], wmix_ref[...])
        xn_scr[...] = _rmsnorm(x, g_ref[...]).astype(BF16)
        acc_scr[...] = x

    @pl.when(t == 0)
    def _():
        halo_scr[f] = jnp.zeros(halo_scr.shape[1:], F32)

    xn = xn_scr[...]
    a = _dot(xn, wa_ref[...])
    b = _dot(xn, wb_ref[...])
    a_scr[0:SUBLANES, :] = halo_scr[f]
    a_scr[SUBLANES:SUBLANES + tm, :] = a
    cw = cw_ref[...]
    ac = (cw[2:3, :] * a + cw[1:2, :] * a_scr[7:7 + tm, :] + cw[0:1, :] * a_scr[6:6 + tm, :]
          + cb_ref[...])
    halo_scr[f] = a_scr[tm:tm + SUBLANES, :]
    hmid = (_gelu(ac) * b).astype(BF16)
    acc_scr[...] += _dot(hmid, wo_ref[...])

    @pl.when(f == nf - 1)
    def _():
        out = acc_scr[...]
        if final_norm:
            out = _rmsnorm(out, fg_ref[...])
        o_ref[...] = out


def _ffn_layer(h, g, w_in, conv_w, conv_b, w_out, final_g, *, final_norm, mix=None, wmix=None,
               tm=1024, tf=512):
    B, S, D = h.shape
    F = w_out.shape[0]
    assert S % tm == 0 and F % tf == 0
    nf = F // tf
    wa = w_in[:, :F].astype(BF16)
    wb = w_in[:, F:].astype(BF16)
    has_mix = mix is not None
    mix_specs, mix_args = [], []
    if has_mix:
        K = mix.shape[2]
        mix_specs = [pl.BlockSpec((None, tm, K), lambda b, t, f: (b, t, 0)),
                     pl.BlockSpec((K, D), lambda b, t, f: (0, 0))]
        mix_args = [mix, wmix]
    return pl.pallas_call(
        functools.partial(_ffn_kernel, tm=tm, final_norm=final_norm, has_mix=has_mix),
        grid=(B, S // tm, nf),
        in_specs=mix_specs + [
            pl.BlockSpec((None, tm, D), lambda b, t, f: (b, t, 0)),
            pl.BlockSpec((1, D), lambda b, t, f: (0, 0)),
            pl.BlockSpec((D, tf), lambda b, t, f: (0, f)),
            pl.BlockSpec((D, tf), lambda b, t, f: (0, f)),
            pl.BlockSpec((FFN_CONV, tf), lambda b, t, f: (0, f)),
            pl.BlockSpec((1, tf), lambda b, t, f: (0, f)),
            pl.BlockSpec((tf, D), lambda b, t, f: (f, 0)),
            pl.BlockSpec((1, D), lambda b, t, f: (0, 0)),
        ],
        out_specs=pl.BlockSpec((None, tm, D), lambda b, t, f: (b, t, 0)),
        out_shape=jax.ShapeDtypeStruct((B, S, D), F32),
        scratch_shapes=[
            pltpu.VMEM((tm, D), BF16),
            pltpu.VMEM((tm, D), F32),
            pltpu.VMEM((tm + SUBLANES, tf), F32),
            pltpu.VMEM((nf, SUBLANES, tf), F32),
        ],
        compiler_params=_params(("arbitrary", "arbitrary", "arbitrary")),
        name="ffn_layer",
    )(*mix_args, h, g.reshape(1, D), wa, wb, conv_w, conv_b.reshape(1, F), w_out.astype(BF16),
      final_g.reshape(1, D))


def _nsa_proj_kernel(x_ref, g_ref, wq_ref, wkc_ref, wkv_ref, wg_ref, q_ref, t2_ref, kv_ref, gt_ref,
                     kc_scr):
    xn = _rmsnorm(x_ref[...], g_ref[...]).astype(BF16)
    q_ref[...] = (_dot(xn, wq_ref[...]) * (HEAD_DIM ** -0.5 * LOG2E)).astype(BF16)
    kc = _dot(xn, wkc_ref[...])
    nrow = t2_ref.shape[1]
    for j in range(kc_scr.shape[0]):
        kc_scr[j] = kc[:, j * LANES:(j + 1) * LANES]
        for i in range(CMP_STRIDE):
            t2_ref[j, :, i * LANES:(i + 1) * LANES] = (
                kc_scr.at[j][pl.ds(i, nrow, stride=CMP_STRIDE), :].astype(BF16))
    kv_ref[...] = _dot(xn, wkv_ref[...]).astype(BF16)
    gt_ref[...] = _sigmoid(_dot(xn, wg_ref[...]))


def _nsa_proj(h, g, wq, wkc, wkv, wg, *, tm=1024):
    B, S, D = h.shape
    nq, nkc, nkv, ng = wq.shape[1], wkc.shape[1], wkv.shape[1], wg.shape[1]
    nslab = nkc // LANES
    const = lambda shape: pl.BlockSpec(shape, lambda b, t: (0,) * len(shape))
    row = lambda n: pl.BlockSpec((None, tm, n), lambda b, t: (b, t, 0))
    return pl.pallas_call(
        _nsa_proj_kernel,
        grid=(B, S // tm),
        in_specs=[row(D), const((1, D)), const((D, nq)), const((D, nkc)), const((D, nkv)),
                  const((D, ng))],
        out_specs=[row(nq),
                   pl.BlockSpec((None, nslab, tm // CMP_STRIDE, CMP_STRIDE * LANES),
                                lambda b, t: (b, 0, t, 0)),
                   row(nkv), row(ng)],
        out_shape=[jax.ShapeDtypeStruct((B, S, nq), BF16),
                   jax.ShapeDtypeStruct((B, nslab, S // CMP_STRIDE, CMP_STRIDE * LANES), BF16),
                   jax.ShapeDtypeStruct((B, S, nkv), BF16),
                   jax.ShapeDtypeStruct((B, S, ng), F32)],
        scratch_shapes=[pltpu.VMEM((nslab, tm, LANES), F32)],
        compiler_params=_params(("arbitrary", "arbitrary")),
        name="nsa_proj",
    )(h, g.reshape(1, D), wq, wkc, wkv, wg)


def _cmp_kernel(t2_ref, pos_ref, wt_ref, wb_ref, b1_ref, w2_ref, o_ref):
    nb, nrow, K = t2_ref.shape
    t2 = t2_ref[...].reshape(nb * nrow, K)
    w_top = wt_ref[...]
    w_bot = wb_ref[...]
    pos = pos_ref[...].astype(BF16)
    bias = _dot(pos[:, :K], w_top) + _dot(pos[:, K:], w_bot) + b1_ref[...]
    hid = _dot(t2, w_top) + pltpu.roll(_dot(t2, w_bot), nb * nrow - 1, 0) + bias[0:1, :]
    out = _dot(_gelu(hid).astype(BF16), w2_ref[...])
    o_ref[...] = out.reshape(nb, nrow, out.shape[1]).astype(o_ref.dtype)


def _nsa_compress(t2, pos, w_top, w_bot, b1, w2, *, nb=4):
    B, nslab, nrow, K = t2.shape
    npair = nslab // 2
    H2 = w_top.shape[2]
    nb = math.gcd(nb, B)
    per_kv = lambda shape: pl.BlockSpec((None,) + shape, lambda s, p, b: (s,) + (0,) * len(shape))
    return pl.pallas_call(
        _cmp_kernel,
        grid=(2, npair, B // nb),
        in_specs=[
            pl.BlockSpec((nb, None, nrow, K), lambda s, p, b: (b, s * npair + p, 0, 0)),
            per_kv((SUBLANES, 2 * K)), per_kv((K, H2)), per_kv((K, H2)), per_kv((1, H2)),
            per_kv((H2, LANES)),
        ],
        out_specs=pl.BlockSpec((nb, None, nrow, LANES), lambda s, p, b: (b, s, 0, p)),
        out_shape=jax.ShapeDtypeStruct((B, 2, nrow, npair * LANES), BF16),
        compiler_params=_params(("arbitrary", "arbitrary", "arbitrary")),
        name="nsa_compress",
    )(t2, pos, w_top, w_bot, b1, w2)


NSLOT = 2 * GROUP
LOG2E = 1.4426950408889634
NPIECE = 4
SEL_LANE0 = 0
POS_LANE0 = 32
CMP_LANE0 = 40
QK_AHEAD = 3
SCORE_BUFS = 4
PROB_BUFS = 4
ONES_ROWS = 16


def _bf16_pieces(x, n):
    out, r = [], np.float64(x)
    for _ in range(n):
        p = np.float64(np.asarray(r, np.float32).astype(ml_dtypes.bfloat16).astype(np.float32))
        out.append(p)
        r = r - p
    return out


def _feature_base(gi):
    return HEAD_DIM if gi == 0 else 0


def _attn_tables(S, ncmp, tq):
    npair = N_KV // 2
    qf = np.zeros((npair, NSLOT, HEAD_DIM, tq), np.float32)
    for pair in range(npair):
        for slot in range(NSLOT):
            h = NSLOT * pair + slot
            for i, p in enumerate(_bf16_pieces(2.0 ** (-(h + 1) / 2.0) * LOG2E, NPIECE)):
                qf[pair, slot, POS_LANE0 + 2 * i] = SEL_BLOCK * p
                qf[pair, slot, POS_LANE0 + 2 * i + 1] = p
                qf[pair, slot, CMP_LANE0 + i] = CMP_STRIDE * p
    pos = np.arange(S)
    fs = np.zeros((2, S, LANES), np.float32)
    fw = np.zeros((2, S, LANES), np.float32)
    fc = np.zeros((2, ncmp, LANES), np.float32)
    for gi in range(2):
        f0 = _feature_base(gi)
        fs[gi, pos, f0 + SEL_LANE0 + pos // SEL_BLOCK] = 1.0
        for i in range(NPIECE):
            for f in (fs, fw):
                f[gi, :, f0 + POS_LANE0 + 2 * i] = pos // SEL_BLOCK
                f[gi, :, f0 + POS_LANE0 + 2 * i + 1] = pos % SEL_BLOCK
            fc[gi, :, f0 + CMP_LANE0 + i] = np.arange(ncmp)
    return (jnp.asarray(qf, BF16), jnp.asarray(fs, BF16), jnp.asarray(fw, BF16),
            jnp.asarray(fc, BF16))


def _attn_kernel(q_ref, ks_ref, vs_ref, kw_ref, vw_ref, kc_ref, vc_ref, gt_ref, ovt_ref, qf_ref,
                 fs_ref, fw_ref, fc_ref, o_ref, qaug_scr, kaug_scr, vt_scr, bias_scr, m_scr,
                 acc_scr, oc_scr, psum_scr, s_scr, p_scr, next_scr, *, tq, seq):
    tk = tq
    qt = pl.program_id(2)
    q0 = qt * tq
    ncmp = kc_ref.shape[0]
    nsel = seq // SEL_BLOCK
    ntile = seq // tk

    @pl.when(qt == 0)
    def _():
        own0 = lax.broadcasted_iota(jnp.int32, (tk, LANES), 1) < HEAD_DIM
        for br, k_ref, f_ref, v_ref in ((0, ks_ref, fs_ref, vs_ref), (1, kw_ref, fw_ref, vw_ref)):
            for j in range(ntile):
                rows = slice(j * tk, (j + 1) * tk)
                k2 = k_ref[rows, :]
                kaug_scr[br, 0, j] = jnp.where(own0, k2, f_ref[0, rows, :])
                kaug_scr[br, 1, j] = jnp.where(own0, f_ref[1, rows, :], k2)
                vt = v_ref[rows, :].astype(F32).T.astype(BF16)
                ones = jnp.ones((ONES_ROWS, tk), BF16)
                for gi in range(2):
                    vt_scr[br, j, gi] = jnp.concatenate(
                        [vt[gi * HEAD_DIM:(gi + 1) * HEAD_DIM, :], ones], axis=0)
        d = (lax.broadcasted_iota(jnp.int32, (tk, tq), 1)
             - lax.broadcasted_iota(jnp.int32, (tk, tq), 0))
        bias_scr[0] = jnp.where(d < 0, 0.0, NEG)
        bias_scr[1] = jnp.zeros((tk, tq), F32)
        bias_scr[2] = jnp.where(d >= 0, 0.0, NEG)

    m_scr[...] = jnp.full(m_scr.shape, NEG, F32)
    acc_scr[...] = jnp.zeros(acc_scr.shape, F32)

    for r in range(GROUP):
        qct = q_ref[:, r * LANES:(r + 1) * LANES].astype(F32).T.astype(BF16)
        qaug_scr[r] = jnp.concatenate([qct[0:HEAD_DIM, :], qf_ref[r]], axis=0)
        qaug_scr[GROUP + r] = jnp.concatenate([qf_ref[GROUP + r], qct[HEAD_DIM:, :]], axis=0)

    own0c = lax.broadcasted_iota(jnp.int32, (ncmp, LANES), 1) < HEAD_DIM
    kc2 = kc_ref[...]
    kcaug = [jnp.where(own0c, kc2, fc_ref[0]), jnp.where(own0c, fc_ref[1], kc2)]
    vct = vc_ref[...].astype(F32).T.astype(BF16)
    cend = lax.broadcasted_iota(jnp.int32, (ncmp, tq), 0) * CMP_STRIDE + (CMP_BLOCK - 1)
    cmp_bias = jnp.where(cend <= (q0 + lax.broadcasted_iota(jnp.int32, (ncmp, tq), 1)), 0.0, NEG)
    nbuf = s_scr.shape[0]
    assert NSLOT % nbuf == 0

    def issue_cmp_scores(slot):
        s_scr[slot % nbuf, 0:ncmp, :] = _dot(kcaug[slot // GROUP], qaug_scr[slot])

    def issue_scores(br, j, slot):
        s_scr[slot % nbuf] = _dot(kaug_scr[br, slot // GROUP, j], qaug_scr[slot])

    def cmp_values(slot):
        gi = slot // GROUP
        oc_scr[slot] = _dot(vct[gi * HEAD_DIM:(gi + 1) * HEAD_DIM, :], p_scr[slot % PROB_BUFS, 0:ncmp, :])

    win_lo = jnp.maximum(qt - 2, 0)
    for slot in range(QK_AHEAD):
        issue_cmp_scores(slot)
    psum = [None, None]
    for slot in range(NSLOT):
        gi = slot // GROUP
        sm = s_scr[slot % nbuf, 0:ncmp, :] + cmp_bias
        mx = jnp.max(sm, axis=0, keepdims=True)
        e = jnp.exp2(sm - mx)
        l = jnp.sum(e, axis=0, keepdims=True)
        p = e * jnp.where(mx > 0.5 * NEG, 1.0 / l, 0.0)
        psum[gi] = p if psum[gi] is None else psum[gi] + p
        p_scr[slot % PROB_BUFS, 0:ncmp, :] = p.astype(BF16)
        if slot + QK_AHEAD < NSLOT:
            issue_cmp_scores(slot + QK_AHEAD)
        else:
            issue_scores(1, win_lo, slot + QK_AHEAD - NSLOT)
        if slot >= 1:
            cmp_values(slot - 1)
    cmp_values(NSLOT - 1)
    for gi in range(2):
        psum_scr[gi] = psum[gi]

    def normalized(br, slot):
        return (acc_scr[br, slot, 0:HEAD_DIM, :]
                / acc_scr[br, slot, HEAD_DIM:HEAD_DIM + 1, :])

    def tile(br, j, ahead, bias, side_work=None):
        def weighted_values(slot, alpha):
            acc_scr[br, slot] = alpha * acc_scr[br, slot] + _dot(
                vt_scr[br, j, slot // GROUP], p_scr[slot % PROB_BUFS])

        alphas = []
        for slot in range(NSLOT):
            s = s_scr[slot % nbuf]
            if bias is not None:
                s = s + bias
            m_prev = m_scr[br, slot]
            m_new = jnp.maximum(m_prev, jnp.max(s, axis=0, keepdims=True))
            alphas.append(jnp.exp2(m_prev - m_new))
            p_scr[slot % PROB_BUFS] = jnp.exp2(s - m_new).astype(BF16)
            m_scr[br, slot] = m_new
            if slot + QK_AHEAD < NSLOT:
                issue_scores(br, j, slot + QK_AHEAD)
            elif ahead is not None:
                issue_scores(ahead[0], ahead[1], slot + QK_AHEAD - NSLOT)
            if slot >= 1:
                weighted_values(slot - 1, alphas[slot - 1])
            if side_work is not None:
                side_work(slot)
        weighted_values(NSLOT - 1, alphas[NSLOT - 1])

    def sweep(br, j_lo, bias_fn, last_ahead, last_side_work):
        def body(j, carry):
            tile(br, j, (br, j + 1), bias_fn(j))
            return carry

        lax.fori_loop(j_lo, qt, body, 0)
        tile(br, qt, last_ahead, bias_scr[2], last_side_work)

    jidx = lax.broadcasted_iota(jnp.int32, (nsel, tq), 0)
    ntop = min(N_SELECT, nsel)
    topk = {}

    def topk_start():
        ovt = ovt_ref[...]
        cur = (q0 + lax.broadcasted_iota(jnp.int32, (nsel, tq), 1)) // SEL_BLOCK
        forced = (jidx == 0) | (jidx == cur) | (jidx == cur - 1)
        future = jidx > cur
        for gi in range(2):
            ps = psum_scr[gi]
            p1 = ps.astype(BF16)
            r1 = ps - p1.astype(F32)
            p2 = r1.astype(BF16)
            p3 = (r1 - p2.astype(F32)).astype(BF16)
            imp = _dot(ovt, p1) + _dot(ovt, p2) + _dot(ovt, p3)
            topk[gi] = (jnp.where(forced, FORCED_BONUS, jnp.where(future, -1.0, imp)),
                        jnp.full((nsel, tq), NEG, F32))

    def topk_round():
        for gi in range(2):
            score, selneg = topk[gi]
            mx = jnp.max(score, axis=0, keepdims=True)
            cand = jnp.where(score == mx, jidx, nsel)
            first = jnp.min(cand, axis=0, keepdims=True)
            hit = jidx == first
            topk[gi] = (jnp.where(hit, -3e38, score), jnp.where(hit, 0.0, selneg))

    def topk_finish():
        for gi in range(2):
            f0 = _feature_base(gi) + SEL_LANE0
            selneg = topk[gi][1].astype(BF16)
            for r in range(GROUP):
                qaug_scr[gi * GROUP + r, f0:f0 + nsel, :] = selneg
        chosen = jnp.max(jnp.maximum(topk[0][1], topk[1][1]), axis=1, keepdims=True)
        blk_per_tile = tk // SEL_BLOCK
        assert blk_per_tile < 16 and ntile <= 8
        tile_of_blk = lax.broadcasted_iota(jnp.int32, (nsel, 1), 0) // blk_per_tile
        digits = jnp.sum(jnp.where(chosen == 0.0, jnp.left_shift(1, 4 * tile_of_blk), 0))
        nxt = qt
        count = jnp.int32(0)
        for j in reversed(range(ntile)):
            next_scr[j] = nxt
            hit = (jnp.right_shift(digits, 4 * j) & 15) != 0
            if j == 0:
                hit = True
            take = hit & (j < qt)
            nxt = jnp.where(take, j, nxt)
            count = count + take.astype(jnp.int32)
        topk['count'] = count

    def topk_side_work(slot):
        last = NSLOT - QK_AHEAD - 1
        if slot == 0:
            topk_start()
        for i in range(ntop):
            if i * (last + 1) // ntop == slot:
                topk_round()
        if slot == last:
            topk_finish()

    sweep(1, win_lo, lambda j: bias_scr[j - qt + 2], (0, 0), topk_side_work)

    def chosen_tile(_, j):
        j_next = next_scr[j]
        tile(0, j, (0, j_next), None)
        return j_next

    lax.fori_loop(0, topk['count'], chosen_tile, jnp.int32(0))
    tile(0, qt, None, bias_scr[2])

    gtt = gt_ref[...].T
    for r in range(GROUP):
        comb = []
        for gi in range(2):
            slot = gi * GROUP + r
            comb.append(gtt[slot:slot + 1, :] * oc_scr[slot]
                        + gtt[NSLOT + slot:NSLOT + slot + 1, :] * normalized(0, slot)
                        + gtt[2 * NSLOT + slot:2 * NSLOT + slot + 1, :] * normalized(1, slot))
        col = jnp.concatenate(comb, axis=0).T
        o_ref[:, r * LANES:(r + 1) * LANES] = col.astype(o_ref.dtype)


def _nsa_attention(q, kv, kvc, gates, *, tq=256):
    B, S, _ = q.shape
    npair = N_KV // 2
    ncmp = kvc.shape[2]
    nsel = S // SEL_BLOCK
    assert S % tq == 0 and WINDOW == 2 * tq and nsel <= SEL_LANE0 + POS_LANE0
    cstart = np.arange(ncmp) * CMP_STRIDE
    selj = np.arange(nsel)
    ovt = ((cstart[None, :] < (selj[:, None] + 1) * SEL_BLOCK)
           & (cstart[None, :] + CMP_BLOCK > selj[:, None] * SEL_BLOCK))
    ovt = jnp.asarray(ovt, BF16)
    qf, fs, fw, fc = _attn_tables(S, ncmp, tq)

    def kvspec(cb):
        return pl.BlockSpec((None, S, LANES), lambda b, p, t: (b, 0, cb + p))

    def cspec(s):
        return pl.BlockSpec((None, None, ncmp, LANES), lambda b, p, t: (b, s, 0, p))

    const = lambda shape: pl.BlockSpec(shape, lambda b, p, t: (0,) * len(shape))
    qw = GROUP * LANES
    ntile = S // tq
    return pl.pallas_call(
        functools.partial(_attn_kernel, tq=tq, seq=S),
        grid=(B, npair, S // tq),
        in_specs=[
            pl.BlockSpec((None, tq, qw), lambda b, p, t: (b, t, p)),
            kvspec(0), kvspec(2), kvspec(4), kvspec(6),
            cspec(0), cspec(1),
            pl.BlockSpec((None, tq, LANES), lambda b, p, t: (b, t, p)),
            const((nsel, ncmp)),
            pl.BlockSpec((None, NSLOT, HEAD_DIM, tq), lambda b, p, t: (p, 0, 0, 0)),
            const((2, S, LANES)), const((2, S, LANES)), const((2, ncmp, LANES)),
        ],
        out_specs=pl.BlockSpec((None, tq, qw), lambda b, p, t: (b, t, p)),
        out_shape=jax.ShapeDtypeStruct((B, S, npair * qw), BF16),
        scratch_shapes=[
            pltpu.VMEM((NSLOT, LANES, tq), BF16),
            pltpu.VMEM((2, 2, ntile, tq, LANES), BF16),
            pltpu.VMEM((2, ntile, 2, HEAD_DIM + ONES_ROWS, tq), BF16),
            pltpu.VMEM((3, tq, tq), F32),
            pltpu.VMEM((2, NSLOT, 1, tq), F32),
            pltpu.VMEM((2, NSLOT, HEAD_DIM + ONES_ROWS, tq), F32),
            pltpu.VMEM((NSLOT, HEAD_DIM, tq), F32),
            pltpu.VMEM((2, ncmp, tq), F32),
            pltpu.VMEM((SCORE_BUFS, tq, tq), F32),
            pltpu.VMEM((PROB_BUFS, tq, tq), BF16),
            pltpu.SMEM((ntile,), jnp.int32),
        ],
        compiler_params=_params(("arbitrary", "arbitrary", "arbitrary")),
        name="nsa_attention",
    )(q, kv, kv, kv, kv, kvc, kvc, gates, ovt, qf, fs, fw, fc)


def _head_pair_perm():
    perm = np.zeros(N_HEADS * HEAD_DIM, np.int32)
    d = np.arange(HEAD_DIM)
    for pair in range(N_KV // 2):
        for r in range(GROUP):
            for gi in range(2):
                h = (2 * pair + gi) * GROUP + r
                n0 = pair * GROUP * LANES + r * LANES + gi * HEAD_DIM
                perm[n0 + d] = h * HEAD_DIM + d
    return perm


def _nsa_layer(h, g, w_in, cmp_pos, cmp_w1, cmp_b1, cmp_w2, w_out):
    B, S, D = h.shape
    qcols = N_HEADS * HEAD_DIM
    kvcols = 3 * 2 * N_KV * HEAD_DIM
    ngate = 3 * N_HEADS
    perm = _head_pair_perm()
    ccols = 2 * N_KV * HEAD_DIM
    wq = w_in[:, :qcols][:, perm].astype(BF16)
    wkc = w_in[:, qcols:qcols + ccols].astype(BF16)
    wkv = w_in[:, qcols + ccols:qcols + kvcols].astype(BF16)
    wg_src = w_in[:, qcols + kvcols:]
    wg = jnp.zeros((D, (N_KV // 2) * LANES), F32)
    for pair in range(N_KV // 2):
        for br in range(3):
            src = br * N_HEADS + pair * NSLOT
            dst = pair * LANES + br * NSLOT
            wg = wg.at[:, dst:dst + NSLOT].set(wg_src[:, src:src + NSLOT])
    wg = wg.astype(BF16)

    q, t2, kv, gates = _nsa_proj(h, g, wq, wkc, wkv, wg)

    hid = cmp_w1.shape[2]
    eye = jnp.eye(2, dtype=BF16)
    w1r = cmp_w1.astype(BF16).reshape(2, 2, CMP_STRIDE, HEAD_DIM, hid)
    w1p = jnp.einsum('stidh,gk->stigdkh', w1r, eye).reshape(2, 2, CMP_STRIDE * LANES, 2 * hid)
    posr = cmp_pos.reshape(2, 2, CMP_STRIDE, 1, HEAD_DIM)
    posp = jnp.broadcast_to(posr, (2, 2, CMP_STRIDE, 2, HEAD_DIM)).reshape(2, 1, -1)
    posp = jnp.broadcast_to(posp, (2, SUBLANES, posp.shape[2]))
    b1p = jnp.tile(cmp_b1, (1, 2)).reshape(2, 1, 2 * hid)
    w2p = jnp.einsum('shd,gk->sghkd', cmp_w2.astype(BF16), eye).reshape(2, 2 * hid, LANES)
    kvc = _nsa_compress(t2, posp, w1p[:, 0], w1p[:, 1], b1p, w2p)

    return _nsa_attention(q, kv, kvc, gates), w_out[perm, :].astype(BF16)


def kernel(x, lru_norm_g, lru_w_in, lru_conv_w, lru_conv_b, lru_gate_w, lru_gate_b, lru_a_param,
           lru_w_out, nsa_norm_g, nsa_w_in, nsa_cmp_pos, nsa_cmp_w1, nsa_cmp_b1, nsa_cmp_w2,
           nsa_w_out, ffn_norm_g, ffn_w_in, ffn_conv_w, ffn_conv_b, ffn_w_out, final_norm_g):
    h = _lru_layer(x, lru_norm_g[0], lru_w_in[0], lru_conv_w[0], lru_conv_b[0], lru_gate_w[0],
                   lru_gate_b[0], lru_a_param[0], lru_w_out[0])
    h = _ffn_layer(h, ffn_norm_g[0], ffn_w_in[0], ffn_conv_w[0], ffn_conv_b[0], ffn_w_out[0],
                   final_norm_g, final_norm=False)
    o, wo = _nsa_layer(h, nsa_norm_g[0], nsa_w_in[0], nsa_cmp_pos[0], nsa_cmp_w1[0],
                       nsa_cmp_b1[0], nsa_cmp_w2[0], nsa_w_out[0])
    return _ffn_layer(h, ffn_norm_g[1], ffn_w_in[1], ffn_conv_w[1], ffn_conv_b[1], ffn_w_out[1],
                      final_norm_g, final_norm=True, mix=o, wmix=wo)
```

```python
import functools
import math

import ml_dtypes
import numpy as np
import jax
import jax.numpy as jnp
from jax import lax
from jax.experimental import pallas as pl
from jax.experimental.pallas import tpu as pltpu

BF16 = jnp.bfloat16
F32 = jnp.float32

EPS = 1e-6
LRU_BLOCK_W = 128
LRU_CONV = 4
LRU_C = 8.0
N_HEADS = 16
HEAD_DIM = 64
N_KV = 4
GROUP = N_HEADS // N_KV
CMP_BLOCK = 32
CMP_STRIDE = 16
SEL_BLOCK = 64
N_SELECT = 8
WINDOW = 512
FORCED_BONUS = 1e4
NEG = -1e30
FFN_CONV = 3

LANES = 128
SUBLANES = 8
VMEM_LIMIT = 56 * 1024 * 1024


def _gelu(x):
    c = math.sqrt(2.0 / math.pi)
    inner = x * (c + (c * 0.044715) * (x * x))
    return (0.5 * x) * (1.0 + jnp.tanh(inner))


def _sigmoid(x):
    return 0.5 * jnp.tanh(0.5 * x) + 0.5


def _rmsnorm(x, g):
    return x * lax.rsqrt(jnp.mean(x * x, axis=-1, keepdims=True) + EPS) * g


def _dot(a, b):
    return jnp.dot(a, b, preferred_element_type=F32)


def _params(sem):
    return pltpu.CompilerParams(dimension_semantics=sem, vmem_limit_bytes=VMEM_LIMIT)


SCAN_CHUNK = SUBLANES * SUBLANES
PROJ_W = 2 * LANES


def _lru_kernel(x_ref, g_ref, wy_ref, wx_ref, cw_ref, cb_ref, gw_ref, gb_ref, ap_ref, wo_ref,
                o_ref, tail_scr, a_scr, b_scr, hc_scr, u_scr, yx_scr, *, tm, nblk):
    t = pl.program_id(1)

    @pl.when(t == 0)
    def _():
        tail_scr[...] = jnp.zeros(tail_scr.shape, F32)
        hc_scr[...] = jnp.zeros(hc_scr.shape, F32)

    x = x_ref[...]
    xn = _rmsnorm(x, g_ref[...]).astype(BF16)
    cw = cw_ref[...]
    cb = cb_ref[...]

    z = -ap_ref[...]
    c8 = -LRU_C * (jnp.maximum(z, 0.0) + jnp.log1p(jnp.exp(-jnp.abs(z))))
    gb = gb_ref[...]

    blk_per_chunk = PROJ_W // LANES
    nchunk = nblk // blk_per_chunk

    def project(i, which):
        cols = slice(i * PROJ_W, (i + 1) * PROJ_W)
        w_ref = wy_ref if which == 0 else wx_ref
        yx_scr[i % 2, which] = _dot(xn, w_ref[:, cols])

    sub_w = lax.broadcasted_iota(jnp.int32, (SUBLANES, PROJ_W), 0)

    def conv(i):
        cols = slice(i * PROJ_W, (i + 1) * PROJ_W)
        xb = yx_scr[i % 2, 1]
        tail = tail_scr[:, cols]
        xc = cw[LRU_CONV - 1:LRU_CONV, cols] * xb + cb[:, cols]
        for d in range(1, LRU_CONV):
            rolled = pltpu.roll(xb, d, 0)
            top = jnp.where(sub_w < d, pltpu.roll(tail, d, 0), rolled[0:SUBLANES, :])
            shifted = jnp.concatenate([top, rolled[SUBLANES:, :]], axis=0)
            xc = xc + cw[LRU_CONV - 1 - d:LRU_CONV - d, cols] * shifted
        tail_scr[:, cols] = xb[tm - SUBLANES:tm, :]
        return xc

    sub = lax.broadcasted_iota(jnp.int32, (SUBLANES, LANES), 0)
    out_split = (nchunk - 1) * PROJ_W
    out_half = out_split // 2

    def out_rows_a():
        o_ref[...] = x + _dot(u_scr[:, 0:out_half], wo_ref[0:out_half, :])

    def out_rows_b():
        o_ref[...] += _dot(u_scr[:, out_half:out_split], wo_ref[out_half:out_split, :])

    queued = [functools.partial(project, i, which)
              for i in range(1, nchunk) for which in (0, 1)] + [out_rows_a, out_rows_b]
    assert len(queued) == nblk
    project(0, 0)
    project(0, 1)
    for n in range(nblk):
        i, k = divmod(n, blk_per_chunk)
        if k == 0:
            xc = conv(i)
        lo, hi = n * LANES, (n + 1) * LANES
        xcn = xc[:, k * LANES:(k + 1) * LANES]
        xcb = xcn.astype(BF16)
        zr = _dot(xcb, gw_ref[0, n])
        zi = _dot(xcb, gw_ref[1, n])
        if n < len(queued):
            queued[n]()
        r = _sigmoid(zr + gb[0:1, lo:hi])
        ig = _sigmoid(zi + gb[1:2, lo:hi])
        log_a = c8[:, lo:hi] * r
        a = jnp.exp(log_a)
        w = -jnp.tanh(log_a) * (a * a + 1.0)
        mult = jnp.where(w > 0.0, w * lax.rsqrt(w), 0.0)
        a_scr[n] = a
        b_scr[n] = mult * ig * xcn

        carry = hc_scr[n]
        for c in range(tm // SCAN_CHUNK):
            base = c * SCAN_CHUNK
            acum, hloc = [], []
            for j in range(SUBLANES):
                aj = a_scr.at[n][pl.ds(base + j, SUBLANES, stride=SUBLANES), :]
                bj = b_scr.at[n][pl.ds(base + j, SUBLANES, stride=SUBLANES), :]
                if j == 0:
                    acum.append(aj)
                    hloc.append(bj)
                else:
                    hloc.append(aj * hloc[-1] + bj)
                    acum.append(aj * acum[-1])
            p, e = acum[-1], hloc[-1]
            for d in (1, 2, 4):
                keep = sub >= d
                psh = pltpu.roll(p, d, 0)
                esh = pltpu.roll(e, d, 0)
                e = jnp.where(keep, p * esh + e, e)
                p = jnp.where(keep, p * psh, p)
            hend = e + p * carry
            cin = jnp.where(sub == 0, carry, pltpu.roll(hend, 1, 0))
            for j in range(SUBLANES):
                b_scr.at[n][pl.ds(base + j, SUBLANES, stride=SUBLANES), :] = (
                    hloc[j] + acum[j] * cin)
            carry = jnp.broadcast_to(hend[SUBLANES - 1:SUBLANES, :], (SUBLANES, LANES))
        hc_scr[n] = carry
        y = _gelu(yx_scr[i % 2, 0, :, k * LANES:(k + 1) * LANES])
        u_scr[:, lo:hi] = (b_scr[n] * y).astype(BF16)

    o_ref[...] += _dot(u_scr[:, out_split:], wo_ref[out_split:, :])


def _lru_layer(h, g, w_in, conv_w, conv_b, gate_w, gate_b, a_param, w_out, *, tm=512):
    B, S, D = h.shape
    W = w_out.shape[0]
    nblk = W // LANES
    assert gate_w.shape == (2, nblk, LANES, LANES) and S % tm == 0 and tm % SCAN_CHUNK == 0
    wy = w_in[:, :W].astype(BF16)
    wx = w_in[:, W:].astype(BF16)
    const = lambda shape: pl.BlockSpec(shape, lambda b, t: (0,) * len(shape))
    return pl.pallas_call(
        functools.partial(_lru_kernel, tm=tm, nblk=nblk),
        grid=(B, S // tm),
        in_specs=[
            pl.BlockSpec((None, tm, D), lambda b, t: (b, t, 0)),
            const((1, D)), const((D, W)), const((D, W)), const((LRU_CONV, W)), const((1, W)),
            const((2, nblk, LANES, LANES)), const((2, W)), const((1, W)), const((W, D)),
        ],
        out_specs=pl.BlockSpec((None, tm, D), lambda b, t: (b, t, 0)),
        out_shape=jax.ShapeDtypeStruct((B, S, D), F32),
        scratch_shapes=[
            pltpu.VMEM((SUBLANES, W), F32),
            pltpu.VMEM((nblk, tm, LANES), F32),
            pltpu.VMEM((nblk, tm, LANES), F32),
            pltpu.VMEM((nblk, SUBLANES, LANES), F32),
            pltpu.VMEM((tm, W), BF16),
            pltpu.VMEM((2, 2, tm, PROJ_W), F32),
        ],
        compiler_params=_params(("arbitrary", "arbitrary")),
        name="lru_layer",
    )(h, g.reshape(1, D), wy, wx, conv_w, conv_b.reshape(1, W), gate_w.astype(BF16), gate_b,
      a_param.reshape(1, W), w_out.astype(BF16))


def _ffn_kernel(*refs, tm, final_norm, has_mix):
    if has_mix:
        mix_ref, wmix_ref, *refs = refs
    (x_ref, g_ref, wa_ref, wb_ref, cw_ref, cb_ref, wo_ref, fg_ref, o_ref,
     xn_scr, acc_scr, a_scr, halo_scr) = refs
    t = pl.program_id(1)
    f = pl.program_id(2)
    nf = pl.num_programs(2)

    @pl.when(f == 0)
    def _():
        x = x_ref[...]
        if has_mix:
            x = x + _dot(mix_ref[...], wmix_ref[...])
        xn_scr[...] = _rmsnorm(x, g_ref[...]).astype(BF16)
        acc_scr[...] = x

    @pl.when(t == 0)
    def _():
        halo_scr[f] = jnp.zeros(halo_scr.shape[1:], F32)

    xn = xn_scr[...]
    a = _dot(xn, wa_ref[...])
    b = _dot(xn, wb_ref[...])
    a_scr[0:SUBLANES, :] = halo_scr[f]
    a_scr[SUBLANES:SUBLANES + tm, :] = a
    cw = cw_ref[...]
    ac = (cw[2:3, :] * a + cw[1:2, :] * a_scr[7:7 + tm, :] + cw[0:1, :] * a_scr[6:6 + tm, :]
          + cb_ref[...])
    halo_scr[f] = a_scr[tm:tm + SUBLANES, :]
    hmid = (_gelu(ac) * b).astype(BF16)
    acc_scr[...] += _dot(hmid, wo_ref[...])

    @pl.when(f == nf - 1)
    def _():
        out = acc_scr[...]
        if final_norm:
            out = _rmsnorm(out, fg_ref[...])
        o_ref[...] = out


def _ffn_layer(h, g, w_in, conv_w, conv_b, w_out, final_g, *, final_norm, mix=None, wmix=None,
               tm=1024, tf=512):
    B, S, D = h.shape
    F = w_out.shape[0]
    assert S % tm == 0 and F % tf == 0
    nf = F // tf
    wa = w_in[:, :F].astype(BF16)
    wb = w_in[:, F:].astype(BF16)
    has_mix = mix is not None
    mix_specs, mix_args = [], []
    if has_mix:
        K = mix.shape[2]
        mix_specs = [pl.BlockSpec((None, tm, K), lambda b, t, f: (b, t, 0)),
                     pl.BlockSpec((K, D), lambda b, t, f: (0, 0))]
        mix_args = [mix, wmix]
    return pl.pallas_call(
        functools.partial(_ffn_kernel, tm=tm, final_norm=final_norm, has_mix=has_mix),
        grid=(B, S // tm, nf),
        in_specs=mix_specs + [
            pl.BlockSpec((None, tm, D), lambda b, t, f: (b, t, 0)),
            pl.BlockSpec((1, D), lambda b, t, f: (0, 0)),
            pl.BlockSpec((D, tf), lambda b, t, f: (0, f)),
            pl.BlockSpec((D, tf), lambda b, t, f: (0, f)),
            pl.BlockSpec((FFN_CONV, tf), lambda b, t, f: (0, f)),
            pl.BlockSpec((1, tf), lambda b, t, f: (0, f)),
            pl.BlockSpec((tf, D), lambda b, t, f: (f, 0)),
            pl.BlockSpec((1, D), lambda b, t, f: (0, 0)),
        ],
        out_specs=pl.BlockSpec((None, tm, D), lambda b, t, f: (b, t, 0)),
        out_shape=jax.ShapeDtypeStruct((B, S, D), F32),
        scratch_shapes=[
            pltpu.VMEM((tm, D), BF16),
            pltpu.VMEM((tm, D), F32),
            pltpu.VMEM((tm + SUBLANES, tf), F32),
            pltpu.VMEM((nf, SUBLANES, tf), F32),
        ],
        compiler_params=_params(("arbitrary", "arbitrary", "arbitrary")),
        name="ffn_layer",
    )(*mix_args, h, g.reshape(1, D), wa, wb, conv_w, conv_b.reshape(1, F), w_out.astype(BF16),
      final_g.reshape(1, D))


def _nsa_proj_kernel(x_ref, g_ref, wq_ref, wkc_ref, wkv_ref, wg_ref, q_ref, t2_ref, kv_ref, gt_ref,
                     kc_scr):
    xn = _rmsnorm(x_ref[...], g_ref[...]).astype(BF16)
    q_ref[...] = (_dot(xn, wq_ref[...]) * (HEAD_DIM ** -0.5 * LOG2E)).astype(BF16)
    kc = _dot(xn, wkc_ref[...])
    nrow = t2_ref.shape[1]
    for j in range(kc_scr.shape[0]):
        kc_scr[j] = kc[:, j * LANES:(j + 1) * LANES]
        for i in range(CMP_STRIDE):
            t2_ref[j, :, i * LANES:(i + 1) * LANES] = (
                kc_scr.at[j][pl.ds(i, nrow, stride=CMP_STRIDE), :].astype(BF16))
    kv_ref[...] = _dot(xn, wkv_ref[...]).astype(BF16)
    gt_ref[...] = _sigmoid(_dot(xn, wg_ref[...]))


def _nsa_proj(h, g, wq, wkc, wkv, wg, *, tm=1024):
    B, S, D = h.shape
    nq, nkc, nkv, ng = wq.shape[1], wkc.shape[1], wkv.shape[1], wg.shape[1]
    nslab = nkc // LANES
    const = lambda shape: pl.BlockSpec(shape, lambda b, t: (0,) * len(shape))
    row = lambda n: pl.BlockSpec((None, tm, n), lambda b, t: (b, t, 0))
    return pl.pallas_call(
        _nsa_proj_kernel,
        grid=(B, S // tm),
        in_specs=[row(D), const((1, D)), const((D, nq)), const((D, nkc)), const((D, nkv)),
                  const((D, ng))],
        out_specs=[row(nq),
                   pl.BlockSpec((None, nslab, tm // CMP_STRIDE, CMP_STRIDE * LANES),
                                lambda b, t: (b, 0, t, 0)),
                   row(nkv), row(ng)],
        out_shape=[jax.ShapeDtypeStruct((B, S, nq), BF16),
                   jax.ShapeDtypeStruct((B, nslab, S // CMP_STRIDE, CMP_STRIDE * LANES), BF16),
                   jax.ShapeDtypeStruct((B, S, nkv), BF16),
                   jax.ShapeDtypeStruct((B, S, ng), F32)],
        scratch_shapes=[pltpu.VMEM((nslab, tm, LANES), F32)],
        compiler_params=_params(("arbitrary", "arbitrary")),
        name="nsa_proj",
    )(h, g.reshape(1, D), wq, wkc, wkv, wg)


def _cmp_kernel(t2_ref, pos_ref, wt_ref, wb_ref, b1_ref, w2_ref, o_ref):
    nb, nrow, K = t2_ref.shape
    t2 = t2_ref[...].reshape(nb * nrow, K)
    w_top = wt_ref[...]
    w_bot = wb_ref[...]
    pos = pos_ref[...].astype(BF16)
    bias = _dot(pos[:, :K], w_top) + _dot(pos[:, K:], w_bot) + b1_ref[...]
    hid = _dot(t2, w_top) + pltpu.roll(_dot(t2, w_bot), nb * nrow - 1, 0) + bias[0:1, :]
    out = _dot(_gelu(hid).astype(BF16), w2_ref[...])
    o_ref[...] = out.reshape(nb, nrow, out.shape[1]).astype(o_ref.dtype)


def _nsa_compress(t2, pos, w_top, w_bot, b1, w2, *, nb=4):
    B, nslab, nrow, K = t2.shape
    npair = nslab // 2
    H2 = w_top.shape[2]
    nb = math.gcd(nb, B)
    per_kv = lambda shape: pl.BlockSpec((None,) + shape, lambda s, p, b: (s,) + (0,) * len(shape))
    return pl.pallas_call(
        _cmp_kernel,
        grid=(2, npair, B // nb),
        in_specs=[
            pl.BlockSpec((nb, None, nrow, K), lambda s, p, b: (b, s * npair + p, 0, 0)),
            per_kv((SUBLANES, 2 * K)), per_kv((K, H2)), per_kv((K, H2)), per_kv((1, H2)),
            per_kv((H2, LANES)),
        ],
        out_specs=pl.BlockSpec((nb, None, nrow, LANES), lambda s, p, b: (b, s, 0, p)),
        out_shape=jax.ShapeDtypeStruct((B, 2, nrow, npair * LANES), BF16),
        compiler_params=_params(("arbitrary", "arbitrary", "arbitrary")),
        name="nsa_compress",
    )(t2, pos, w_top, w_bot, b1, w2)


NSLOT = 2 * GROUP
LOG2E = 1.4426950408889634
NPIECE = 4
SEL_LANE0 = 0
POS_LANE0 = 32
CMP_LANE0 = 40
QK_AHEAD = 3
SCORE_BUFS = 4
PROB_BUFS = 4
PV_BEHIND = 2
ONES_ROWS = 16


def _bf16_pieces(x, n):
    out, r = [], np.float64(x)
    for _ in range(n):
        p = np.float64(np.asarray(r, np.float32).astype(ml_dtypes.bfloat16).astype(np.float32))
        out.append(p)
        r = r - p
    return out


def _feature_base(gi):
    return HEAD_DIM if gi == 0 else 0


def _attn_tables(S, ncmp, tq):
    npair = N_KV // 2
    qf = np.zeros((npair, NSLOT, HEAD_DIM, tq), np.float32)
    for pair in range(npair):
        for slot in range(NSLOT):
            h = NSLOT * pair + slot
            for i, p in enumerate(_bf16_pieces(2.0 ** (-(h + 1) / 2.0) * LOG2E, NPIECE)):
                qf[pair, slot, POS_LANE0 + 2 * i] = SEL_BLOCK * p
                qf[pair, slot, POS_LANE0 + 2 * i + 1] = p
                qf[pair, slot, CMP_LANE0 + i] = CMP_STRIDE * p
    pos = np.arange(S)
    fs = np.zeros((2, S, LANES), np.float32)
    fw = np.zeros((2, S, LANES), np.float32)
    fc = np.zeros((2, ncmp, LANES), np.float32)
    for gi in range(2):
        f0 = _feature_base(gi)
        fs[gi, pos, f0 + SEL_LANE0 + pos // SEL_BLOCK] = 1.0
        for i in range(NPIECE):
            for f in (fs, fw):
                f[gi, :, f0 + POS_LANE0 + 2 * i] = pos // SEL_BLOCK
                f[gi, :, f0 + POS_LANE0 + 2 * i + 1] = pos % SEL_BLOCK
            fc[gi, :, f0 + CMP_LANE0 + i] = np.arange(ncmp)
    return (jnp.asarray(qf, BF16), jnp.asarray(fs, BF16), jnp.asarray(fw, BF16),
            jnp.asarray(fc, BF16))


def _attn_kernel(q_ref, ks_ref, vs_ref, kw_ref, vw_ref, kc_ref, vc_ref, gt_ref, ovt_ref, qf_ref,
                 fs_ref, fw_ref, fc_ref, o_ref, qaug_scr, kaug_scr, vt_scr, bias_scr, m_scr,
                 acc_scr, oc_scr, psum_scr, s_scr, p_scr, next_scr, *, tq, seq):
    tk = tq
    qt = pl.program_id(2)
    q0 = qt * tq
    ncmp = kc_ref.shape[0]
    nsel = seq // SEL_BLOCK
    ntile = seq // tk

    @pl.when(qt == 0)
    def _():
        own0 = lax.broadcasted_iota(jnp.int32, (tk, LANES), 1) < HEAD_DIM
        for br, k_ref, f_ref, v_ref in ((0, ks_ref, fs_ref, vs_ref), (1, kw_ref, fw_ref, vw_ref)):
            for j in range(ntile):
                rows = slice(j * tk, (j + 1) * tk)
                k2 = k_ref[rows, :]
                kaug_scr[br, 0, j] = jnp.where(own0, k2, f_ref[0, rows, :])
                kaug_scr[br, 1, j] = jnp.where(own0, f_ref[1, rows, :], k2)
                vt = v_ref[rows, :].astype(F32).T.astype(BF16)
                ones = jnp.ones((ONES_ROWS, tk), BF16)
                for gi in range(2):
                    vt_scr[br, j, gi] = jnp.concatenate(
                        [vt[gi * HEAD_DIM:(gi + 1) * HEAD_DIM, :], ones], axis=0)
        d = (lax.broadcasted_iota(jnp.int32, (tk, tq), 1)
             - lax.broadcasted_iota(jnp.int32, (tk, tq), 0))
        bias_scr[0] = jnp.where(d < 0, 0.0, NEG)
        bias_scr[1] = jnp.zeros((tk, tq), F32)
        bias_scr[2] = jnp.where(d >= 0, 0.0, NEG)

    m_scr[...] = jnp.full(m_scr.shape, NEG, F32)
    acc_scr[...] = jnp.zeros(acc_scr.shape, F32)

    for r in range(GROUP):
        qct = q_ref[:, r * LANES:(r + 1) * LANES].astype(F32).T.astype(BF16)
        qaug_scr[r] = jnp.concatenate([qct[0:HEAD_DIM, :], qf_ref[r]], axis=0)
        qaug_scr[GROUP + r] = jnp.concatenate([qf_ref[GROUP + r], qct[HEAD_DIM:, :]], axis=0)

    own0c = lax.broadcasted_iota(jnp.int32, (ncmp, LANES), 1) < HEAD_DIM
    kc2 = kc_ref[...]
    kcaug = [jnp.where(own0c, kc2, fc_ref[0]), jnp.where(own0c, fc_ref[1], kc2)]
    vct = vc_ref[...].astype(F32).T.astype(BF16)
    cend = lax.broadcasted_iota(jnp.int32, (ncmp, tq), 0) * CMP_STRIDE + (CMP_BLOCK - 1)
    cmp_bias = jnp.where(cend <= (q0 + lax.broadcasted_iota(jnp.int32, (ncmp, tq), 1)), 0.0, NEG)
    nbuf = s_scr.shape[0]
    assert NSLOT % nbuf == 0

    def issue_cmp_scores(slot):
        s_scr[slot % nbuf, 0:ncmp, :] = _dot(kcaug[slot // GROUP], qaug_scr[slot])

    def issue_scores(br, j, slot):
        s_scr[slot % nbuf] = _dot(kaug_scr[br, slot // GROUP, j], qaug_scr[slot])

    def cmp_values(slot):
        gi = slot // GROUP
        oc_scr[slot] = _dot(vct[gi * HEAD_DIM:(gi + 1) * HEAD_DIM, :], p_scr[slot % PROB_BUFS, 0:ncmp, :])

    win_lo = jnp.maximum(qt - 2, 0)
    for slot in range(QK_AHEAD):
        issue_cmp_scores(slot)
    psum = [None, None]
    for slot in range(NSLOT):
        gi = slot // GROUP
        sm = s_scr[slot % nbuf, 0:ncmp, :] + cmp_bias
        mx = jnp.max(sm, axis=0, keepdims=True)
        e = jnp.exp2(sm - mx)
        l = jnp.sum(e, axis=0, keepdims=True)
        p = e * jnp.where(mx > 0.5 * NEG, 1.0 / l, 0.0)
        psum[gi] = p if psum[gi] is None else psum[gi] + p
        p_scr[slot % PROB_BUFS, 0:ncmp, :] = p.astype(BF16)
        if slot + QK_AHEAD < NSLOT:
            issue_cmp_scores(slot + QK_AHEAD)
        else:
            issue_scores(1, win_lo, slot + QK_AHEAD - NSLOT)
        if slot >= 1:
            cmp_values(slot - 1)
    cmp_values(NSLOT - 1)
    for gi in range(2):
        psum_scr[gi] = psum[gi]

    def normalized(br, slot):
        return (acc_scr[br, slot, 0:HEAD_DIM, :]
                / acc_scr[br, slot, HEAD_DIM:HEAD_DIM + 1, :])

    def tile(br, j, ahead, bias, side_work=None):
        def weighted_values(slot, alpha):
            acc_scr[br, slot] = alpha * acc_scr[br, slot] + _dot(
                vt_scr[br, j, slot // GROUP], p_scr[slot % PROB_BUFS])

        alphas = []
        for slot in range(NSLOT):
            s = s_scr[slot % nbuf]
            if bias is not None:
                s = s + bias
            m_prev = m_scr[br, slot]
            m_new = jnp.maximum(m_prev, jnp.max(s, axis=0, keepdims=True))
            alphas.append(jnp.exp2(m_prev - m_new))
            p_scr[slot % PROB_BUFS] = jnp.exp2(s - m_new).astype(BF16)
            m_scr[br, slot] = m_new
            if slot + QK_AHEAD < NSLOT:
                issue_scores(br, j, slot + QK_AHEAD)
            elif ahead is not None:
                issue_scores(ahead[0], ahead[1], slot + QK_AHEAD - NSLOT)
            if slot >= PV_BEHIND:
                weighted_values(slot - PV_BEHIND, alphas[slot - PV_BEHIND])
            if side_work is not None:
                side_work(slot)
        for slot in range(NSLOT - PV_BEHIND, NSLOT):
            weighted_values(slot, alphas[slot])

    def sweep(br, j_lo, bias_fn, last_ahead, last_side_work):
        def body(j, carry):
            tile(br, j, (br, j + 1), bias_fn(j))
            return carry

        lax.fori_loop(j_lo, qt, body, 0)
        tile(br, qt, last_ahead, bias_scr[2], last_side_work)

    jidx = lax.broadcasted_iota(jnp.int32, (nsel, tq), 0)
    ntop = min(N_SELECT, nsel)
    topk = {}

    def topk_start():
        ovt = ovt_ref[...]
        cur = (q0 + lax.broadcasted_iota(jnp.int32, (nsel, tq), 1)) // SEL_BLOCK
        forced = (jidx == 0) | (jidx == cur) | (jidx == cur - 1)
        future = jidx > cur
        for gi in range(2):
            ps = psum_scr[gi]
            p1 = ps.astype(BF16)
            r1 = ps - p1.astype(F32)
            p2 = r1.astype(BF16)
            p3 = (r1 - p2.astype(F32)).astype(BF16)
            imp = _dot(ovt, p1) + _dot(ovt, p2) + _dot(ovt, p3)
            topk[gi] = (jnp.where(forced, FORCED_BONUS, jnp.where(future, -1.0, imp)),
                        jnp.full((nsel, tq), NEG, F32))

    def topk_round():
        for gi in range(2):
            score, selneg = topk[gi]
            mx = jnp.max(score, axis=0, keepdims=True)
            cand = jnp.where(score == mx, jidx, nsel)
            first = jnp.min(cand, axis=0, keepdims=True)
            hit = jidx == first
            topk[gi] = (jnp.where(hit, -3e38, score), jnp.where(hit, 0.0, selneg))

    def topk_finish():
        for gi in range(2):
            f0 = _feature_base(gi) + SEL_LANE0
            selneg = topk[gi][1].astype(BF16)
            for r in range(GROUP):
                qaug_scr[gi * GROUP + r, f0:f0 + nsel, :] = selneg
        chosen = jnp.max(jnp.maximum(topk[0][1], topk[1][1]), axis=1, keepdims=True)
        blk_per_tile = tk // SEL_BLOCK
        assert blk_per_tile < 16 and ntile <= 8
        tile_of_blk = lax.broadcasted_iota(jnp.int32, (nsel, 1), 0) // blk_per_tile
        digits = jnp.sum(jnp.where(chosen == 0.0, jnp.left_shift(1, 4 * tile_of_blk), 0))
        nxt = qt
        count = jnp.int32(0)
        for j in reversed(range(ntile)):
            next_scr[j] = nxt
            hit = (jnp.right_shift(digits, 4 * j) & 15) != 0
            if j == 0:
                hit = True
            take = hit & (j < qt)
            nxt = jnp.where(take, j, nxt)
            count = count + take.astype(jnp.int32)
        topk['count'] = count

    def topk_side_work(slot):
        last = NSLOT - QK_AHEAD - 1
        if slot == 0:
            topk_start()
        for i in range(ntop):
            if i * (last + 1) // ntop == slot:
                topk_round()
        if slot == last:
            topk_finish()

    sweep(1, win_lo, lambda j: bias_scr[j - qt + 2], (0, 0), topk_side_work)

    def chosen_tile(_, j):
        j_next = next_scr[j]
        tile(0, j, (0, j_next), None)
        return j_next

    lax.fori_loop(0, topk['count'], chosen_tile, jnp.int32(0))
    tile(0, qt, None, bias_scr[2])

    gtt = gt_ref[...].T
    for r in range(GROUP):
        comb = []
        for gi in range(2):
            slot = gi * GROUP + r
            comb.append(gtt[slot:slot + 1, :] * oc_scr[slot]
                        + gtt[NSLOT + slot:NSLOT + slot + 1, :] * normalized(0, slot)
                        + gtt[2 * NSLOT + slot:2 * NSLOT + slot + 1, :] * normalized(1, slot))
        col = jnp.concatenate(comb, axis=0).T
        o_ref[:, r * LANES:(r + 1) * LANES] = col.astype(o_ref.dtype)


def _nsa_attention(q, kv, kvc, gates, *, tq=256):
    B, S, _ = q.shape
    npair = N_KV // 2
    ncmp = kvc.shape[2]
    nsel = S // SEL_BLOCK
    assert S % tq == 0 and WINDOW == 2 * tq and nsel <= SEL_LANE0 + POS_LANE0
    cstart = np.arange(ncmp) * CMP_STRIDE
    selj = np.arange(nsel)
    ovt = ((cstart[None, :] < (selj[:, None] + 1) * SEL_BLOCK)
           & (cstart[None, :] + CMP_BLOCK > selj[:, None] * SEL_BLOCK))
    ovt = jnp.asarray(ovt, BF16)
    qf, fs, fw, fc = _attn_tables(S, ncmp, tq)

    def kvspec(cb):
        return pl.BlockSpec((None, S, LANES), lambda b, p, t: (b, 0, cb + p))

    def cspec(s):
        return pl.BlockSpec((None, None, ncmp, LANES), lambda b, p, t: (b, s, 0, p))

    const = lambda shape: pl.BlockSpec(shape, lambda b, p, t: (0,) * len(shape))
    qw = GROUP * LANES
    ntile = S // tq
    return pl.pallas_call(
        functools.partial(_attn_kernel, tq=tq, seq=S),
        grid=(B, npair, S // tq),
        in_specs=[
            pl.BlockSpec((None, tq, qw), lambda b, p, t: (b, t, p)),
            kvspec(0), kvspec(2), kvspec(4), kvspec(6),
            cspec(0), cspec(1),
            pl.BlockSpec((None, tq, LANES), lambda b, p, t: (b, t, p)),
            const((nsel, ncmp)),
            pl.BlockSpec((None, NSLOT, HEAD_DIM, tq), lambda b, p, t: (p, 0, 0, 0)),
            const((2, S, LANES)), const((2, S, LANES)), const((2, ncmp, LANES)),
        ],
        out_specs=pl.BlockSpec((None, tq, qw), lambda b, p, t: (b, t, p)),
        out_shape=jax.ShapeDtypeStruct((B, S, npair * qw), BF16),
        scratch_shapes=[
            pltpu.VMEM((NSLOT, LANES, tq), BF16),
            pltpu.VMEM((2, 2, ntile, tq, LANES), BF16),
            pltpu.VMEM((2, ntile, 2, HEAD_DIM + ONES_ROWS, tq), BF16),
            pltpu.VMEM((3, tq, tq), F32),
            pltpu.VMEM((2, NSLOT, 1, tq), F32),
            pltpu.VMEM((2, NSLOT, HEAD_DIM + ONES_ROWS, tq), F32),
            pltpu.VMEM((NSLOT, HEAD_DIM, tq), F32),
            pltpu.VMEM((2, ncmp, tq), F32),
            pltpu.VMEM((SCORE_BUFS, tq, tq), F32),
            pltpu.VMEM((PROB_BUFS, tq, tq), BF16),
            pltpu.SMEM((ntile,), jnp.int32),
        ],
        compiler_params=_params(("arbitrary", "arbitrary", "arbitrary")),
        name="nsa_attention",
    )(q, kv, kv, kv, kv, kvc, kvc, gates, ovt, qf, fs, fw, fc)


def _head_pair_perm():
    perm = np.zeros(N_HEADS * HEAD_DIM, np.int32)
    d = np.arange(HEAD_DIM)
    for pair in range(N_KV // 2):
        for r in range(GROUP):
            for gi in range(2):
                h = (2 * pair + gi) * GROUP + r
                n0 = pair * GROUP * LANES + r * LANES + gi * HEAD_DIM
                perm[n0 + d] = h * HEAD_DIM + d
    return perm


def _nsa_layer(h, g, w_in, cmp_pos, cmp_w1, cmp_b1, cmp_w2, w_out):
    B, S, D = h.shape
    qcols = N_HEADS * HEAD_DIM
    kvcols = 3 * 2 * N_KV * HEAD_DIM
    ngate = 3 * N_HEADS
    perm = _head_pair_perm()
    ccols = 2 * N_KV * HEAD_DIM
    wq = w_in[:, :qcols][:, perm].astype(BF16)
    wkc = w_in[:, qcols:qcols + ccols].astype(BF16)
    wkv = w_in[:, qcols + ccols:qcols + kvcols].astype(BF16)
    wg_src = w_in[:, qcols + kvcols:]
    wg = jnp.zeros((D, (N_KV // 2) * LANES), F32)
    for pair in range(N_KV // 2):
        for br in range(3):
            src = br * N_HEADS + pair * NSLOT
            dst = pair * LANES + br * NSLOT
            wg = wg.at[:, dst:dst + NSLOT].set(wg_src[:, src:src + NSLOT])
    wg = wg.astype(BF16)

    q, t2, kv, gates = _nsa_proj(h, g, wq, wkc, wkv, wg)

    hid = cmp_w1.shape[2]
    eye = jnp.eye(2, dtype=BF16)
    w1r = cmp_w1.astype(BF16).reshape(2, 2, CMP_STRIDE, HEAD_DIM, hid)
    w1p = jnp.einsum('stidh,gk->stigdkh', w1r, eye).reshape(2, 2, CMP_STRIDE * LANES, 2 * hid)
    posr = cmp_pos.reshape(2, 2, CMP_STRIDE, 1, HEAD_DIM)
    posp = jnp.broadcast_to(posr, (2, 2, CMP_STRIDE, 2, HEAD_DIM)).reshape(2, 1, -1)
    posp = jnp.broadcast_to(posp, (2, SUBLANES, posp.shape[2]))
    b1p = jnp.tile(cmp_b1, (1, 2)).reshape(2, 1, 2 * hid)
    w2p = jnp.einsum('shd,gk->sghkd', cmp_w2.astype(BF16), eye).reshape(2, 2 * hid, LANES)
    kvc = _nsa_compress(t2, posp, w1p[:, 0], w1p[:, 1], b1p, w2p)

    return _nsa_attention(q, kv, kvc, gates), w_out[perm, :].astype(BF16)


def kernel(x, lru_norm_g, lru_w_in, lru_conv_w, lru_conv_b, lru_gate_w, lru_gate_b, lru_a_param,
           lru_w_out, nsa_norm_g, nsa_w_in, nsa_cmp_pos, nsa_cmp_w1, nsa_cmp_b1, nsa_cmp_w2,
           nsa_w_out, ffn_norm_g, ffn_w_in, ffn_conv_w, ffn_conv_b, ffn_w_out, final_norm_g):
    h = _lru_layer(x, lru_norm_g[0], lru_w_in[0], lru_conv_w[0], lru_conv_b[0], lru_gate_w[0],
                   lru_gate_b[0], lru_a_param[0], lru_w_out[0])
    h = _ffn_layer(h, ffn_norm_g[0], ffn_w_in[0], ffn_conv_w[0], ffn_conv_b[0], ffn_w_out[0],
                   final_norm_g, final_norm=False)
    o, wo = _nsa_layer(h, nsa_norm_g[0], nsa_w_in[0], nsa_cmp_pos[0], nsa_cmp_w1[0],
                       nsa_cmp_b1[0], nsa_cmp_w2[0], nsa_w_out[0])
    return _ffn_layer(h, ffn_norm_g[1], ffn_w_in[1], ffn_conv_w[1], ffn_conv_b[1], ffn_w_out[1],
                      final_norm_g, final_norm=True, mix=o, wmix=wo)
```

```python
import functools
import math

import ml_dtypes
import numpy as np
import jax
import jax.numpy as jnp
from jax import lax
from jax.experimental import pallas as pl
from jax.experimental.pallas import tpu as pltpu

BF16 = jnp.bfloat16
F32 = jnp.float32

EPS = 1e-6
LRU_BLOCK_W = 128
LRU_CONV = 4
LRU_C = 8.0
N_HEADS = 16
HEAD_DIM = 64
N_KV = 4
GROUP = N_HEADS // N_KV
CMP_BLOCK = 32
CMP_STRIDE = 16
SEL_BLOCK = 64
N_SELECT = 8
WINDOW = 512
FORCED_BONUS = 1e4
NEG = -1e30
FFN_CONV = 3

LANES = 128
SUBLANES = 8
VMEM_LIMIT = 56 * 1024 * 1024


def _gelu(x):
    c = math.sqrt(2.0 / math.pi)
    inner = x * (c + (c * 0.044715) * (x * x))
    return (0.5 * x) * (1.0 + jnp.tanh(inner))


def _sigmoid(x):
    return 0.5 * jnp.tanh(0.5 * x) + 0.5


def _rmsnorm(x, g):
    return x * lax.rsqrt(jnp.mean(x * x, axis=-1, keepdims=True) + EPS) * g


def _dot(a, b):
    return jnp.dot(a, b, preferred_element_type=F32)


def _params(sem):
    return pltpu.CompilerParams(dimension_semantics=sem, vmem_limit_bytes=VMEM_LIMIT)


SCAN_CHUNK = SUBLANES * SUBLANES
PROJ_W = 2 * LANES


def _lru_kernel(x_ref, g_ref, wy_ref, wx_ref, cw_ref, cb_ref, gw_ref, gb_ref, ap_ref, wo_ref,
                o_ref, tail_scr, a_scr, b_scr, hc_scr, u_scr, yx_scr, *, tm, nblk):
    t = pl.program_id(1)

    @pl.when(t == 0)
    def _():
        tail_scr[...] = jnp.zeros(tail_scr.shape, F32)
        hc_scr[...] = jnp.zeros(hc_scr.shape, F32)

    x = x_ref[...]
    xn = _rmsnorm(x, g_ref[...]).astype(BF16)
    cw = cw_ref[...]
    cb = cb_ref[...]

    z = -ap_ref[...]
    c8 = -LRU_C * (jnp.maximum(z, 0.0) + jnp.log1p(jnp.exp(-jnp.abs(z))))
    gb = gb_ref[...]

    blk_per_chunk = PROJ_W // LANES
    nchunk = nblk // blk_per_chunk

    def project(i, which):
        cols = slice(i * PROJ_W, (i + 1) * PROJ_W)
        w_ref = wy_ref if which == 0 else wx_ref
        yx_scr[i % 2, which] = _dot(xn, w_ref[:, cols])

    sub_w = lax.broadcasted_iota(jnp.int32, (SUBLANES, PROJ_W), 0)

    def conv(i):
        cols = slice(i * PROJ_W, (i + 1) * PROJ_W)
        xb = yx_scr[i % 2, 1]
        tail = tail_scr[:, cols]
        xc = cw[LRU_CONV - 1:LRU_CONV, cols] * xb + cb[:, cols]
        for d in range(1, LRU_CONV):
            rolled = pltpu.roll(xb, d, 0)
            top = jnp.where(sub_w < d, pltpu.roll(tail, d, 0), rolled[0:SUBLANES, :])
            shifted = jnp.concatenate([top, rolled[SUBLANES:, :]], axis=0)
            xc = xc + cw[LRU_CONV - 1 - d:LRU_CONV - d, cols] * shifted
        tail_scr[:, cols] = xb[tm - SUBLANES:tm, :]
        return xc

    sub = lax.broadcasted_iota(jnp.int32, (SUBLANES, LANES), 0)
    out_split = (nchunk - 1) * PROJ_W
    out_half = out_split // 2

    def out_rows_a():
        o_ref[...] = x + _dot(u_scr[:, 0:out_half], wo_ref[0:out_half, :])

    def out_rows_b():
        o_ref[...] += _dot(u_scr[:, out_half:out_split], wo_ref[out_half:out_split, :])

    queued = [functools.partial(project, i, which)
              for i in range(1, nchunk) for which in (0, 1)] + [out_rows_a, out_rows_b]
    assert len(queued) == nblk
    project(0, 0)
    project(0, 1)
    for n in range(nblk):
        i, k = divmod(n, blk_per_chunk)
        if k == 0:
            xc = conv(i)
        lo, hi = n * LANES, (n + 1) * LANES
        xcn = xc[:, k * LANES:(k + 1) * LANES]
        xcb = xcn.astype(BF16)
        zr = _dot(xcb, gw_ref[0, n])
        zi = _dot(xcb, gw_ref[1, n])
        if n < len(queued):
            queued[n]()
        r = _sigmoid(zr + gb[0:1, lo:hi])
        ig = _sigmoid(zi + gb[1:2, lo:hi])
        log_a = c8[:, lo:hi] * r
        a = jnp.exp(log_a)
        w = -jnp.tanh(log_a) * (a * a + 1.0)
        mult = jnp.where(w > 0.0, w * lax.rsqrt(w), 0.0)
        a_scr[n] = a
        b_scr[n] = mult * ig * xcn

        carry = hc_scr[n]
        for c in range(tm // SCAN_CHUNK):
            base = c * SCAN_CHUNK
            acum, hloc = [], []
            for j in range(SUBLANES):
                aj = a_scr.at[n][pl.ds(base + j, SUBLANES, stride=SUBLANES), :]
                bj = b_scr.at[n][pl.ds(base + j, SUBLANES, stride=SUBLANES), :]
                if j == 0:
                    acum.append(aj)
                    hloc.append(bj)
                else:
                    hloc.append(aj * hloc[-1] + bj)
                    acum.append(aj * acum[-1])
            p, e = acum[-1], hloc[-1]
            for d in (1, 2, 4):
                keep = sub >= d
                psh = pltpu.roll(p, d, 0)
                esh = pltpu.roll(e, d, 0)
                e = jnp.where(keep, p * esh + e, e)
                p = jnp.where(keep, p * psh, p)
            hend = e + p * carry
            cin = jnp.where(sub == 0, carry, pltpu.roll(hend, 1, 0))
            for j in range(SUBLANES):
                b_scr.at[n][pl.ds(base + j, SUBLANES, stride=SUBLANES), :] = (
                    hloc[j] + acum[j] * cin)
            carry = jnp.broadcast_to(hend[SUBLANES - 1:SUBLANES, :], (SUBLANES, LANES))
        hc_scr[n] = carry
        y = _gelu(yx_scr[i % 2, 0, :, k * LANES:(k + 1) * LANES])
        u_scr[:, lo:hi] = (b_scr[n] * y).astype(BF16)

    o_ref[...] += _dot(u_scr[:, out_split:], wo_ref[out_split:, :])


def _lru_layer(h, g, w_in, conv_w, conv_b, gate_w, gate_b, a_param, w_out, *, tm=512):
    B, S, D = h.shape
    W = w_out.shape[0]
    nblk = W // LANES
    assert gate_w.shape == (2, nblk, LANES, LANES) and S % tm == 0 and tm % SCAN_CHUNK == 0
    wy = w_in[:, :W].astype(BF16)
    wx = w_in[:, W:].astype(BF16)
    const = lambda shape: pl.BlockSpec(shape, lambda b, t: (0,) * len(shape))
    return pl.pallas_call(
        functools.partial(_lru_kernel, tm=tm, nblk=nblk),
        grid=(B, S // tm),
        in_specs=[
            pl.BlockSpec((None, tm, D), lambda b, t: (b, t, 0)),
            const((1, D)), const((D, W)), const((D, W)), const((LRU_CONV, W)), const((1, W)),
            const((2, nblk, LANES, LANES)), const((2, W)), const((1, W)), const((W, D)),
        ],
        out_specs=pl.BlockSpec((None, tm, D), lambda b, t: (b, t, 0)),
        out_shape=jax.ShapeDtypeStruct((B, S, D), F32),
        scratch_shapes=[
            pltpu.VMEM((SUBLANES, W), F32),
            pltpu.VMEM((nblk, tm, LANES), F32),
            pltpu.VMEM((nblk, tm, LANES), F32),
            pltpu.VMEM((nblk, SUBLANES, LANES), F32),
            pltpu.VMEM((tm, W), BF16),
            pltpu.VMEM((2, 2, tm, PROJ_W), F32),
        ],
        compiler_params=_params(("arbitrary", "arbitrary")),
        name="lru_layer",
    )(h, g.reshape(1, D), wy, wx, conv_w, conv_b.reshape(1, W), gate_w.astype(BF16), gate_b,
      a_param.reshape(1, W), w_out.astype(BF16))


def _ffn_kernel(*refs, tm, final_norm, has_mix):
    if has_mix:
        mix_ref, wmix_ref, *refs = refs
    (x_ref, g_ref, wa_ref, wb_ref, cw_ref, cb_ref, wo_ref, fg_ref, o_ref,
     xn_scr, acc_scr, a_scr, halo_scr) = refs
    t = pl.program_id(1)
    f = pl.program_id(2)
    nf = pl.num_programs(2)

    @pl.when(f == 0)
    def _():
        x = x_ref[...]
        if has_mix:
            x = x + _dot(mix_ref[...], wmix_ref[...])
        xn_scr[...] = _rmsnorm(x, g_ref[...]).astype(BF16)
        acc_scr[...] = x

    @pl.when(t == 0)
    def _():
        halo_scr[f] = jnp.zeros(halo_scr.shape[1:], F32)

    xn = xn_scr[...]
    a = _dot(xn, wa_ref[...])
    b = _dot(xn, wb_ref[...])
    a_scr[0:SUBLANES, :] = halo_scr[f]
    a_scr[SUBLANES:SUBLANES + tm, :] = a
    cw = cw_ref[...]
    ac = (cw[2:3, :] * a + cw[1:2, :] * a_scr[7:7 + tm, :] + cw[0:1, :] * a_scr[6:6 + tm, :]
          + cb_ref[...])
    halo_scr[f] = a_scr[tm:tm + SUBLANES, :]
    hmid = (_gelu(ac) * b).astype(BF16)
    acc_scr[...] += _dot(hmid, wo_ref[...])

    @pl.when(f == nf - 1)
    def _():
        out = acc_scr[...]
        if final_norm:
            out = _rmsnorm(out, fg_ref[...])
        o_ref[...] = out


def _ffn_layer(h, g, w_in, conv_w, conv_b, w_out, final_g, *, final_norm, mix=None, wmix=None,
               tm=1024, tf=512):
    B, S, D = h.shape
    F = w_out.shape[0]
    assert S % tm == 0 and F % tf == 0
    nf = F // tf
    wa = w_in[:, :F].astype(BF16)
    wb = w_in[:, F:].astype(BF16)
    has_mix = mix is not None
    mix_specs, mix_args = [], []
    if has_mix:
        K = mix.shape[2]
        mix_specs = [pl.BlockSpec((None, tm, K), lambda b, t, f: (b, t, 0)),
                     pl.BlockSpec((K, D), lambda b, t, f: (0, 0))]
        mix_args = [mix, wmix]
    return pl.pallas_call(
        functools.partial(_ffn_kernel, tm=tm, final_norm=final_norm, has_mix=has_mix),
        grid=(B, S // tm, nf),
        in_specs=mix_specs + [
            pl.BlockSpec((None, tm, D), lambda b, t, f: (b, t, 0)),
            pl.BlockSpec((1, D), lambda b, t, f: (0, 0)),
            pl.BlockSpec((D, tf), lambda b, t, f: (0, f)),
            pl.BlockSpec((D, tf), lambda b, t, f: (0, f)),
            pl.BlockSpec((FFN_CONV, tf), lambda b, t, f: (0, f)),
            pl.BlockSpec((1, tf), lambda b, t, f: (0, f)),
            pl.BlockSpec((tf, D), lambda b, t, f: (f, 0)),
            pl.BlockSpec((1, D), lambda b, t, f: (0, 0)),
        ],
        out_specs=pl.BlockSpec((None, tm, D), lambda b, t, f: (b, t, 0)),
        out_shape=jax.ShapeDtypeStruct((B, S, D), F32),
        scratch_shapes=[
            pltpu.VMEM((tm, D), BF16),
            pltpu.VMEM((tm, D), F32),
            pltpu.VMEM((tm + SUBLANES, tf), F32),
            pltpu.VMEM((nf, SUBLANES, tf), F32),
        ],
        compiler_params=_params(("arbitrary", "arbitrary", "arbitrary")),
        name="ffn_layer",
    )(*mix_args, h, g.reshape(1, D), wa, wb, conv_w, conv_b.reshape(1, F), w_out.astype(BF16),
      final_g.reshape(1, D))


def _nsa_proj_kernel(x_ref, g_ref, wq_ref, wkc_ref, wkv_ref, wg_ref, q_ref, t2_ref, kv_ref, gt_ref,
                     kc_scr):
    xn = _rmsnorm(x_ref[...], g_ref[...]).astype(BF16)
    q_ref[...] = (_dot(xn, wq_ref[...]) * (HEAD_DIM ** -0.5 * LOG2E)).astype(BF16)
    kc = _dot(xn, wkc_ref[...])
    nrow = t2_ref.shape[1]
    for j in range(kc_scr.shape[0]):
        kc_scr[j] = kc[:, j * LANES:(j + 1) * LANES]
        for i in range(CMP_STRIDE):
            t2_ref[j, :, i * LANES:(i + 1) * LANES] = (
                kc_scr.at[j][pl.ds(i, nrow, stride=CMP_STRIDE), :].astype(BF16))
    kv_ref[...] = _dot(xn, wkv_ref[...]).astype(BF16)
    gt_ref[...] = _sigmoid(_dot(xn, wg_ref[...]))


def _nsa_proj(h, g, wq, wkc, wkv, wg, *, tm=1024):
    B, S, D = h.shape
    nq, nkc, nkv, ng = wq.shape[1], wkc.shape[1], wkv.shape[1], wg.shape[1]
    nslab = nkc // LANES
    const = lambda shape: pl.BlockSpec(shape, lambda b, t: (0,) * len(shape))
    row = lambda n: pl.BlockSpec((None, tm, n), lambda b, t: (b, t, 0))
    return pl.pallas_call(
        _nsa_proj_kernel,
        grid=(B, S // tm),
        in_specs=[row(D), const((1, D)), const((D, nq)), const((D, nkc)), const((D, nkv)),
                  const((D, ng))],
        out_specs=[row(nq),
                   pl.BlockSpec((None, nslab, tm // CMP_STRIDE, CMP_STRIDE * LANES),
                                lambda b, t: (b, 0, t, 0)),
                   row(nkv), row(ng)],
        out_shape=[jax.ShapeDtypeStruct((B, S, nq), BF16),
                   jax.ShapeDtypeStruct((B, nslab, S // CMP_STRIDE, CMP_STRIDE * LANES), BF16),
                   jax.ShapeDtypeStruct((B, S, nkv), BF16),
                   jax.ShapeDtypeStruct((B, S, ng), F32)],
        scratch_shapes=[pltpu.VMEM((nslab, tm, LANES), F32)],
        compiler_params=_params(("arbitrary", "arbitrary")),
        name="nsa_proj",
    )(h, g.reshape(1, D), wq, wkc, wkv, wg)


def _cmp_kernel(t2_ref, pos_ref, wt_ref, wb_ref, b1_ref, w2_ref, o_ref):
    nb, nrow, K = t2_ref.shape
    t2 = t2_ref[...].reshape(nb * nrow, K)
    w_top = wt_ref[...]
    w_bot = wb_ref[...]
    pos = pos_ref[...].astype(BF16)
    bias = _dot(pos[:, :K], w_top) + _dot(pos[:, K:], w_bot) + b1_ref[...]
    hid = _dot(t2, w_top) + pltpu.roll(_dot(t2, w_bot), nb * nrow - 1, 0) + bias[0:1, :]
    out = _dot(_gelu(hid).astype(BF16), w2_ref[...])
    o_ref[...] = out.reshape(nb, nrow, out.shape[1]).astype(o_ref.dtype)


def _nsa_compress(t2, pos, w_top, w_bot, b1, w2, *, nb=4):
    B, nslab, nrow, K = t2.shape
    npair = nslab // 2
    H2 = w_top.shape[2]
    nb = math.gcd(nb, B)
    per_kv = lambda shape: pl.BlockSpec((None,) + shape, lambda s, p, b: (s,) + (0,) * len(shape))
    return pl.pallas_call(
        _cmp_kernel,
        grid=(2, npair, B // nb),
        in_specs=[
            pl.BlockSpec((nb, None, nrow, K), lambda s, p, b: (b, s * npair + p, 0, 0)),
            per_kv((SUBLANES, 2 * K)), per_kv((K, H2)), per_kv((K, H2)), per_kv((1, H2)),
            per_kv((H2, LANES)),
        ],
        out_specs=pl.BlockSpec((nb, None, nrow, LANES), lambda s, p, b: (b, s, 0, p)),
        out_shape=jax.ShapeDtypeStruct((B, 2, nrow, npair * LANES), BF16),
        compiler_params=_params(("arbitrary", "arbitrary", "arbitrary")),
        name="nsa_compress",
    )(t2, pos, w_top, w_bot, b1, w2)


NSLOT = 2 * GROUP
LOG2E = 1.4426950408889634
NPIECE = 4
SEL_LANE0 = 0
POS_LANE0 = 32
CMP_LANE0 = 40
QK_AHEAD = 3
SCORE_BUFS = 4
PROB_BUFS = 4
PV_BEHIND = 2
ONES_ROWS = 16


def _bf16_pieces(x, n):
    out, r = [], np.float64(x)
    for _ in range(n):
        p = np.float64(np.asarray(r, np.float32).astype(ml_dtypes.bfloat16).astype(np.float32))
        out.append(p)
        r = r - p
    return out


def _feature_base(gi):
    return HEAD_DIM if gi == 0 else 0


def _attn_tables(S, ncmp, tq):
    npair = N_KV // 2
    qf = np.zeros((npair, NSLOT, HEAD_DIM, tq), np.float32)
    for pair in range(npair):
        for slot in range(NSLOT):
            h = NSLOT * pair + slot
            for i, p in enumerate(_bf16_pieces(2.0 ** (-(h + 1) / 2.0) * LOG2E, NPIECE)):
                qf[pair, slot, POS_LANE0 + 2 * i] = SEL_BLOCK * p
                qf[pair, slot, POS_LANE0 + 2 * i + 1] = p
                qf[pair, slot, CMP_LANE0 + i] = CMP_STRIDE * p
    pos = np.arange(S)
    fs = np.zeros((2, S, LANES), np.float32)
    fw = np.zeros((2, S, LANES), np.float32)
    fc = np.zeros((2, ncmp, LANES), np.float32)
    for gi in range(2):
        f0 = _feature_base(gi)
        fs[gi, pos, f0 + SEL_LANE0 + pos // SEL_BLOCK] = 1.0
        for i in range(NPIECE):
            for f in (fs, fw):
                f[gi, :, f0 + POS_LANE0 + 2 * i] = pos // SEL_BLOCK
                f[gi, :, f0 + POS_LANE0 + 2 * i + 1] = pos % SEL_BLOCK
            fc[gi, :, f0 + CMP_LANE0 + i] = np.arange(ncmp)
    return (jnp.asarray(qf, BF16), jnp.asarray(fs, BF16), jnp.asarray(fw, BF16),
            jnp.asarray(fc, BF16))


def _attn_kernel(q_ref, ks_ref, vs_ref, kw_ref, vw_ref, kc_ref, vc_ref, gt_ref, ovt_ref, qf_ref,
                 fs_ref, fw_ref, fc_ref, o_ref, qaug_scr, kaug_scr, vt_scr, bias_scr, m_scr,
                 acc_scr, oc_scr, psum_scr, s_scr, p_scr, next_scr, *, tq, seq):
    tk = tq
    qt = pl.program_id(2)
    q0 = qt * tq
    ncmp = kc_ref.shape[0]
    nsel = seq // SEL_BLOCK
    ntile = seq // tk

    @pl.when(qt == 0)
    def _():
        own0 = lax.broadcasted_iota(jnp.int32, (tk, LANES), 1) < HEAD_DIM
        for br, k_ref, f_ref, v_ref in ((0, ks_ref, fs_ref, vs_ref), (1, kw_ref, fw_ref, vw_ref)):
            for j in range(ntile):
                rows = slice(j * tk, (j + 1) * tk)
                k2 = k_ref[rows, :]
                kaug_scr[br, 0, j] = jnp.where(own0, k2, f_ref[0, rows, :])
                kaug_scr[br, 1, j] = jnp.where(own0, f_ref[1, rows, :], k2)
                vt = v_ref[rows, :].T
                ones = jnp.ones((ONES_ROWS, tk), BF16)
                for gi in range(2):
                    vt_scr[br, j, gi] = jnp.concatenate(
                        [vt[gi * HEAD_DIM:(gi + 1) * HEAD_DIM, :], ones], axis=0)
        d = (lax.broadcasted_iota(jnp.int32, (tk, tq), 1)
             - lax.broadcasted_iota(jnp.int32, (tk, tq), 0))
        bias_scr[0] = jnp.where(d < 0, 0.0, NEG)
        bias_scr[1] = jnp.zeros((tk, tq), F32)
        bias_scr[2] = jnp.where(d >= 0, 0.0, NEG)

    m_scr[...] = jnp.full(m_scr.shape, NEG, F32)
    acc_scr[...] = jnp.zeros(acc_scr.shape, F32)

    for r in range(GROUP):
        qct = q_ref[:, r * LANES:(r + 1) * LANES].T
        qaug_scr[r] = jnp.concatenate([qct[0:HEAD_DIM, :], qf_ref[r]], axis=0)
        qaug_scr[GROUP + r] = jnp.concatenate([qf_ref[GROUP + r], qct[HEAD_DIM:, :]], axis=0)

    own0c = lax.broadcasted_iota(jnp.int32, (ncmp, LANES), 1) < HEAD_DIM
    kc2 = kc_ref[...]
    kcaug = [jnp.where(own0c, kc2, fc_ref[0]), jnp.where(own0c, fc_ref[1], kc2)]
    vct = vc_ref[...].T
    cend = lax.broadcasted_iota(jnp.int32, (ncmp, tq), 0) * CMP_STRIDE + (CMP_BLOCK - 1)
    cmp_bias = jnp.where(cend <= (q0 + lax.broadcasted_iota(jnp.int32, (ncmp, tq), 1)), 0.0, NEG)
    nbuf = s_scr.shape[0]
    assert NSLOT % nbuf == 0

    def issue_cmp_scores(slot):
        s_scr[slot % nbuf, 0:ncmp, :] = _dot(kcaug[slot // GROUP], qaug_scr[slot])

    def issue_scores(br, j, slot):
        s_scr[slot % nbuf] = _dot(kaug_scr[br, slot // GROUP, j], qaug_scr[slot])

    def cmp_values(slot):
        gi = slot // GROUP
        oc_scr[slot] = _dot(vct[gi * HEAD_DIM:(gi + 1) * HEAD_DIM, :], p_scr[slot % PROB_BUFS, 0:ncmp, :])

    win_lo = jnp.maximum(qt - 2, 0)
    for slot in range(QK_AHEAD):
        issue_cmp_scores(slot)
    psum = [None, None]
    for slot in range(NSLOT):
        gi = slot // GROUP
        sm = s_scr[slot % nbuf, 0:ncmp, :] + cmp_bias
        mx = jnp.max(sm, axis=0, keepdims=True)
        e = jnp.exp2(sm - mx)
        l = jnp.sum(e, axis=0, keepdims=True)
        p = e * jnp.where(mx > 0.5 * NEG, 1.0 / l, 0.0)
        psum[gi] = p if psum[gi] is None else psum[gi] + p
        p_scr[slot % PROB_BUFS, 0:ncmp, :] = p.astype(BF16)
        if slot + QK_AHEAD < NSLOT:
            issue_cmp_scores(slot + QK_AHEAD)
        else:
            issue_scores(1, win_lo, slot + QK_AHEAD - NSLOT)
        if slot >= 1:
            cmp_values(slot - 1)
    cmp_values(NSLOT - 1)
    for gi in range(2):
        psum_scr[gi] = psum[gi]

    def normalized(br, slot):
        return (acc_scr[br, slot, 0:HEAD_DIM, :]
                * (1.0 / acc_scr[br, slot, HEAD_DIM:HEAD_DIM + 1, :]))

    def tile(br, j, ahead, bias, side_work=None):
        def weighted_values(slot, alpha):
            acc_scr[br, slot] = alpha * acc_scr[br, slot] + _dot(
                vt_scr[br, j, slot // GROUP], p_scr[slot % PROB_BUFS])

        alphas = []
        for slot in range(NSLOT):
            s = s_scr[slot % nbuf]
            if bias is not None:
                s = s + bias
            m_prev = m_scr[br, slot]
            m_new = jnp.maximum(m_prev, jnp.max(s, axis=0, keepdims=True))
            alphas.append(jnp.exp2(m_prev - m_new))
            p_scr[slot % PROB_BUFS] = jnp.exp2(s - m_new).astype(BF16)
            m_scr[br, slot] = m_new
            if slot + QK_AHEAD < NSLOT:
                issue_scores(br, j, slot + QK_AHEAD)
            elif ahead is not None:
                issue_scores(ahead[0], ahead[1], slot + QK_AHEAD - NSLOT)
            if slot >= PV_BEHIND:
                weighted_values(slot - PV_BEHIND, alphas[slot - PV_BEHIND])
            if side_work is not None:
                side_work(slot)
        for slot in range(NSLOT - PV_BEHIND, NSLOT):
            weighted_values(slot, alphas[slot])

    def sweep(br, j_lo, bias_fn, last_ahead, last_side_work):
        def body(j, carry):
            tile(br, j, (br, j + 1), bias_fn(j))
            return carry

        lax.fori_loop(j_lo, qt, body, 0)
        tile(br, qt, last_ahead, bias_scr[2], last_side_work)

    jidx = lax.broadcasted_iota(jnp.int32, (nsel, tq), 0)
    ntop = min(N_SELECT, nsel)
    topk = {}

    def topk_start():
        ovt = ovt_ref[...]
        cur = (q0 + lax.broadcasted_iota(jnp.int32, (nsel, tq), 1)) // SEL_BLOCK
        forced = (jidx == 0) | (jidx == cur) | (jidx == cur - 1)
        future = jidx > cur
        for gi in range(2):
            ps = psum_scr[gi]
            p1 = ps.astype(BF16)
            r1 = ps - p1.astype(F32)
            p2 = r1.astype(BF16)
            p3 = (r1 - p2.astype(F32)).astype(BF16)
            imp = _dot(ovt, p1) + _dot(ovt, p2) + _dot(ovt, p3)
            topk[gi] = (jnp.where(forced, FORCED_BONUS, jnp.where(future, -1.0, imp)),
                        jnp.full((nsel, tq), NEG, F32))

    def topk_round():
        for gi in range(2):
            score, selneg = topk[gi]
            mx = jnp.max(score, axis=0, keepdims=True)
            cand = jnp.where(score == mx, jidx, nsel)
            first = jnp.min(cand, axis=0, keepdims=True)
            hit = jidx == first
            topk[gi] = (jnp.where(hit, -3e38, score), jnp.where(hit, 0.0, selneg))

    def topk_finish():
        for gi in range(2):
            f0 = _feature_base(gi) + SEL_LANE0
            selneg = topk[gi][1].astype(BF16)
            for r in range(GROUP):
                qaug_scr[gi * GROUP + r, f0:f0 + nsel, :] = selneg
        chosen = jnp.max(jnp.maximum(topk[0][1], topk[1][1]), axis=1, keepdims=True)
        blk_per_tile = tk // SEL_BLOCK
        assert blk_per_tile < 16 and ntile <= 8
        tile_of_blk = lax.broadcasted_iota(jnp.int32, (nsel, 1), 0) // blk_per_tile
        digits = jnp.sum(jnp.where(chosen == 0.0, jnp.left_shift(1, 4 * tile_of_blk), 0))
        nxt = qt
        count = jnp.int32(0)
        for j in reversed(range(ntile)):
            next_scr[j] = nxt
            hit = (jnp.right_shift(digits, 4 * j) & 15) != 0
            if j == 0:
                hit = True
            take = hit & (j < qt)
            nxt = jnp.where(take, j, nxt)
            count = count + take.astype(jnp.int32)
        topk['count'] = count

    def topk_side_work(slot):
        last = NSLOT - QK_AHEAD - 1
        if slot == 0:
            topk_start()
        for i in range(ntop):
            if i * (last + 1) // ntop == slot:
                topk_round()
        if slot == last:
            topk_finish()

    sweep(1, win_lo, lambda j: bias_scr[j - qt + 2], (0, 0), topk_side_work)

    def chosen_tile(_, j):
        j_next = next_scr[j]
        tile(0, j, (0, j_next), None)
        return j_next

    lax.fori_loop(0, topk['count'], chosen_tile, jnp.int32(0))
    tile(0, qt, None, bias_scr[2])

    gtt = gt_ref[...].T
    for r in range(GROUP):
        comb = []
        for gi in range(2):
            slot = gi * GROUP + r
            comb.append(gtt[slot:slot + 1, :] * oc_scr[slot]
                        + gtt[NSLOT + slot:NSLOT + slot + 1, :] * normalized(0, slot)
                        + gtt[2 * NSLOT + slot:2 * NSLOT + slot + 1, :] * normalized(1, slot))
        col = jnp.concatenate(comb, axis=0).astype(o_ref.dtype)
        o_ref[:, r * LANES:(r + 1) * LANES] = col.T


def _nsa_attention(q, kv, kvc, gates, *, tq=256):
    B, S, _ = q.shape
    npair = N_KV // 2
    ncmp = kvc.shape[2]
    nsel = S // SEL_BLOCK
    assert S % tq == 0 and WINDOW == 2 * tq and nsel <= SEL_LANE0 + POS_LANE0
    cstart = np.arange(ncmp) * CMP_STRIDE
    selj = np.arange(nsel)
    ovt = ((cstart[None, :] < (selj[:, None] + 1) * SEL_BLOCK)
           & (cstart[None, :] + CMP_BLOCK > selj[:, None] * SEL_BLOCK))
    ovt = jnp.asarray(ovt, BF16)
    qf, fs, fw, fc = _attn_tables(S, ncmp, tq)

    def kvspec(cb):
        return pl.BlockSpec((None, S, LANES), lambda b, p, t: (b, 0, cb + p))

    def cspec(s):
        return pl.BlockSpec((None, None, ncmp, LANES), lambda b, p, t: (b, s, 0, p))

    const = lambda shape: pl.BlockSpec(shape, lambda b, p, t: (0,) * len(shape))
    qw = GROUP * LANES
    ntile = S // tq
    return pl.pallas_call(
        functools.partial(_attn_kernel, tq=tq, seq=S),
        grid=(B, npair, S // tq),
        in_specs=[
            pl.BlockSpec((None, tq, qw), lambda b, p, t: (b, t, p)),
            kvspec(0), kvspec(2), kvspec(4), kvspec(6),
            cspec(0), cspec(1),
            pl.BlockSpec((None, tq, LANES), lambda b, p, t: (b, t, p)),
            const((nsel, ncmp)),
            pl.BlockSpec((None, NSLOT, HEAD_DIM, tq), lambda b, p, t: (p, 0, 0, 0)),
            const((2, S, LANES)), const((2, S, LANES)), const((2, ncmp, LANES)),
        ],
        out_specs=pl.BlockSpec((None, tq, qw), lambda b, p, t: (b, t, p)),
        out_shape=jax.ShapeDtypeStruct((B, S, npair * qw), BF16),
        scratch_shapes=[
            pltpu.VMEM((NSLOT, LANES, tq), BF16),
            pltpu.VMEM((2, 2, ntile, tq, LANES), BF16),
            pltpu.VMEM((2, ntile, 2, HEAD_DIM + ONES_ROWS, tq), BF16),
            pltpu.VMEM((3, tq, tq), F32),
            pltpu.VMEM((2, NSLOT, 1, tq), F32),
            pltpu.VMEM((2, NSLOT, HEAD_DIM + ONES_ROWS, tq), F32),
            pltpu.VMEM((NSLOT, HEAD_DIM, tq), F32),
            pltpu.VMEM((2, ncmp, tq), F32),
            pltpu.VMEM((SCORE_BUFS, tq, tq), F32),
            pltpu.VMEM((PROB_BUFS, tq, tq), BF16),
            pltpu.SMEM((ntile,), jnp.int32),
        ],
        compiler_params=_params(("arbitrary", "arbitrary", "arbitrary")),
        name="nsa_attention",
    )(q, kv, kv, kv, kv, kvc, kvc, gates, ovt, qf, fs, fw, fc)


def _head_pair_perm():
    perm = np.zeros(N_HEADS * HEAD_DIM, np.int32)
    d = np.arange(HEAD_DIM)
    for pair in range(N_KV // 2):
        for r in range(GROUP):
            for gi in range(2):
                h = (2 * pair + gi) * GROUP + r
                n0 = pair * GROUP * LANES + r * LANES + gi * HEAD_DIM
                perm[n0 + d] = h * HEAD_DIM + d
    return perm


def _nsa_layer(h, g, w_in, cmp_pos, cmp_w1, cmp_b1, cmp_w2, w_out):
    B, S, D = h.shape
    qcols = N_HEADS * HEAD_DIM
    kvcols = 3 * 2 * N_KV * HEAD_DIM
    ngate = 3 * N_HEADS
    perm = _head_pair_perm()
    ccols = 2 * N_KV * HEAD_DIM
    wq = w_in[:, :qcols][:, perm].astype(BF16)
    wkc = w_in[:, qcols:qcols + ccols].astype(BF16)
    wkv = w_in[:, qcols + ccols:qcols + kvcols].astype(BF16)
    wg_src = w_in[:, qcols + kvcols:]
    wg = jnp.zeros((D, (N_KV // 2) * LANES), F32)
    for pair in range(N_KV // 2):
        for br in range(3):
            src = br * N_HEADS + pair * NSLOT
            dst = pair * LANES + br * NSLOT
            wg = wg.at[:, dst:dst + NSLOT].set(wg_src[:, src:src + NSLOT])
    wg = wg.astype(BF16)

    q, t2, kv, gates = _nsa_proj(h, g, wq, wkc, wkv, wg)

    hid = cmp_w1.shape[2]
    eye = jnp.eye(2, dtype=BF16)
    w1r = cmp_w1.astype(BF16).reshape(2, 2, CMP_STRIDE, HEAD_DIM, hid)
    w1p = jnp.einsum('stidh,gk->stigdkh', w1r, eye).reshape(2, 2, CMP_STRIDE * LANES, 2 * hid)
    posr = cmp_pos.reshape(2, 2, CMP_STRIDE, 1, HEAD_DIM)
    posp = jnp.broadcast_to(posr, (2, 2, CMP_STRIDE, 2, HEAD_DIM)).reshape(2, 1, -1)
    posp = jnp.broadcast_to(posp, (2, SUBLANES, posp.shape[2]))
    b1p = jnp.tile(cmp_b1, (1, 2)).reshape(2, 1, 2 * hid)
    w2p = jnp.einsum('shd,gk->sghkd', cmp_w2.astype(BF16), eye).reshape(2, 2 * hid, LANES)
    kvc = _nsa_compress(t2, posp, w1p[:, 0], w1p[:, 1], b1p, w2p)

    return _nsa_attention(q, kv, kvc, gates), w_out[perm, :].astype(BF16)


def kernel(x, lru_norm_g, lru_w_in, lru_conv_w, lru_conv_b, lru_gate_w, lru_gate_b, lru_a_param,
           lru_w_out, nsa_norm_g, nsa_w_in, nsa_cmp_pos, nsa_cmp_w1, nsa_cmp_b1, nsa_cmp_w2,
           nsa_w_out, ffn_norm_g, ffn_w_in, ffn_conv_w, ffn_conv_b, ffn_w_out, final_norm_g):
    h = _lru_layer(x, lru_norm_g[0], lru_w_in[0], lru_conv_w[0], lru_conv_b[0], lru_gate_w[0],
                   lru_gate_b[0], lru_a_param[0], lru_w_out[0])
    h = _ffn_layer(h, ffn_norm_g[0], ffn_w_in[0], ffn_conv_w[0], ffn_conv_b[0], ffn_w_out[0],
                   final_norm_g, final_norm=False)
    o, wo = _nsa_layer(h, nsa_norm_g[0], nsa_w_in[0], nsa_cmp_pos[0], nsa_cmp_w1[0],
                       nsa_cmp_b1[0], nsa_cmp_w2[0], nsa_w_out[0])
    return _ffn_layer(h, ffn_norm_g[1], ffn_w_in[1], ffn_conv_w[1], ffn_conv_b[1], ffn_w_out[1],
                      final_norm_g, final_norm=True, mix=o, wmix=wo)
```

```python
import functools
import math

import ml_dtypes
import numpy as np
import jax
import jax.numpy as jnp
from jax import lax
from jax.experimental import pallas as pl
from jax.experimental.pallas import tpu as pltpu

BF16 = jnp.bfloat16
F32 = jnp.float32

EPS = 1e-6
LRU_BLOCK_W = 128
LRU_CONV = 4
LRU_C = 8.0
N_HEADS = 16
HEAD_DIM = 64
N_KV = 4
GROUP = N_HEADS // N_KV
CMP_BLOCK = 32
CMP_STRIDE = 16
SEL_BLOCK = 64
N_SELECT = 8
WINDOW = 512
FORCED_BONUS = 1e4
NEG = -1e30
FFN_CONV = 3

LANES = 128
SUBLANES = 8
VMEM_LIMIT = 56 * 1024 * 1024


def _gelu(x):
    c = math.sqrt(2.0 / math.pi)
    inner = x * (c + (c * 0.044715) * (x * x))
    return (0.5 * x) * (1.0 + jnp.tanh(inner))


def _sigmoid(x):
    return 0.5 * jnp.tanh(0.5 * x) + 0.5


def _rmsnorm(x, g):
    return x * lax.rsqrt(jnp.mean(x * x, axis=-1, keepdims=True) + EPS) * g


def _dot(a, b):
    return jnp.dot(a, b, preferred_element_type=F32)


def _params(sem):
    return pltpu.CompilerParams(dimension_semantics=sem, vmem_limit_bytes=VMEM_LIMIT)


SCAN_CHUNK = SUBLANES * SUBLANES
PROJ_W = 2 * LANES


def _lru_kernel(x_ref, g_ref, wy_ref, wx_ref, cw_ref, cb_ref, gw_ref, gb_ref, ap_ref, wo_ref,
                o_ref, tail_scr, a_scr, b_scr, hc_scr, u_scr, yx_scr, *, tm, nblk):
    t = pl.program_id(1)

    @pl.when(t == 0)
    def _():
        tail_scr[...] = jnp.zeros(tail_scr.shape, F32)
        hc_scr[...] = jnp.zeros(hc_scr.shape, F32)

    x = x_ref[...]
    xn = _rmsnorm(x, g_ref[...]).astype(BF16)
    cw = cw_ref[...]
    cb = cb_ref[...]

    z = -ap_ref[...]
    c8 = -LRU_C * (jnp.maximum(z, 0.0) + jnp.log1p(jnp.exp(-jnp.abs(z))))
    gb = gb_ref[...]

    blk_per_chunk = PROJ_W // LANES
    nchunk = nblk // blk_per_chunk

    def project(i, which):
        cols = slice(i * PROJ_W, (i + 1) * PROJ_W)
        w_ref = wy_ref if which == 0 else wx_ref
        yx_scr[i % 2, which] = _dot(xn, w_ref[:, cols])

    sub_w = lax.broadcasted_iota(jnp.int32, (SUBLANES, PROJ_W), 0)

    def conv(i):
        cols = slice(i * PROJ_W, (i + 1) * PROJ_W)
        xb = yx_scr[i % 2, 1]
        tail = tail_scr[:, cols]
        xc = cw[LRU_CONV - 1:LRU_CONV, cols] * xb + cb[:, cols]
        for d in range(1, LRU_CONV):
            rolled = pltpu.roll(xb, d, 0)
            top = jnp.where(sub_w < d, pltpu.roll(tail, d, 0), rolled[0:SUBLANES, :])
            shifted = jnp.concatenate([top, rolled[SUBLANES:, :]], axis=0)
            xc = xc + cw[LRU_CONV - 1 - d:LRU_CONV - d, cols] * shifted
        tail_scr[:, cols] = xb[tm - SUBLANES:tm, :]
        return xc

    sub = lax.broadcasted_iota(jnp.int32, (SUBLANES, LANES), 0)
    out_split = (nchunk - 1) * PROJ_W
    out_half = out_split // 2

    def out_rows_a():
        o_ref[...] = x + _dot(u_scr[:, 0:out_half], wo_ref[0:out_half, :])

    def out_rows_b():
        o_ref[...] += _dot(u_scr[:, out_half:out_split], wo_ref[out_half:out_split, :])

    queued = [functools.partial(project, i, which)
              for i in range(1, nchunk) for which in (0, 1)] + [out_rows_a, out_rows_b]
    assert len(queued) == nblk
    project(0, 0)
    project(0, 1)
    for n in range(nblk):
        i, k = divmod(n, blk_per_chunk)
        if k == 0:
            xc = conv(i)
        lo, hi = n * LANES, (n + 1) * LANES
        xcn = xc[:, k * LANES:(k + 1) * LANES]
        xcb = xcn.astype(BF16)
        zr = _dot(xcb, gw_ref[0, n])
        zi = _dot(xcb, gw_ref[1, n])
        if n < len(queued):
            queued[n]()
        r = _sigmoid(zr + gb[0:1, lo:hi])
        ig = _sigmoid(zi + gb[1:2, lo:hi])
        log_a = c8[:, lo:hi] * r
        a = jnp.exp(log_a)
        w = -jnp.tanh(log_a) * (a * a + 1.0)
        mult = jnp.where(w > 0.0, w * lax.rsqrt(w), 0.0)
        a_scr[n] = a
        b_scr[n] = mult * ig * xcn

        carry = hc_scr[n]
        for c in range(tm // SCAN_CHUNK):
            base = c * SCAN_CHUNK
            acum, hloc = [], []
            for j in range(SUBLANES):
                aj = a_scr.at[n][pl.ds(base + j, SUBLANES, stride=SUBLANES), :]
                bj = b_scr.at[n][pl.ds(base + j, SUBLANES, stride=SUBLANES), :]
                if j == 0:
                    acum.append(aj)
                    hloc.append(bj)
                else:
                    hloc.append(aj * hloc[-1] + bj)
                    acum.append(aj * acum[-1])
            p, e = acum[-1], hloc[-1]
            for d in (1, 2, 4):
                keep = sub >= d
                psh = pltpu.roll(p, d, 0)
                esh = pltpu.roll(e, d, 0)
                e = jnp.where(keep, p * esh + e, e)
                p = jnp.where(keep, p * psh, p)
            hend = e + p * carry
            cin = jnp.where(sub == 0, carry, pltpu.roll(hend, 1, 0))
            for j in range(SUBLANES):
                b_scr.at[n][pl.ds(base + j, SUBLANES, stride=SUBLANES), :] = (
                    hloc[j] + acum[j] * cin)
            carry = jnp.broadcast_to(hend[SUBLANES - 1:SUBLANES, :], (SUBLANES, LANES))
        hc_scr[n] = carry
        y = _gelu(yx_scr[i % 2, 0, :, k * LANES:(k + 1) * LANES])
        u_scr[:, lo:hi] = (b_scr[n] * y).astype(BF16)

    o_ref[...] += _dot(u_scr[:, out_split:], wo_ref[out_split:, :])


def _lru_layer(h, g, w_in, conv_w, conv_b, gate_w, gate_b, a_param, w_out, *, tm=512):
    B, S, D = h.shape
    W = w_out.shape[0]
    nblk = W // LANES
    assert gate_w.shape == (2, nblk, LANES, LANES) and S % tm == 0 and tm % SCAN_CHUNK == 0
    wy = w_in[:, :W].astype(BF16)
    wx = w_in[:, W:].astype(BF16)
    const = lambda shape: pl.BlockSpec(shape, lambda b, t: (0,) * len(shape))
    return pl.pallas_call(
        functools.partial(_lru_kernel, tm=tm, nblk=nblk),
        grid=(B, S // tm),
        in_specs=[
            pl.BlockSpec((None, tm, D), lambda b, t: (b, t, 0)),
            const((1, D)), const((D, W)), const((D, W)), const((LRU_CONV, W)), const((1, W)),
            const((2, nblk, LANES, LANES)), const((2, W)), const((1, W)), const((W, D)),
        ],
        out_specs=pl.BlockSpec((None, tm, D), lambda b, t: (b, t, 0)),
        out_shape=jax.ShapeDtypeStruct((B, S, D), F32),
        scratch_shapes=[
            pltpu.VMEM((SUBLANES, W), F32),
            pltpu.VMEM((nblk, tm, LANES), F32),
            pltpu.VMEM((nblk, tm, LANES), F32),
            pltpu.VMEM((nblk, SUBLANES, LANES), F32),
            pltpu.VMEM((tm, W), BF16),
            pltpu.VMEM((2, 2, tm, PROJ_W), F32),
        ],
        compiler_params=_params(("arbitrary", "arbitrary")),
        name="lru_layer",
    )(h, g.reshape(1, D), wy, wx, conv_w, conv_b.reshape(1, W), gate_w.astype(BF16), gate_b,
      a_param.reshape(1, W), w_out.astype(BF16))


def _ffn_kernel(*refs, tm, final_norm, has_mix):
    if has_mix:
        mix_ref, wmix_ref, *refs = refs
    (x_ref, g_ref, wa_ref, wb_ref, cw_ref, cb_ref, wo_ref, fg_ref, o_ref,
     xn_scr, acc_scr, a_scr, halo_scr) = refs
    t = pl.program_id(1)
    f = pl.program_id(2)
    nf = pl.num_programs(2)

    @pl.when(f == 0)
    def _():
        x = x_ref[...]
        if has_mix:
            x = x + _dot(mix_ref[...], wmix_ref[...])
        xn_scr[...] = _rmsnorm(x, g_ref[...]).astype(BF16)
        acc_scr[...] = x

    @pl.when(t == 0)
    def _():
        halo_scr[f] = jnp.zeros(halo_scr.shape[1:], F32)

    xn = xn_scr[...]
    a = _dot(xn, wa_ref[...])
    b = _dot(xn, wb_ref[...])
    a_scr[0:SUBLANES, :] = halo_scr[f]
    a_scr[SUBLANES:SUBLANES + tm, :] = a
    cw = cw_ref[...]
    ac = (cw[2:3, :] * a + cw[1:2, :] * a_scr[7:7 + tm, :] + cw[0:1, :] * a_scr[6:6 + tm, :]
          + cb_ref[...])
    halo_scr[f] = a_scr[tm:tm + SUBLANES, :]
    hmid = (_gelu(ac) * b).astype(BF16)
    acc_scr[...] += _dot(hmid, wo_ref[...])

    @pl.when(f == nf - 1)
    def _():
        out = acc_scr[...]
        if final_norm:
            out = _rmsnorm(out, fg_ref[...])
        o_ref[...] = out


def _ffn_layer(h, g, w_in, conv_w, conv_b, w_out, final_g, *, final_norm, mix=None, wmix=None,
               tm=1024, tf=512):
    B, S, D = h.shape
    F = w_out.shape[0]
    assert S % tm == 0 and F % tf == 0
    nf = F // tf
    wa = w_in[:, :F].astype(BF16)
    wb = w_in[:, F:].astype(BF16)
    has_mix = mix is not None
    mix_specs, mix_args = [], []
    if has_mix:
        K = mix.shape[2]
        mix_specs = [pl.BlockSpec((None, tm, K), lambda b, t, f: (b, t, 0)),
                     pl.BlockSpec((K, D), lambda b, t, f: (0, 0))]
        mix_args = [mix, wmix]
    return pl.pallas_call(
        functools.partial(_ffn_kernel, tm=tm, final_norm=final_norm, has_mix=has_mix),
        grid=(B, S // tm, nf),
        in_specs=mix_specs + [
            pl.BlockSpec((None, tm, D), lambda b, t, f: (b, t, 0)),
            pl.BlockSpec((1, D), lambda b, t, f: (0, 0)),
            pl.BlockSpec((D, tf), lambda b, t, f: (0, f)),
            pl.BlockSpec((D, tf), lambda b, t, f: (0, f)),
            pl.BlockSpec((FFN_CONV, tf), lambda b, t, f: (0, f)),
            pl.BlockSpec((1, tf), lambda b, t, f: (0, f)),
            pl.BlockSpec((tf, D), lambda b, t, f: (f, 0)),
            pl.BlockSpec((1, D), lambda b, t, f: (0, 0)),
        ],
        out_specs=pl.BlockSpec((None, tm, D), lambda b, t, f: (b, t, 0)),
        out_shape=jax.ShapeDtypeStruct((B, S, D), F32),
        scratch_shapes=[
            pltpu.VMEM((tm, D), BF16),
            pltpu.VMEM((tm, D), F32),
            pltpu.VMEM((tm + SUBLANES, tf), F32),
            pltpu.VMEM((nf, SUBLANES, tf), F32),
        ],
        compiler_params=_params(("arbitrary", "arbitrary", "arbitrary")),
        name="ffn_layer",
    )(*mix_args, h, g.reshape(1, D), wa, wb, conv_w, conv_b.reshape(1, F), w_out.astype(BF16),
      final_g.reshape(1, D))


def _nsa_proj_kernel(x_ref, g_ref, wq_ref, wkc_ref, wkv_ref, wg_ref, q_ref, t2_ref, kv_ref, gt_ref,
                     kc_scr):
    xn = _rmsnorm(x_ref[...], g_ref[...]).astype(BF16)
    q_ref[...] = (_dot(xn, wq_ref[...]) * (HEAD_DIM ** -0.5 * LOG2E)).astype(BF16)
    kc = _dot(xn, wkc_ref[...])
    nrow = t2_ref.shape[1]
    for j in range(kc_scr.shape[0]):
        kc_scr[j] = kc[:, j * LANES:(j + 1) * LANES]
        for i in range(CMP_STRIDE):
            t2_ref[j, :, i * LANES:(i + 1) * LANES] = (
                kc_scr.at[j][pl.ds(i, nrow, stride=CMP_STRIDE), :].astype(BF16))
    kv_ref[...] = _dot(xn, wkv_ref[...]).astype(BF16)
    gt_ref[...] = _sigmoid(_dot(xn, wg_ref[...]))


def _nsa_proj(h, g, wq, wkc, wkv, wg, *, tm=1024):
    B, S, D = h.shape
    nq, nkc, nkv, ng = wq.shape[1], wkc.shape[1], wkv.shape[1], wg.shape[1]
    nslab = nkc // LANES
    const = lambda shape: pl.BlockSpec(shape, lambda b, t: (0,) * len(shape))
    row = lambda n: pl.BlockSpec((None, tm, n), lambda b, t: (b, t, 0))
    return pl.pallas_call(
        _nsa_proj_kernel,
        grid=(B, S // tm),
        in_specs=[row(D), const((1, D)), const((D, nq)), const((D, nkc)), const((D, nkv)),
                  const((D, ng))],
        out_specs=[row(nq),
                   pl.BlockSpec((None, nslab, tm // CMP_STRIDE, CMP_STRIDE * LANES),
                                lambda b, t: (b, 0, t, 0)),
                   row(nkv), row(ng)],
        out_shape=[jax.ShapeDtypeStruct((B, S, nq), BF16),
                   jax.ShapeDtypeStruct((B, nslab, S // CMP_STRIDE, CMP_STRIDE * LANES), BF16),
                   jax.ShapeDtypeStruct((B, S, nkv), BF16),
                   jax.ShapeDtypeStruct((B, S, ng), F32)],
        scratch_shapes=[pltpu.VMEM((nslab, tm, LANES), F32)],
        compiler_params=_params(("arbitrary", "arbitrary")),
        name="nsa_proj",
    )(h, g.reshape(1, D), wq, wkc, wkv, wg)


def _cmp_kernel(t2_ref, pos_ref, wt_ref, wb_ref, b1_ref, w2_ref, o_ref):
    nb, nrow, K = t2_ref.shape
    t2 = t2_ref[...].reshape(nb * nrow, K)
    w_top = wt_ref[...]
    w_bot = wb_ref[...]
    pos = pos_ref[...].astype(BF16)
    bias = _dot(pos[:, :K], w_top) + _dot(pos[:, K:], w_bot) + b1_ref[...]
    hid = _dot(t2, w_top) + pltpu.roll(_dot(t2, w_bot), nb * nrow - 1, 0) + bias[0:1, :]
    out = _dot(_gelu(hid).astype(BF16), w2_ref[...])
    o_ref[...] = out.reshape(nb, nrow, out.shape[1]).astype(o_ref.dtype)


def _nsa_compress(t2, pos, w_top, w_bot, b1, w2, *, nb=4):
    B, nslab, nrow, K = t2.shape
    npair = nslab // 2
    H2 = w_top.shape[2]
    nb = math.gcd(nb, B)
    per_kv = lambda shape: pl.BlockSpec((None,) + shape, lambda s, p, b: (s,) + (0,) * len(shape))
    return pl.pallas_call(
        _cmp_kernel,
        grid=(2, npair, B // nb),
        in_specs=[
            pl.BlockSpec((nb, None, nrow, K), lambda s, p, b: (b, s * npair + p, 0, 0)),
            per_kv((SUBLANES, 2 * K)), per_kv((K, H2)), per_kv((K, H2)), per_kv((1, H2)),
            per_kv((H2, LANES)),
        ],
        out_specs=pl.BlockSpec((nb, None, nrow, LANES), lambda s, p, b: (b, s, 0, p)),
        out_shape=jax.ShapeDtypeStruct((B, 2, nrow, npair * LANES), BF16),
        compiler_params=_params(("arbitrary", "arbitrary", "arbitrary")),
        name="nsa_compress",
    )(t2, pos, w_top, w_bot, b1, w2)


NSLOT = 2 * GROUP
LOG2E = 1.4426950408889634
NPIECE = 4
SEL_LANE0 = 0
POS_LANE0 = 32
CMP_LANE0 = 40
QK_AHEAD = 3
SCORE_BUFS = 4
PROB_BUFS = 4
PV_BEHIND = 2
ONES_ROWS = 16


def _bf16_pieces(x, n):
    out, r = [], np.float64(x)
    for _ in range(n):
        p = np.float64(np.asarray(r, np.float32).astype(ml_dtypes.bfloat16).astype(np.float32))
        out.append(p)
        r = r - p
    return out


def _feature_base(gi):
    return HEAD_DIM if gi == 0 else 0


def _attn_tables(S, ncmp, tq):
    npair = N_KV // 2
    qf = np.zeros((npair, NSLOT, HEAD_DIM, tq), np.float32)
    for pair in range(npair):
        for slot in range(NSLOT):
            h = NSLOT * pair + slot
            for i, p in enumerate(_bf16_pieces(2.0 ** (-(h + 1) / 2.0) * LOG2E, NPIECE)):
                qf[pair, slot, POS_LANE0 + 2 * i] = SEL_BLOCK * p
                qf[pair, slot, POS_LANE0 + 2 * i + 1] = p
                qf[pair, slot, CMP_LANE0 + i] = CMP_STRIDE * p
    pos = np.arange(S)
    fs = np.zeros((2, S, LANES), np.float32)
    fw = np.zeros((2, S, LANES), np.float32)
    fc = np.zeros((2, ncmp, LANES), np.float32)
    for gi in range(2):
        f0 = _feature_base(gi)
        fs[gi, pos, f0 + SEL_LANE0 + pos // SEL_BLOCK] = 1.0
        for i in range(NPIECE):
            for f in (fs, fw):
                f[gi, :, f0 + POS_LANE0 + 2 * i] = pos // SEL_BLOCK
                f[gi, :, f0 + POS_LANE0 + 2 * i + 1] = pos % SEL_BLOCK
            fc[gi, :, f0 + CMP_LANE0 + i] = np.arange(ncmp)
    return (jnp.asarray(qf, BF16), jnp.asarray(fs, BF16), jnp.asarray(fw, BF16),
            jnp.asarray(fc, BF16))


def _attn_kernel(q_ref, ks_ref, vs_ref, kw_ref, vw_ref, kc_ref, vc_ref, gt_ref, ovt_ref, qf_ref,
                 fs_ref, fw_ref, fc_ref, o_ref, qaug_scr, kaug_scr, vt_scr, bias_scr, m_scr,
                 acc_scr, oc_scr, psum_scr, s_scr, p_scr, next_scr, *, tq, seq):
    tk = tq
    qt = pl.program_id(2)
    q0 = qt * tq
    ncmp = kc_ref.shape[0]
    nsel = seq // SEL_BLOCK
    ntile = seq // tk

    @pl.when(qt == 0)
    def _():
        own0 = lax.broadcasted_iota(jnp.int32, (tk, LANES), 1) < HEAD_DIM
        for br, k_ref, f_ref, v_ref in ((0, ks_ref, fs_ref, vs_ref), (1, kw_ref, fw_ref, vw_ref)):
            for j in range(ntile):
                rows = slice(j * tk, (j + 1) * tk)
                k2 = k_ref[rows, :]
                kaug_scr[br, 0, j] = jnp.where(own0, k2, f_ref[0, rows, :])
                kaug_scr[br, 1, j] = jnp.where(own0, f_ref[1, rows, :], k2)
                vt = v_ref[rows, :].T
                ones = jnp.ones((ONES_ROWS, tk), BF16)
                for gi in range(2):
                    vt_scr[br, j, gi] = jnp.concatenate(
                        [vt[gi * HEAD_DIM:(gi + 1) * HEAD_DIM, :], ones], axis=0)
        d = (lax.broadcasted_iota(jnp.int32, (tk, tq), 1)
             - lax.broadcasted_iota(jnp.int32, (tk, tq), 0))
        bias_scr[0] = jnp.where(d < 0, 0.0, NEG)
        bias_scr[1] = jnp.where(d >= 0, 0.0, NEG)

    m_scr[...] = jnp.full(m_scr.shape, NEG, F32)
    acc_scr[...] = jnp.zeros(acc_scr.shape, F32)

    for r in range(GROUP):
        qct = q_ref[:, r * LANES:(r + 1) * LANES].T
        qaug_scr[r] = jnp.concatenate([qct[0:HEAD_DIM, :], qf_ref[r]], axis=0)
        qaug_scr[GROUP + r] = jnp.concatenate([qf_ref[GROUP + r], qct[HEAD_DIM:, :]], axis=0)

    own0c = lax.broadcasted_iota(jnp.int32, (ncmp, LANES), 1) < HEAD_DIM
    kc2 = kc_ref[...]
    kcaug = [jnp.where(own0c, kc2, fc_ref[0]), jnp.where(own0c, fc_ref[1], kc2)]
    vct = vc_ref[...].T
    cend = lax.broadcasted_iota(jnp.int32, (ncmp, tq), 0) * CMP_STRIDE + (CMP_BLOCK - 1)
    cmp_bias = jnp.where(cend <= (q0 + lax.broadcasted_iota(jnp.int32, (ncmp, tq), 1)), 0.0, NEG)
    nbuf = s_scr.shape[0]
    assert NSLOT % nbuf == 0

    def issue_cmp_scores(slot):
        s_scr[slot % nbuf, 0:ncmp, :] = _dot(kcaug[slot // GROUP], qaug_scr[slot])

    def issue_scores(br, j, slot):
        s_scr[slot % nbuf] = _dot(kaug_scr[br, slot // GROUP, j], qaug_scr[slot])

    def cmp_values(slot):
        gi = slot // GROUP
        oc_scr[slot] = _dot(vct[gi * HEAD_DIM:(gi + 1) * HEAD_DIM, :], p_scr[slot % PROB_BUFS, 0:ncmp, :])

    win_lo = jnp.maximum(qt - 2, 0)
    for slot in range(QK_AHEAD):
        issue_cmp_scores(slot)
    psum = [None, None]
    for slot in range(NSLOT):
        gi = slot // GROUP
        sm = s_scr[slot % nbuf, 0:ncmp, :] + cmp_bias
        mx = jnp.max(sm, axis=0, keepdims=True)
        e = jnp.exp2(sm - mx)
        l = jnp.sum(e, axis=0, keepdims=True)
        p = e * jnp.where(mx > 0.5 * NEG, 1.0 / l, 0.0)
        psum[gi] = p if psum[gi] is None else psum[gi] + p
        p_scr[slot % PROB_BUFS, 0:ncmp, :] = p.astype(BF16)
        if slot + QK_AHEAD < NSLOT:
            issue_cmp_scores(slot + QK_AHEAD)
        else:
            issue_scores(1, win_lo, slot + QK_AHEAD - NSLOT)
        if slot >= 1:
            cmp_values(slot - 1)
    cmp_values(NSLOT - 1)
    for gi in range(2):
        psum_scr[gi] = psum[gi]

    def normalized(br, slot):
        return (acc_scr[br, slot, 0:HEAD_DIM, :]
                * (1.0 / acc_scr[br, slot, HEAD_DIM:HEAD_DIM + 1, :]))

    def tile(br, j, ahead, bias, side_work=None):
        def weighted_values(slot, alpha):
            acc_scr[br, slot] = alpha * acc_scr[br, slot] + _dot(
                vt_scr[br, j, slot // GROUP], p_scr[slot % PROB_BUFS])

        alphas = []
        for slot in range(NSLOT):
            s = s_scr[slot % nbuf]
            if bias is not None:
                s = s + bias
            m_prev = m_scr[br, slot]
            m_new = jnp.maximum(m_prev, jnp.max(s, axis=0, keepdims=True))
            alphas.append(jnp.exp2(m_prev - m_new))
            p_scr[slot % PROB_BUFS] = jnp.exp2(s - m_new).astype(BF16)
            m_scr[br, slot] = m_new
            if slot + QK_AHEAD < NSLOT:
                issue_scores(br, j, slot + QK_AHEAD)
            elif ahead is not None:
                issue_scores(ahead[0], ahead[1], slot + QK_AHEAD - NSLOT)
            if slot >= PV_BEHIND:
                weighted_values(slot - PV_BEHIND, alphas[slot - PV_BEHIND])
            if side_work is not None:
                side_work(slot)
        for slot in range(NSLOT - PV_BEHIND, NSLOT):
            weighted_values(slot, alphas[slot])

    jidx = lax.broadcasted_iota(jnp.int32, (nsel, tq), 0)
    ntop = min(N_SELECT, nsel)
    topk = {}

    def topk_start():
        ovt = ovt_ref[...]
        cur = (q0 + lax.broadcasted_iota(jnp.int32, (nsel, tq), 1)) // SEL_BLOCK
        forced = (jidx == 0) | (jidx == cur) | (jidx == cur - 1)
        future = jidx > cur
        for gi in range(2):
            ps = psum_scr[gi]
            p1 = ps.astype(BF16)
            r1 = ps - p1.astype(F32)
            p2 = r1.astype(BF16)
            p3 = (r1 - p2.astype(F32)).astype(BF16)
            imp = _dot(ovt, p1) + _dot(ovt, p2) + _dot(ovt, p3)
            topk[gi] = (jnp.where(forced, FORCED_BONUS, jnp.where(future, -1.0, imp)),
                        jnp.full((nsel, tq), NEG, F32))

    def topk_round():
        for gi in range(2):
            score, selneg = topk[gi]
            mx = jnp.max(score, axis=0, keepdims=True)
            cand = jnp.where(score == mx, jidx, nsel)
            first = jnp.min(cand, axis=0, keepdims=True)
            hit = jidx == first
            topk[gi] = (jnp.where(hit, -3e38, score), jnp.where(hit, 0.0, selneg))

    def topk_finish():
        for gi in range(2):
            f0 = _feature_base(gi) + SEL_LANE0
            selneg = topk[gi][1].astype(BF16)
            for r in range(GROUP):
                qaug_scr[gi * GROUP + r, f0:f0 + nsel, :] = selneg
        chosen = jnp.max(jnp.maximum(topk[0][1], topk[1][1]), axis=1, keepdims=True)
        blk_per_tile = tk // SEL_BLOCK
        assert blk_per_tile < 16 and ntile <= 8
        tile_of_blk = lax.broadcasted_iota(jnp.int32, (nsel, 1), 0) // blk_per_tile
        digits = jnp.sum(jnp.where(chosen == 0.0, jnp.left_shift(1, 4 * tile_of_blk), 0))
        nxt = qt
        count = jnp.int32(0)
        for j in reversed(range(ntile)):
            next_scr[j] = nxt
            hit = (jnp.right_shift(digits, 4 * j) & 15) != 0
            if j == 0:
                hit = True
            take = hit & (j < qt)
            nxt = jnp.where(take, j, nxt)
            count = count + take.astype(jnp.int32)
        topk['count'] = count

    def topk_side_work(slot):
        last = NSLOT - QK_AHEAD - 1
        if slot == 0:
            topk_start()
        for i in range(ntop):
            if i * (last + 1) // ntop == slot:
                topk_round()
        if slot == last:
            topk_finish()

    @pl.when(qt >= 2)
    def _():
        tile(1, qt - 2, (1, qt - 1), bias_scr[0])

    @pl.when(qt >= 1)
    def _():
        tile(1, qt - 1, (1, qt), None)

    tile(1, qt, (0, 0), bias_scr[1], topk_side_work)

    def chosen_tile(_, j):
        j_next = next_scr[j]
        tile(0, j, (0, j_next), None)
        return j_next

    lax.fori_loop(0, topk['count'], chosen_tile, jnp.int32(0))
    tile(0, qt, None, bias_scr[1])

    gtt = gt_ref[...].T
    for r in range(GROUP):
        comb = []
        for gi in range(2):
            slot = gi * GROUP + r
            comb.append(gtt[slot:slot + 1, :] * oc_scr[slot]
                        + gtt[NSLOT + slot:NSLOT + slot + 1, :] * normalized(0, slot)
                        + gtt[2 * NSLOT + slot:2 * NSLOT + slot + 1, :] * normalized(1, slot))
        col = jnp.concatenate(comb, axis=0).astype(o_ref.dtype)
        o_ref[:, r * LANES:(r + 1) * LANES] = col.T


def _nsa_attention(q, kv, kvc, gates, *, tq=256):
    B, S, _ = q.shape
    npair = N_KV // 2
    ncmp = kvc.shape[2]
    nsel = S // SEL_BLOCK
    assert S % tq == 0 and WINDOW == 2 * tq and nsel <= SEL_LANE0 + POS_LANE0
    cstart = np.arange(ncmp) * CMP_STRIDE
    selj = np.arange(nsel)
    ovt = ((cstart[None, :] < (selj[:, None] + 1) * SEL_BLOCK)
           & (cstart[None, :] + CMP_BLOCK > selj[:, None] * SEL_BLOCK))
    ovt = jnp.asarray(ovt, BF16)
    qf, fs, fw, fc = _attn_tables(S, ncmp, tq)

    def kvspec(cb):
        return pl.BlockSpec((None, S, LANES), lambda b, p, t: (b, 0, cb + p))

    def cspec(s):
        return pl.BlockSpec((None, None, ncmp, LANES), lambda b, p, t: (b, s, 0, p))

    const = lambda shape: pl.BlockSpec(shape, lambda b, p, t: (0,) * len(shape))
    qw = GROUP * LANES
    ntile = S // tq
    return pl.pallas_call(
        functools.partial(_attn_kernel, tq=tq, seq=S),
        grid=(B, npair, S // tq),
        in_specs=[
            pl.BlockSpec((None, tq, qw), lambda b, p, t: (b, t, p)),
            kvspec(0), kvspec(2), kvspec(4), kvspec(6),
            cspec(0), cspec(1),
            pl.BlockSpec((None, tq, LANES), lambda b, p, t: (b, t, p)),
            const((nsel, ncmp)),
            pl.BlockSpec((None, NSLOT, HEAD_DIM, tq), lambda b, p, t: (p, 0, 0, 0)),
            const((2, S, LANES)), const((2, S, LANES)), const((2, ncmp, LANES)),
        ],
        out_specs=pl.BlockSpec((None, tq, qw), lambda b, p, t: (b, t, p)),
        out_shape=jax.ShapeDtypeStruct((B, S, npair * qw), BF16),
        scratch_shapes=[
            pltpu.VMEM((NSLOT, LANES, tq), BF16),
            pltpu.VMEM((2, 2, ntile, tq, LANES), BF16),
            pltpu.VMEM((2, ntile, 2, HEAD_DIM + ONES_ROWS, tq), BF16),
            pltpu.VMEM((2, tq, tq), F32),
            pltpu.VMEM((2, NSLOT, 1, tq), F32),
            pltpu.VMEM((2, NSLOT, HEAD_DIM + ONES_ROWS, tq), F32),
            pltpu.VMEM((NSLOT, HEAD_DIM, tq), F32),
            pltpu.VMEM((2, ncmp, tq), F32),
            pltpu.VMEM((SCORE_BUFS, tq, tq), F32),
            pltpu.VMEM((PROB_BUFS, tq, tq), BF16),
            pltpu.SMEM((ntile,), jnp.int32),
        ],
        compiler_params=_params(("arbitrary", "arbitrary", "arbitrary")),
        name="nsa_attention",
    )(q, kv, kv, kv, kv, kvc, kvc, gates, ovt, qf, fs, fw, fc)


def _head_pair_perm():
    perm = np.zeros(N_HEADS * HEAD_DIM, np.int32)
    d = np.arange(HEAD_DIM)
    for pair in range(N_KV // 2):
        for r in range(GROUP):
            for gi in range(2):
                h = (2 * pair + gi) * GROUP + r
                n0 = pair * GROUP * LANES + r * LANES + gi * HEAD_DIM
                perm[n0 + d] = h * HEAD_DIM + d
    return perm


def _nsa_layer(h, g, w_in, cmp_pos, cmp_w1, cmp_b1, cmp_w2, w_out):
    B, S, D = h.shape
    qcols = N_HEADS * HEAD_DIM
    kvcols = 3 * 2 * N_KV * HEAD_DIM
    ngate = 3 * N_HEADS
    perm = _head_pair_perm()
    ccols = 2 * N_KV * HEAD_DIM
    wq = w_in[:, :qcols][:, perm].astype(BF16)
    wkc = w_in[:, qcols:qcols + ccols].astype(BF16)
    wkv = w_in[:, qcols + ccols:qcols + kvcols].astype(BF16)
    wg_src = w_in[:, qcols + kvcols:]
    wg = jnp.zeros((D, (N_KV // 2) * LANES), F32)
    for pair in range(N_KV // 2):
        for br in range(3):
            src = br * N_HEADS + pair * NSLOT
            dst = pair * LANES + br * NSLOT
            wg = wg.at[:, dst:dst + NSLOT].set(wg_src[:, src:src + NSLOT])
    wg = wg.astype(BF16)

    q, t2, kv, gates = _nsa_proj(h, g, wq, wkc, wkv, wg)

    hid = cmp_w1.shape[2]
    eye = jnp.eye(2, dtype=BF16)
    w1r = cmp_w1.astype(BF16).reshape(2, 2, CMP_STRIDE, HEAD_DIM, hid)
    w1p = jnp.einsum('stidh,gk->stigdkh', w1r, eye).reshape(2, 2, CMP_STRIDE * LANES, 2 * hid)
    posr = cmp_pos.reshape(2, 2, CMP_STRIDE, 1, HEAD_DIM)
    posp = jnp.broadcast_to(posr, (2, 2, CMP_STRIDE, 2, HEAD_DIM)).reshape(2, 1, -1)
    posp = jnp.broadcast_to(posp, (2, SUBLANES, posp.shape[2]))
    b1p = jnp.tile(cmp_b1, (1, 2)).reshape(2, 1, 2 * hid)
    w2p = jnp.einsum('shd,gk->sghkd', cmp_w2.astype(BF16), eye).reshape(2, 2 * hid, LANES)
    kvc = _nsa_compress(t2, posp, w1p[:, 0], w1p[:, 1], b1p, w2p)

    return _nsa_attention(q, kv, kvc, gates), w_out[perm, :].astype(BF16)


def kernel(x, lru_norm_g, lru_w_in, lru_conv_w, lru_conv_b, lru_gate_w, lru_gate_b, lru_a_param,
           lru_w_out, nsa_norm_g, nsa_w_in, nsa_cmp_pos, nsa_cmp_w1, nsa_cmp_b1, nsa_cmp_w2,
           nsa_w_out, ffn_norm_g, ffn_w_in, ffn_conv_w, ffn_conv_b, ffn_w_out, final_norm_g):
    h = _lru_layer(x, lru_norm_g[0], lru_w_in[0], lru_conv_w[0], lru_conv_b[0], lru_gate_w[0],
                   lru_gate_b[0], lru_a_param[0], lru_w_out[0])
    h = _ffn_layer(h, ffn_norm_g[0], ffn_w_in[0], ffn_conv_w[0], ffn_conv_b[0], ffn_w_out[0],
                   final_norm_g, final_norm=False)
    o, wo = _nsa_layer(h, nsa_norm_g[0], nsa_w_in[0], nsa_cmp_pos[0], nsa_cmp_w1[0],
                       nsa_cmp_b1[0], nsa_cmp_w2[0], nsa_w_out[0])
    return _ffn_layer(h, ffn_norm_g[1], ffn_w_in[1], ffn_conv_w[1], ffn_conv_b[1], ffn_w_out[1],
                      final_norm_g, final_norm=True, mix=o, wmix=wo)
```

```python
import functools
import math

import ml_dtypes
import numpy as np
import jax
import jax.numpy as jnp
from jax import lax
from jax.experimental import pallas as pl
from jax.experimental.pallas import tpu as pltpu

BF16 = jnp.bfloat16
F32 = jnp.float32

EPS = 1e-6
LRU_BLOCK_W = 128
LRU_CONV = 4
LRU_C = 8.0
N_HEADS = 16
HEAD_DIM = 64
N_KV = 4
GROUP = N_HEADS // N_KV
CMP_BLOCK = 32
CMP_STRIDE = 16
SEL_BLOCK = 64
N_SELECT = 8
WINDOW = 512
FORCED_BONUS = 1e4
NEG = -1e30
FFN_CONV = 3

LANES = 128
SUBLANES = 8
VMEM_LIMIT = 56 * 1024 * 1024


def _gelu(x):
    c = math.sqrt(2.0 / math.pi)
    inner = x * (c + (c * 0.044715) * (x * x))
    return (0.5 * x) * (1.0 + jnp.tanh(inner))


def _sigmoid(x):
    return 0.5 * jnp.tanh(0.5 * x) + 0.5


def _rmsnorm(x, g):
    return x * lax.rsqrt(jnp.mean(x * x, axis=-1, keepdims=True) + EPS) * g


def _dot(a, b):
    return jnp.dot(a, b, preferred_element_type=F32)


def _params(sem):
    return pltpu.CompilerParams(dimension_semantics=sem, vmem_limit_bytes=VMEM_LIMIT)


SCAN_CHUNK = SUBLANES * SUBLANES
PROJ_W = 2 * LANES


def _lru_kernel(x_ref, g_ref, wy_ref, wx_ref, cw_ref, cb_ref, gw_ref, gb_ref, ap_ref, wo_ref,
                o_ref, tail_scr, a_scr, b_scr, hc_scr, u_scr, yx_scr, *, tm, nblk):
    t = pl.program_id(1)

    @pl.when(t == 0)
    def _():
        tail_scr[...] = jnp.zeros(tail_scr.shape, F32)
        hc_scr[...] = jnp.zeros(hc_scr.shape, F32)

    x = x_ref[...]
    xn = _rmsnorm(x, g_ref[...]).astype(BF16)
    cw = cw_ref[...]
    cb = cb_ref[...]

    z = -ap_ref[...]
    c8 = -LRU_C * (jnp.maximum(z, 0.0) + jnp.log1p(jnp.exp(-jnp.abs(z))))
    gb = gb_ref[...]

    blk_per_chunk = PROJ_W // LANES
    nchunk = nblk // blk_per_chunk

    def project(i, which):
        cols = slice(i * PROJ_W, (i + 1) * PROJ_W)
        w_ref = wy_ref if which == 0 else wx_ref
        yx_scr[i % 2, which] = _dot(xn, w_ref[:, cols])

    sub_w = lax.broadcasted_iota(jnp.int32, (SUBLANES, PROJ_W), 0)

    def conv(i):
        cols = slice(i * PROJ_W, (i + 1) * PROJ_W)
        xb = yx_scr[i % 2, 1]
        tail = tail_scr[:, cols]
        xc = cw[LRU_CONV - 1:LRU_CONV, cols] * xb + cb[:, cols]
        for d in range(1, LRU_CONV):
            rolled = pltpu.roll(xb, d, 0)
            top = jnp.where(sub_w < d, pltpu.roll(tail, d, 0), rolled[0:SUBLANES, :])
            shifted = jnp.concatenate([top, rolled[SUBLANES:, :]], axis=0)
            xc = xc + cw[LRU_CONV - 1 - d:LRU_CONV - d, cols] * shifted
        tail_scr[:, cols] = xb[tm - SUBLANES:tm, :]
        return xc

    sub = lax.broadcasted_iota(jnp.int32, (SUBLANES, LANES), 0)
    out_split = (nchunk - 1) * PROJ_W
    out_half = out_split // 2

    def out_rows_a():
        o_ref[...] = x + _dot(u_scr[:, 0:out_half], wo_ref[0:out_half, :])

    def out_rows_b():
        o_ref[...] += _dot(u_scr[:, out_half:out_split], wo_ref[out_half:out_split, :])

    queued = [functools.partial(project, i, which)
              for i in range(1, nchunk) for which in (0, 1)] + [out_rows_a, out_rows_b]
    assert len(queued) == nblk
    project(0, 0)
    project(0, 1)
    for n in range(nblk):
        i, k = divmod(n, blk_per_chunk)
        if k == 0:
            xc = conv(i)
        lo, hi = n * LANES, (n + 1) * LANES
        xcn = xc[:, k * LANES:(k + 1) * LANES]
        xcb = xcn.astype(BF16)
        zr = _dot(xcb, gw_ref[0, n])
        zi = _dot(xcb, gw_ref[1, n])
        if n < len(queued):
            queued[n]()
        r = _sigmoid(zr + gb[0:1, lo:hi])
        ig = _sigmoid(zi + gb[1:2, lo:hi])
        log_a = c8[:, lo:hi] * r
        a = jnp.exp(log_a)
        w = -jnp.tanh(log_a) * (a * a + 1.0)
        mult = jnp.where(w > 0.0, w * lax.rsqrt(w), 0.0)
        a_scr[n] = a
        b_scr[n] = mult * ig * xcn

        carry = hc_scr[n]
        for c in range(tm // SCAN_CHUNK):
            base = c * SCAN_CHUNK
            acum, hloc = [], []
            for j in range(SUBLANES):
                aj = a_scr.at[n][pl.ds(base + j, SUBLANES, stride=SUBLANES), :]
                bj = b_scr.at[n][pl.ds(base + j, SUBLANES, stride=SUBLANES), :]
                if j == 0:
                    acum.append(aj)
                    hloc.append(bj)
                else:
                    hloc.append(aj * hloc[-1] + bj)
                    acum.append(aj * acum[-1])
            p, e = acum[-1], hloc[-1]
            for d in (1, 2, 4):
                keep = sub >= d
                psh = pltpu.roll(p, d, 0)
                esh = pltpu.roll(e, d, 0)
                e = jnp.where(keep, p * esh + e, e)
                p = jnp.where(keep, p * psh, p)
            hend = e + p * carry
            cin = jnp.where(sub == 0, carry, pltpu.roll(hend, 1, 0))
            for j in range(SUBLANES):
                b_scr.at[n][pl.ds(base + j, SUBLANES, stride=SUBLANES), :] = (
                    hloc[j] + acum[j] * cin)
            carry = jnp.broadcast_to(hend[SUBLANES - 1:SUBLANES, :], (SUBLANES, LANES))
        hc_scr[n] = carry
        y = _gelu(yx_scr[i % 2, 0, :, k * LANES:(k + 1) * LANES])
        u_scr[:, lo:hi] = (b_scr[n] * y).astype(BF16)

    o_ref[...] += _dot(u_scr[:, out_split:], wo_ref[out_split:, :])


def _lru_layer(h, g, w_in, conv_w, conv_b, gate_w, gate_b, a_param, w_out, *, tm=512):
    B, S, D = h.shape
    W = w_out.shape[0]
    nblk = W // LANES
    assert gate_w.shape == (2, nblk, LANES, LANES) and S % tm == 0 and tm % SCAN_CHUNK == 0
    wy = w_in[:, :W].astype(BF16)
    wx = w_in[:, W:].astype(BF16)
    const = lambda shape: pl.BlockSpec(shape, lambda b, t: (0,) * len(shape))
    return pl.pallas_call(
        functools.partial(_lru_kernel, tm=tm, nblk=nblk),
        grid=(B, S // tm),
        in_specs=[
            pl.BlockSpec((None, tm, D), lambda b, t: (b, t, 0)),
            const((1, D)), const((D, W)), const((D, W)), const((LRU_CONV, W)), const((1, W)),
            const((2, nblk, LANES, LANES)), const((2, W)), const((1, W)), const((W, D)),
        ],
        out_specs=pl.BlockSpec((None, tm, D), lambda b, t: (b, t, 0)),
        out_shape=jax.ShapeDtypeStruct((B, S, D), F32),
        scratch_shapes=[
            pltpu.VMEM((SUBLANES, W), F32),
            pltpu.VMEM((nblk, tm, LANES), F32),
            pltpu.VMEM((nblk, tm, LANES), F32),
            pltpu.VMEM((nblk, SUBLANES, LANES), F32),
            pltpu.VMEM((tm, W), BF16),
            pltpu.VMEM((2, 2, tm, PROJ_W), F32),
        ],
        compiler_params=_params(("arbitrary", "arbitrary")),
        name="lru_layer",
    )(h, g.reshape(1, D), wy, wx, conv_w, conv_b.reshape(1, W), gate_w.astype(BF16), gate_b,
      a_param.reshape(1, W), w_out.astype(BF16))


def _ffn_kernel(*refs, tm, final_norm, has_mix):
    if has_mix:
        mix_ref, wmix_ref, *refs = refs
    (x_ref, g_ref, wa_ref, wb_ref, cw_ref, cb_ref, wo_ref, fg_ref, o_ref,
     xn_scr, acc_scr, a_scr, halo_scr) = refs
    t = pl.program_id(1)
    f = pl.program_id(2)
    nf = pl.num_programs(2)

    @pl.when(f == 0)
    def _():
        x = x_ref[...]
        if has_mix:
            x = x + _dot(mix_ref[...], wmix_ref[...])
        xn_scr[...] = _rmsnorm(x, g_ref[...]).astype(BF16)
        acc_scr[...] = x

    @pl.when(t == 0)
    def _():
        halo_scr[f] = jnp.zeros(halo_scr.shape[1:], F32)

    xn = xn_scr[...]
    a = _dot(xn, wa_ref[...])
    b = _dot(xn, wb_ref[...])
    a_scr[0:SUBLANES, :] = halo_scr[f]
    a_scr[SUBLANES:SUBLANES + tm, :] = a
    cw = cw_ref[...]
    ac = (cw[2:3, :] * a + cw[1:2, :] * a_scr[7:7 + tm, :] + cw[0:1, :] * a_scr[6:6 + tm, :]
          + cb_ref[...])
    halo_scr[f] = a_scr[tm:tm + SUBLANES, :]
    hmid = (_gelu(ac) * b).astype(BF16)
    acc_scr[...] += _dot(hmid, wo_ref[...])

    @pl.when(f == nf - 1)
    def _():
        out = acc_scr[...]
        if final_norm:
            out = _rmsnorm(out, fg_ref[...])
        o_ref[...] = out


def _ffn_layer(h, g, w_in, conv_w, conv_b, w_out, final_g, *, final_norm, mix=None, wmix=None,
               tm=1024, tf=512):
    B, S, D = h.shape
    F = w_out.shape[0]
    assert S % tm == 0 and F % tf == 0
    nf = F // tf
    wa = w_in[:, :F].astype(BF16)
    wb = w_in[:, F:].astype(BF16)
    has_mix = mix is not None
    mix_specs, mix_args = [], []
    if has_mix:
        K = mix.shape[2]
        mix_specs = [pl.BlockSpec((None, tm, K), lambda b, t, f: (b, t, 0)),
                     pl.BlockSpec((K, D), lambda b, t, f: (0, 0))]
        mix_args = [mix, wmix]
    return pl.pallas_call(
        functools.partial(_ffn_kernel, tm=tm, final_norm=final_norm, has_mix=has_mix),
        grid=(B, S // tm, nf),
        in_specs=mix_specs + [
            pl.BlockSpec((None, tm, D), lambda b, t, f: (b, t, 0)),
            pl.BlockSpec((1, D), lambda b, t, f: (0, 0)),
            pl.BlockSpec((D, tf), lambda b, t, f: (0, f)),
            pl.BlockSpec((D, tf), lambda b, t, f: (0, f)),
            pl.BlockSpec((FFN_CONV, tf), lambda b, t, f: (0, f)),
            pl.BlockSpec((1, tf), lambda b, t, f: (0, f)),
            pl.BlockSpec((tf, D), lambda b, t, f: (f, 0)),
            pl.BlockSpec((1, D), lambda b, t, f: (0, 0)),
        ],
        out_specs=pl.BlockSpec((None, tm, D), lambda b, t, f: (b, t, 0)),
        out_shape=jax.ShapeDtypeStruct((B, S, D), F32),
        scratch_shapes=[
            pltpu.VMEM((tm, D), BF16),
            pltpu.VMEM((tm, D), F32),
            pltpu.VMEM((tm + SUBLANES, tf), F32),
            pltpu.VMEM((nf, SUBLANES, tf), F32),
        ],
        compiler_params=_params(("arbitrary", "arbitrary", "arbitrary")),
        name="ffn_layer",
    )(*mix_args, h, g.reshape(1, D), wa, wb, conv_w, conv_b.reshape(1, F), w_out.astype(BF16),
      final_g.reshape(1, D))


def _nsa_proj_kernel(x_ref, g_ref, wq_ref, wkc_ref, wkv_ref, wg_ref, q_ref, t2_ref, kv_ref, gt_ref,
                     kc_scr):
    xn = _rmsnorm(x_ref[...], g_ref[...]).astype(BF16)
    q_ref[...] = (_dot(xn, wq_ref[...]) * (HEAD_DIM ** -0.5 * LOG2E)).astype(BF16)
    kc = _dot(xn, wkc_ref[...])
    nrow = t2_ref.shape[1]
    for j in range(kc_scr.shape[0]):
        kc_scr[j] = kc[:, j * LANES:(j + 1) * LANES]
        for i in range(CMP_STRIDE):
            t2_ref[j, :, i * LANES:(i + 1) * LANES] = (
                kc_scr.at[j][pl.ds(i, nrow, stride=CMP_STRIDE), :].astype(BF16))
    kv_ref[...] = _dot(xn, wkv_ref[...]).astype(BF16)
    gt_ref[...] = _sigmoid(_dot(xn, wg_ref[...]))


def _nsa_proj(h, g, wq, wkc, wkv, wg, *, tm=1024):
    B, S, D = h.shape
    nq, nkc, nkv, ng = wq.shape[1], wkc.shape[1], wkv.shape[1], wg.shape[1]
    nslab = nkc // LANES
    const = lambda shape: pl.BlockSpec(shape, lambda b, t: (0,) * len(shape))
    row = lambda n: pl.BlockSpec((None, tm, n), lambda b, t: (b, t, 0))
    return pl.pallas_call(
        _nsa_proj_kernel,
        grid=(B, S // tm),
        in_specs=[row(D), const((1, D)), const((D, nq)), const((D, nkc)), const((D, nkv)),
                  const((D, ng))],
        out_specs=[row(nq),
                   pl.BlockSpec((None, nslab, tm // CMP_STRIDE, CMP_STRIDE * LANES),
                                lambda b, t: (b, 0, t, 0)),
                   row(nkv), row(ng)],
        out_shape=[jax.ShapeDtypeStruct((B, S, nq), BF16),
                   jax.ShapeDtypeStruct((B, nslab, S // CMP_STRIDE, CMP_STRIDE * LANES), BF16),
                   jax.ShapeDtypeStruct((B, S, nkv), BF16),
                   jax.ShapeDtypeStruct((B, S, ng), F32)],
        scratch_shapes=[pltpu.VMEM((nslab, tm, LANES), F32)],
        compiler_params=_params(("arbitrary", "arbitrary")),
        name="nsa_proj",
    )(h, g.reshape(1, D), wq, wkc, wkv, wg)


def _cmp_kernel(t2_ref, pos_ref, wt_ref, wb_ref, b1_ref, w2_ref, o_ref):
    nb, nrow, K = t2_ref.shape
    t2 = t2_ref[...].reshape(nb * nrow, K)
    w_top = wt_ref[...]
    w_bot = wb_ref[...]
    pos = pos_ref[...].astype(BF16)
    bias = _dot(pos[:, :K], w_top) + _dot(pos[:, K:], w_bot) + b1_ref[...]
    hid = _dot(t2, w_top) + pltpu.roll(_dot(t2, w_bot), nb * nrow - 1, 0) + bias[0:1, :]
    out = _dot(_gelu(hid).astype(BF16), w2_ref[...])
    o_ref[...] = out.reshape(nb, nrow, out.shape[1]).astype(o_ref.dtype)


def _nsa_compress(t2, pos, w_top, w_bot, b1, w2, *, nb=4):
    B, nslab, nrow, K = t2.shape
    npair = nslab // 2
    H2 = w_top.shape[2]
    nb = math.gcd(nb, B)
    per_kv = lambda shape: pl.BlockSpec((None,) + shape, lambda s, p, b: (s,) + (0,) * len(shape))
    return pl.pallas_call(
        _cmp_kernel,
        grid=(2, npair, B // nb),
        in_specs=[
            pl.BlockSpec((nb, None, nrow, K), lambda s, p, b: (b, s * npair + p, 0, 0)),
            per_kv((SUBLANES, 2 * K)), per_kv((K, H2)), per_kv((K, H2)), per_kv((1, H2)),
            per_kv((H2, LANES)),
        ],
        out_specs=pl.BlockSpec((nb, None, nrow, LANES), lambda s, p, b: (b, s, 0, p)),
        out_shape=jax.ShapeDtypeStruct((B, 2, nrow, npair * LANES), BF16),
        compiler_params=_params(("arbitrary", "arbitrary", "arbitrary")),
        name="nsa_compress",
    )(t2, pos, w_top, w_bot, b1, w2)


NSLOT = 2 * GROUP
LOG2E = 1.4426950408889634
NPIECE = 4
SEL_LANE0 = 0
POS_LANE0 = 32
CMP_LANE0 = 40
QK_AHEAD = 3
SCORE_BUFS = 4
PROB_BUFS = 4
PV_BEHIND = 2
ONES_ROWS = 16


def _bf16_pieces(x, n):
    out, r = [], np.float64(x)
    for _ in range(n):
        p = np.float64(np.asarray(r, np.float32).astype(ml_dtypes.bfloat16).astype(np.float32))
        out.append(p)
        r = r - p
    return out


def _feature_base(gi):
    return HEAD_DIM if gi == 0 else 0


def _attn_tables(S, ncmp, tq):
    npair = N_KV // 2
    qf = np.zeros((npair, NSLOT, HEAD_DIM, tq), np.float32)
    for pair in range(npair):
        for slot in range(NSLOT):
            h = NSLOT * pair + slot
            for i, p in enumerate(_bf16_pieces(2.0 ** (-(h + 1) / 2.0) * LOG2E, NPIECE)):
                qf[pair, slot, POS_LANE0 + 2 * i] = SEL_BLOCK * p
                qf[pair, slot, POS_LANE0 + 2 * i + 1] = p
                qf[pair, slot, CMP_LANE0 + i] = CMP_STRIDE * p
    pos = np.arange(S)
    fs = np.zeros((2, S, LANES), np.float32)
    fw = np.zeros((2, S, LANES), np.float32)
    fc = np.zeros((2, ncmp, LANES), np.float32)
    for gi in range(2):
        f0 = _feature_base(gi)
        fs[gi, pos, f0 + SEL_LANE0 + pos // SEL_BLOCK] = 1.0
        for i in range(NPIECE):
            for f in (fs, fw):
                f[gi, :, f0 + POS_LANE0 + 2 * i] = pos // SEL_BLOCK
                f[gi, :, f0 + POS_LANE0 + 2 * i + 1] = pos % SEL_BLOCK
            fc[gi, :, f0 + CMP_LANE0 + i] = np.arange(ncmp)
    return (jnp.asarray(qf, BF16), jnp.asarray(fs, BF16), jnp.asarray(fw, BF16),
            jnp.asarray(fc, BF16))


def _attn_kernel(q_ref, ks_ref, vs_ref, kw_ref, vw_ref, kc_ref, vc_ref, gt_ref, ovt_ref, qf_ref,
                 fs_ref, fw_ref, fc_ref, o_ref, qaug_scr, kaug_scr, vt_scr, bias_scr, m_scr,
                 acc_scr, oc_scr, psum_scr, s_scr, p_scr, next_scr, *, tq, seq):
    tk = tq
    qt = pl.program_id(2)
    q0 = qt * tq
    ncmp = kc_ref.shape[0]
    nsel = seq // SEL_BLOCK
    ntile = seq // tk

    @pl.when(qt == 0)
    def _():
        own0 = lax.broadcasted_iota(jnp.int32, (tk, LANES), 1) < HEAD_DIM
        for br, k_ref, f_ref, v_ref in ((0, ks_ref, fs_ref, vs_ref), (1, kw_ref, fw_ref, vw_ref)):
            for j in range(ntile):
                rows = slice(j * tk, (j + 1) * tk)
                k2 = k_ref[rows, :]
                kaug_scr[br, 0, j] = jnp.where(own0, k2, f_ref[0, rows, :])
                kaug_scr[br, 1, j] = jnp.where(own0, f_ref[1, rows, :], k2)
                vt = v_ref[rows, :].T
                ones = jnp.ones((ONES_ROWS, tk), BF16)
                for gi in range(2):
                    vt_scr[br, j, gi] = jnp.concatenate(
                        [vt[gi * HEAD_DIM:(gi + 1) * HEAD_DIM, :], ones], axis=0)
        d = (lax.broadcasted_iota(jnp.int32, (tk, tq), 1)
             - lax.broadcasted_iota(jnp.int32, (tk, tq), 0))
        bias_scr[0] = jnp.where(d < 0, 0.0, NEG)
        bias_scr[1] = jnp.where(d >= 0, 0.0, NEG)

    m_scr[...] = jnp.full(m_scr.shape, NEG, F32)
    acc_scr[...] = jnp.zeros(acc_scr.shape, F32)

    for r in range(GROUP):
        qct = q_ref[:, r * LANES:(r + 1) * LANES].T
        qaug_scr[r] = jnp.concatenate([qct[0:HEAD_DIM, :], qf_ref[r]], axis=0)
        qaug_scr[GROUP + r] = jnp.concatenate([qf_ref[GROUP + r], qct[HEAD_DIM:, :]], axis=0)

    own0c = lax.broadcasted_iota(jnp.int32, (ncmp, LANES), 1) < HEAD_DIM
    kc2 = kc_ref[...]
    kcaug = [jnp.where(own0c, kc2, fc_ref[0]), jnp.where(own0c, fc_ref[1], kc2)]
    vct = vc_ref[...].T
    cend = lax.broadcasted_iota(jnp.int32, (ncmp, tq), 0) * CMP_STRIDE + (CMP_BLOCK - 1)
    cmp_bias = jnp.where(cend <= (q0 + lax.broadcasted_iota(jnp.int32, (ncmp, tq), 1)), 0.0, NEG)
    nbuf = s_scr.shape[0]
    assert NSLOT % nbuf == 0

    def issue_cmp_scores(slot):
        s_scr[slot % nbuf, 0:ncmp, :] = _dot(kcaug[slot // GROUP], qaug_scr[slot])

    def issue_scores(br, j, slot):
        s_scr[slot % nbuf] = _dot(kaug_scr[br, slot // GROUP, j], qaug_scr[slot])

    def cmp_values(slot):
        gi = slot // GROUP
        oc_scr[slot] = _dot(vct[gi * HEAD_DIM:(gi + 1) * HEAD_DIM, :], p_scr[slot % PROB_BUFS, 0:ncmp, :])

    win_lo = jnp.maximum(qt - 2, 0)
    for slot in range(QK_AHEAD):
        issue_cmp_scores(slot)
    psum = [None, None]
    for slot in range(NSLOT):
        gi = slot // GROUP
        sm = s_scr[slot % nbuf, 0:ncmp, :] + cmp_bias
        mx = jnp.max(sm, axis=0, keepdims=True)
        e = jnp.exp2(sm - mx)
        l = jnp.sum(e, axis=0, keepdims=True)
        p = e * jnp.where(mx > 0.5 * NEG, 1.0 / l, 0.0)
        psum[gi] = p if psum[gi] is None else psum[gi] + p
        p_scr[slot % PROB_BUFS, 0:ncmp, :] = p.astype(BF16)
        if slot + QK_AHEAD < NSLOT:
            issue_cmp_scores(slot + QK_AHEAD)
        else:
            issue_scores(1, win_lo, slot + QK_AHEAD - NSLOT)
        if slot >= 1:
            cmp_values(slot - 1)
    cmp_values(NSLOT - 1)
    for gi in range(2):
        psum_scr[gi] = psum[gi]

    def normalized(br, slot):
        return (acc_scr[br, slot, 0:HEAD_DIM, :]
                * (1.0 / acc_scr[br, slot, HEAD_DIM:HEAD_DIM + 1, :]))

    def tile(br, j, ahead, bias, side_work=None, nkeys=tk):
        def weighted_values(slot, alpha):
            acc_scr[br, slot] = alpha * acc_scr[br, slot] + _dot(
                vt_scr[br, j, slot // GROUP, :, 0:nkeys], p_scr[slot % PROB_BUFS, 0:nkeys, :])

        alphas = []
        for slot in range(NSLOT):
            s = s_scr[slot % nbuf, 0:nkeys, :]
            if bias is not None:
                s = s + bias
            m_prev = m_scr[br, slot]
            m_new = jnp.maximum(m_prev, jnp.max(s, axis=0, keepdims=True))
            alphas.append(jnp.exp2(m_prev - m_new))
            p_scr[slot % PROB_BUFS, 0:nkeys, :] = jnp.exp2(s - m_new).astype(BF16)
            m_scr[br, slot] = m_new
            if slot + QK_AHEAD < NSLOT:
                issue_scores(br, j, slot + QK_AHEAD)
            elif ahead is not None:
                issue_scores(ahead[0], ahead[1], slot + QK_AHEAD - NSLOT)
            if slot >= PV_BEHIND:
                weighted_values(slot - PV_BEHIND, alphas[slot - PV_BEHIND])
            if side_work is not None:
                side_work(slot)
        for slot in range(NSLOT - PV_BEHIND, NSLOT):
            weighted_values(slot, alphas[slot])

    jidx = lax.broadcasted_iota(jnp.int32, (nsel, tq), 0)
    ntop = min(N_SELECT, nsel)
    topk = {}

    def topk_start():
        ovt = ovt_ref[...]
        cur = (q0 + lax.broadcasted_iota(jnp.int32, (nsel, tq), 1)) // SEL_BLOCK
        forced = (jidx == 0) | (jidx == cur) | (jidx == cur - 1)
        future = jidx > cur
        for gi in range(2):
            ps = psum_scr[gi]
            p1 = ps.astype(BF16)
            r1 = ps - p1.astype(F32)
            p2 = r1.astype(BF16)
            p3 = (r1 - p2.astype(F32)).astype(BF16)
            imp = _dot(ovt, p1) + _dot(ovt, p2) + _dot(ovt, p3)
            topk[gi] = (jnp.where(forced, FORCED_BONUS, jnp.where(future, -1.0, imp)),
                        jnp.full((nsel, tq), NEG, F32))

    def topk_round():
        for gi in range(2):
            score, selneg = topk[gi]
            mx = jnp.max(score, axis=0, keepdims=True)
            cand = jnp.where(score == mx, jidx, nsel)
            first = jnp.min(cand, axis=0, keepdims=True)
            hit = jidx == first
            topk[gi] = (jnp.where(hit, -3e38, score), jnp.where(hit, 0.0, selneg))

    def topk_finish():
        for gi in range(2):
            f0 = _feature_base(gi) + SEL_LANE0
            selneg = topk[gi][1].astype(BF16)
            for r in range(GROUP):
                qaug_scr[gi * GROUP + r, f0:f0 + nsel, :] = selneg
        chosen = jnp.max(jnp.maximum(topk[0][1], topk[1][1]), axis=1, keepdims=True)
        blk_per_tile = tk // SEL_BLOCK
        assert blk_per_tile < 16 and ntile <= 8
        tile_of_blk = lax.broadcasted_iota(jnp.int32, (nsel, 1), 0) // blk_per_tile
        digits = jnp.sum(jnp.where(chosen == 0.0, jnp.left_shift(1, 4 * tile_of_blk), 0))
        nxt = qt
        count = jnp.int32(0)
        for j in reversed(range(ntile)):
            next_scr[j] = nxt
            hit = (jnp.right_shift(digits, 4 * j) & 15) != 0
            if j == 0:
                hit = True
            take = hit & (j < qt)
            nxt = jnp.where(take, j, nxt)
            count = count + take.astype(jnp.int32)
        topk['count'] = count
        topk['only_block0'] = (digits & 15) == 1

    def topk_side_work(slot):
        last = NSLOT - QK_AHEAD - 1
        if slot == 0:
            topk_start()
        for i in range(ntop):
            if i * (last + 1) // ntop == slot:
                topk_round()
        if slot == last:
            topk_finish()

    @pl.when(qt >= 2)
    def _():
        tile(1, qt - 2, (1, qt - 1), bias_scr[0])

    @pl.when(qt >= 1)
    def _():
        tile(1, qt - 1, (1, qt), None)

    tile(1, qt, (0, 0), bias_scr[1], topk_side_work)

    count = topk['count']
    after_first = next_scr[0]

    @pl.when((count > 0) & topk['only_block0'])
    def _():
        tile(0, 0, (0, after_first), None, nkeys=SEL_BLOCK)

    @pl.when((count > 0) & jnp.logical_not(topk['only_block0']))
    def _():
        tile(0, 0, (0, after_first), None)

    def chosen_tile(_, j):
        j_next = next_scr[j]
        tile(0, j, (0, j_next), None)
        return j_next

    lax.fori_loop(1, count, chosen_tile, after_first)
    tile(0, qt, None, bias_scr[1])

    gtt = gt_ref[...].T
    for r in range(GROUP):
        comb = []
        for gi in range(2):
            slot = gi * GROUP + r
            comb.append(gtt[slot:slot + 1, :] * oc_scr[slot]
                        + gtt[NSLOT + slot:NSLOT + slot + 1, :] * normalized(0, slot)
                        + gtt[2 * NSLOT + slot:2 * NSLOT + slot + 1, :] * normalized(1, slot))
        col = jnp.concatenate(comb, axis=0).astype(o_ref.dtype)
        o_ref[:, r * LANES:(r + 1) * LANES] = col.T


def _nsa_attention(q, kv, kvc, gates, *, tq=256):
    B, S, _ = q.shape
    npair = N_KV // 2
    ncmp = kvc.shape[2]
    nsel = S // SEL_BLOCK
    assert S % tq == 0 and WINDOW == 2 * tq and nsel <= SEL_LANE0 + POS_LANE0
    cstart = np.arange(ncmp) * CMP_STRIDE
    selj = np.arange(nsel)
    ovt = ((cstart[None, :] < (selj[:, None] + 1) * SEL_BLOCK)
           & (cstart[None, :] + CMP_BLOCK > selj[:, None] * SEL_BLOCK))
    ovt = jnp.asarray(ovt, BF16)
    qf, fs, fw, fc = _attn_tables(S, ncmp, tq)

    def kvspec(cb):
        return pl.BlockSpec((None, S, LANES), lambda b, p, t: (b, 0, cb + p))

    def cspec(s):
        return pl.BlockSpec((None, None, ncmp, LANES), lambda b, p, t: (b, s, 0, p))

    const = lambda shape: pl.BlockSpec(shape, lambda b, p, t: (0,) * len(shape))
    qw = GROUP * LANES
    ntile = S // tq
    return pl.pallas_call(
        functools.partial(_attn_kernel, tq=tq, seq=S),
        grid=(B, npair, S // tq),
        in_specs=[
            pl.BlockSpec((None, tq, qw), lambda b, p, t: (b, t, p)),
            kvspec(0), kvspec(2), kvspec(4), kvspec(6),
            cspec(0), cspec(1),
            pl.BlockSpec((None, tq, LANES), lambda b, p, t: (b, t, p)),
            const((nsel, ncmp)),
            pl.BlockSpec((None, NSLOT, HEAD_DIM, tq), lambda b, p, t: (p, 0, 0, 0)),
            const((2, S, LANES)), const((2, S, LANES)), const((2, ncmp, LANES)),
        ],
        out_specs=pl.BlockSpec((None, tq, qw), lambda b, p, t: (b, t, p)),
        out_shape=jax.ShapeDtypeStruct((B, S, npair * qw), BF16),
        scratch_shapes=[
            pltpu.VMEM((NSLOT, LANES, tq), BF16),
            pltpu.VMEM((2, 2, ntile, tq, LANES), BF16),
            pltpu.VMEM((2, ntile, 2, HEAD_DIM + ONES_ROWS, tq), BF16),
            pltpu.VMEM((2, tq, tq), F32),
            pltpu.VMEM((2, NSLOT, 1, tq), F32),
            pltpu.VMEM((2, NSLOT, HEAD_DIM + ONES_ROWS, tq), F32),
            pltpu.VMEM((NSLOT, HEAD_DIM, tq), F32),
            pltpu.VMEM((2, ncmp, tq), F32),
            pltpu.VMEM((SCORE_BUFS, tq, tq), F32),
            pltpu.VMEM((PROB_BUFS, tq, tq), BF16),
            pltpu.SMEM((ntile,), jnp.int32),
        ],
        compiler_params=_params(("arbitrary", "arbitrary", "arbitrary")),
        name="nsa_attention",
    )(q, kv, kv, kv, kv, kvc, kvc, gates, ovt, qf, fs, fw, fc)


def _head_pair_perm():
    perm = np.zeros(N_HEADS * HEAD_DIM, np.int32)
    d = np.arange(HEAD_DIM)
    for pair in range(N_KV // 2):
        for r in range(GROUP):
            for gi in range(2):
                h = (2 * pair + gi) * GROUP + r
                n0 = pair * GROUP * LANES + r * LANES + gi * HEAD_DIM
                perm[n0 + d] = h * HEAD_DIM + d
    return perm


def _nsa_layer(h, g, w_in, cmp_pos, cmp_w1, cmp_b1, cmp_w2, w_out):
    B, S, D = h.shape
    qcols = N_HEADS * HEAD_DIM
    kvcols = 3 * 2 * N_KV * HEAD_DIM
    ngate = 3 * N_HEADS
    perm = _head_pair_perm()
    ccols = 2 * N_KV * HEAD_DIM
    wq = w_in[:, :qcols][:, perm].astype(BF16)
    wkc = w_in[:, qcols:qcols + ccols].astype(BF16)
    wkv = w_in[:, qcols + ccols:qcols + kvcols].astype(BF16)
    wg_src = w_in[:, qcols + kvcols:]
    wg = jnp.zeros((D, (N_KV // 2) * LANES), F32)
    for pair in range(N_KV // 2):
        for br in range(3):
            src = br * N_HEADS + pair * NSLOT
            dst = pair * LANES + br * NSLOT
            wg = wg.at[:, dst:dst + NSLOT].set(wg_src[:, src:src + NSLOT])
    wg = wg.astype(BF16)

    q, t2, kv, gates = _nsa_proj(h, g, wq, wkc, wkv, wg)

    hid = cmp_w1.shape[2]
    eye = jnp.eye(2, dtype=BF16)
    w1r = cmp_w1.astype(BF16).reshape(2, 2, CMP_STRIDE, HEAD_DIM, hid)
    w1p = jnp.einsum('stidh,gk->stigdkh', w1r, eye).reshape(2, 2, CMP_STRIDE * LANES, 2 * hid)
    posr = cmp_pos.reshape(2, 2, CMP_STRIDE, 1, HEAD_DIM)
    posp = jnp.broadcast_to(posr, (2, 2, CMP_STRIDE, 2, HEAD_DIM)).reshape(2, 1, -1)
    posp = jnp.broadcast_to(posp, (2, SUBLANES, posp.shape[2]))
    b1p = jnp.tile(cmp_b1, (1, 2)).reshape(2, 1, 2 * hid)
    w2p = jnp.einsum('shd,gk->sghkd', cmp_w2.astype(BF16), eye).reshape(2, 2 * hid, LANES)
    kvc = _nsa_compress(t2, posp, w1p[:, 0], w1p[:, 1], b1p, w2p)

    return _nsa_attention(q, kv, kvc, gates), w_out[perm, :].astype(BF16)


def kernel(x, lru_norm_g, lru_w_in, lru_conv_w, lru_conv_b, lru_gate_w, lru_gate_b, lru_a_param,
           lru_w_out, nsa_norm_g, nsa_w_in, nsa_cmp_pos, nsa_cmp_w1, nsa_cmp_b1, nsa_cmp_w2,
           nsa_w_out, ffn_norm_g, ffn_w_in, ffn_conv_w, ffn_conv_b, ffn_w_out, final_norm_g):
    h = _lru_layer(x, lru_norm_g[0], lru_w_in[0], lru_conv_w[0], lru_conv_b[0], lru_gate_w[0],
                   lru_gate_b[0], lru_a_param[0], lru_w_out[0])
    h = _ffn_layer(h, ffn_norm_g[0], ffn_w_in[0], ffn_conv_w[0], ffn_conv_b[0], ffn_w_out[0],
                   final_norm_g, final_norm=False)
    o, wo = _nsa_layer(h, nsa_norm_g[0], nsa_w_in[0], nsa_cmp_pos[0], nsa_cmp_w1[0],
                       nsa_cmp_b1[0], nsa_cmp_w2[0], nsa_w_out[0])
    return _ffn_layer(h, ffn_norm_g[1], ffn_w_in[1], ffn_conv_w[1], ffn_conv_b[1], ffn_w_out[1],
                      final_norm_g, final_norm=True, mix=o, wmix=wo)
```

```python
import functools
import math

import ml_dtypes
import numpy as np
import jax
import jax.numpy as jnp
from jax import lax
from jax.experimental import pallas as pl
from jax.experimental.pallas import tpu as pltpu

BF16 = jnp.bfloat16
F32 = jnp.float32

EPS = 1e-6
LRU_BLOCK_W = 128
LRU_CONV = 4
LRU_C = 8.0
N_HEADS = 16
HEAD_DIM = 64
N_KV = 4
GROUP = N_HEADS // N_KV
CMP_BLOCK = 32
CMP_STRIDE = 16
SEL_BLOCK = 64
N_SELECT = 8
WINDOW = 512
FORCED_BONUS = 1e4
NEG = -1e30
FFN_CONV = 3

LANES = 128
SUBLANES = 8
VMEM_LIMIT = 56 * 1024 * 1024


def _gelu(x):
    c = math.sqrt(2.0 / math.pi)
    inner = x * (c + (c * 0.044715) * (x * x))
    return (0.5 * x) * (1.0 + jnp.tanh(inner))


def _sigmoid(x):
    return 0.5 * jnp.tanh(0.5 * x) + 0.5


def _rmsnorm(x, g):
    return x * lax.rsqrt(jnp.mean(x * x, axis=-1, keepdims=True) + EPS) * g


def _dot(a, b):
    return jnp.dot(a, b, preferred_element_type=F32)


def _params(sem):
    return pltpu.CompilerParams(dimension_semantics=sem, vmem_limit_bytes=VMEM_LIMIT)


SCAN_CHUNK = SUBLANES * SUBLANES
PROJ_W = 2 * LANES


def _lru_kernel(x_ref, g_ref, wy_ref, wx_ref, cw_ref, cb_ref, gw_ref, gb_ref, ap_ref, wo_ref,
                o_ref, tail_scr, a_scr, b_scr, hc_scr, u_scr, yx_scr, *, tm, nblk):
    t = pl.program_id(1)

    @pl.when(t == 0)
    def _():
        tail_scr[...] = jnp.zeros(tail_scr.shape, F32)
        hc_scr[...] = jnp.zeros(hc_scr.shape, F32)

    x = x_ref[...]
    xn = _rmsnorm(x, g_ref[...]).astype(BF16)
    cw = cw_ref[...]
    cb = cb_ref[...]

    z = -ap_ref[...]
    c8 = -LRU_C * (jnp.maximum(z, 0.0) + jnp.log1p(jnp.exp(-jnp.abs(z))))
    gb = gb_ref[...]

    blk_per_chunk = PROJ_W // LANES
    nchunk = nblk // blk_per_chunk

    def project(i, which):
        cols = slice(i * PROJ_W, (i + 1) * PROJ_W)
        w_ref = wy_ref if which == 0 else wx_ref
        yx_scr[i % 2, which] = _dot(xn, w_ref[:, cols])

    sub_w = lax.broadcasted_iota(jnp.int32, (SUBLANES, PROJ_W), 0)

    def conv(i):
        cols = slice(i * PROJ_W, (i + 1) * PROJ_W)
        xb = yx_scr[i % 2, 1]
        tail = tail_scr[:, cols]
        xc = cw[LRU_CONV - 1:LRU_CONV, cols] * xb + cb[:, cols]
        for d in range(1, LRU_CONV):
            rolled = pltpu.roll(xb, d, 0)
            top = jnp.where(sub_w < d, pltpu.roll(tail, d, 0), rolled[0:SUBLANES, :])
            shifted = jnp.concatenate([top, rolled[SUBLANES:, :]], axis=0)
            xc = xc + cw[LRU_CONV - 1 - d:LRU_CONV - d, cols] * shifted
        tail_scr[:, cols] = xb[tm - SUBLANES:tm, :]
        return xc

    sub = lax.broadcasted_iota(jnp.int32, (SUBLANES, LANES), 0)
    out_split = (nchunk - 1) * PROJ_W
    out_half = out_split // 2

    def out_rows_a():
        o_ref[...] = x + _dot(u_scr[:, 0:out_half], wo_ref[0:out_half, :])

    def out_rows_b():
        o_ref[...] += _dot(u_scr[:, out_half:out_split], wo_ref[out_half:out_split, :])

    queued = [functools.partial(project, i, which)
              for i in range(1, nchunk) for which in (0, 1)] + [out_rows_a, out_rows_b]
    assert len(queued) == nblk
    project(0, 0)
    project(0, 1)
    for n in range(nblk):
        i, k = divmod(n, blk_per_chunk)
        if k == 0:
            xc = conv(i)
        lo, hi = n * LANES, (n + 1) * LANES
        xcn = xc[:, k * LANES:(k + 1) * LANES]
        xcb = xcn.astype(BF16)
        zr = _dot(xcb, gw_ref[0, n])
        zi = _dot(xcb, gw_ref[1, n])
        if n < len(queued):
            queued[n]()
        r = _sigmoid(zr + gb[0:1, lo:hi])
        ig = _sigmoid(zi + gb[1:2, lo:hi])
        log_a = c8[:, lo:hi] * r
        a = jnp.exp(log_a)
        w = -jnp.tanh(log_a) * (a * a + 1.0)
        mult = jnp.where(w > 0.0, w * lax.rsqrt(w), 0.0)
        a_scr[n] = a
        b_scr[n] = mult * ig * xcn

        carry = hc_scr[n]
        for c in range(tm // SCAN_CHUNK):
            base = c * SCAN_CHUNK
            acum, hloc = [], []
            for j in range(SUBLANES):
                aj = a_scr.at[n][pl.ds(base + j, SUBLANES, stride=SUBLANES), :]
                bj = b_scr.at[n][pl.ds(base + j, SUBLANES, stride=SUBLANES), :]
                if j == 0:
                    acum.append(aj)
                    hloc.append(bj)
                else:
                    hloc.append(aj * hloc[-1] + bj)
                    acum.append(aj * acum[-1])
            p, e = acum[-1], hloc[-1]
            for d in (1, 2, 4):
                keep = sub >= d
                psh = pltpu.roll(p, d, 0)
                esh = pltpu.roll(e, d, 0)
                e = jnp.where(keep, p * esh + e, e)
                p = jnp.where(keep, p * psh, p)
            hend = e + p * carry
            cin = jnp.where(sub == 0, carry, pltpu.roll(hend, 1, 0))
            for j in range(SUBLANES):
                b_scr.at[n][pl.ds(base + j, SUBLANES, stride=SUBLANES), :] = (
                    hloc[j] + acum[j] * cin)
            carry = jnp.broadcast_to(hend[SUBLANES - 1:SUBLANES, :], (SUBLANES, LANES))
        hc_scr[n] = carry
        y = _gelu(yx_scr[i % 2, 0, :, k * LANES:(k + 1) * LANES])
        u_scr[:, lo:hi] = (b_scr[n] * y).astype(BF16)

    o_ref[...] += _dot(u_scr[:, out_split:], wo_ref[out_split:, :])


def _lru_layer(h, g, w_in, conv_w, conv_b, gate_w, gate_b, a_param, w_out, *, tm=1024):
    B, S, D = h.shape
    W = w_out.shape[0]
    nblk = W // LANES
    assert gate_w.shape == (2, nblk, LANES, LANES) and S % tm == 0 and tm % SCAN_CHUNK == 0
    wy = w_in[:, :W].astype(BF16)
    wx = w_in[:, W:].astype(BF16)
    const = lambda shape: pl.BlockSpec(shape, lambda b, t: (0,) * len(shape),
                                       pipeline_mode=pl.Buffered(1))
    return pl.pallas_call(
        functools.partial(_lru_kernel, tm=tm, nblk=nblk),
        grid=(B, S // tm),
        in_specs=[
            pl.BlockSpec((None, tm, D), lambda b, t: (b, t, 0)),
            const((1, D)), const((D, W)), const((D, W)), const((LRU_CONV, W)), const((1, W)),
            const((2, nblk, LANES, LANES)), const((2, W)), const((1, W)), const((W, D)),
        ],
        out_specs=pl.BlockSpec((None, tm, D), lambda b, t: (b, t, 0)),
        out_shape=jax.ShapeDtypeStruct((B, S, D), F32),
        scratch_shapes=[
            pltpu.VMEM((SUBLANES, W), F32),
            pltpu.VMEM((nblk, tm, LANES), F32),
            pltpu.VMEM((nblk, tm, LANES), F32),
            pltpu.VMEM((nblk, SUBLANES, LANES), F32),
            pltpu.VMEM((tm, W), BF16),
            pltpu.VMEM((2, 2, tm, PROJ_W), F32),
        ],
        compiler_params=_params(("arbitrary", "arbitrary")),
        name="lru_layer",
    )(h, g.reshape(1, D), wy, wx, conv_w, conv_b.reshape(1, W), gate_w.astype(BF16), gate_b,
      a_param.reshape(1, W), w_out.astype(BF16))


def _ffn_kernel(*refs, tm, final_norm, has_mix):
    if has_mix:
        mix_ref, wmix_ref, *refs = refs
    (x_ref, g_ref, wa_ref, wb_ref, cw_ref, cb_ref, wo_ref, fg_ref, o_ref,
     xn_scr, acc_scr, a_scr, halo_scr) = refs
    t = pl.program_id(1)
    f = pl.program_id(2)
    nf = pl.num_programs(2)

    @pl.when(f == 0)
    def _():
        x = x_ref[...]
        if has_mix:
            x = x + _dot(mix_ref[...], wmix_ref[...])
        xn_scr[...] = _rmsnorm(x, g_ref[...]).astype(BF16)
        acc_scr[...] = x

    @pl.when(t == 0)
    def _():
        halo_scr[f] = jnp.zeros(halo_scr.shape[1:], F32)

    xn = xn_scr[...]
    a = _dot(xn, wa_ref[...])
    b = _dot(xn, wb_ref[...])
    a_scr[0:SUBLANES, :] = halo_scr[f]
    a_scr[SUBLANES:SUBLANES + tm, :] = a
    cw = cw_ref[...]
    ac = (cw[2:3, :] * a + cw[1:2, :] * a_scr[7:7 + tm, :] + cw[0:1, :] * a_scr[6:6 + tm, :]
          + cb_ref[...])
    halo_scr[f] = a_scr[tm:tm + SUBLANES, :]
    hmid = (_gelu(ac) * b).astype(BF16)
    acc_scr[...] += _dot(hmid, wo_ref[...])

    @pl.when(f == nf - 1)
    def _():
        out = acc_scr[...]
        if final_norm:
            out = _rmsnorm(out, fg_ref[...])
        o_ref[...] = out


def _ffn_layer(h, g, w_in, conv_w, conv_b, w_out, final_g, *, final_norm, mix=None, wmix=None,
               tm=1024, tf=512):
    B, S, D = h.shape
    F = w_out.shape[0]
    assert S % tm == 0 and F % tf == 0
    nf = F // tf
    wa = w_in[:, :F].astype(BF16)
    wb = w_in[:, F:].astype(BF16)
    has_mix = mix is not None
    mix_specs, mix_args = [], []
    if has_mix:
        K = mix.shape[2]
        mix_specs = [pl.BlockSpec((None, tm, K), lambda b, t, f: (b, t, 0)),
                     pl.BlockSpec((K, D), lambda b, t, f: (0, 0))]
        mix_args = [mix, wmix]
    return pl.pallas_call(
        functools.partial(_ffn_kernel, tm=tm, final_norm=final_norm, has_mix=has_mix),
        grid=(B, S // tm, nf),
        in_specs=mix_specs + [
            pl.BlockSpec((None, tm, D), lambda b, t, f: (b, t, 0)),
            pl.BlockSpec((1, D), lambda b, t, f: (0, 0)),
            pl.BlockSpec((D, tf), lambda b, t, f: (0, f)),
            pl.BlockSpec((D, tf), lambda b, t, f: (0, f)),
            pl.BlockSpec((FFN_CONV, tf), lambda b, t, f: (0, f)),
            pl.BlockSpec((1, tf), lambda b, t, f: (0, f)),
            pl.BlockSpec((tf, D), lambda b, t, f: (f, 0)),
            pl.BlockSpec((1, D), lambda b, t, f: (0, 0)),
        ],
        out_specs=pl.BlockSpec((None, tm, D), lambda b, t, f: (b, t, 0)),
        out_shape=jax.ShapeDtypeStruct((B, S, D), F32),
        scratch_shapes=[
            pltpu.VMEM((tm, D), BF16),
            pltpu.VMEM((tm, D), F32),
            pltpu.VMEM((tm + SUBLANES, tf), F32),
            pltpu.VMEM((nf, SUBLANES, tf), F32),
        ],
        compiler_params=_params(("arbitrary", "arbitrary", "arbitrary")),
        name="ffn_layer",
    )(*mix_args, h, g.reshape(1, D), wa, wb, conv_w, conv_b.reshape(1, F), w_out.astype(BF16),
      final_g.reshape(1, D))


def _nsa_proj_kernel(x_ref, g_ref, wq_ref, wkc_ref, wkv_ref, wg_ref, q_ref, t2_ref, kv_ref, gt_ref,
                     kc_scr):
    xn = _rmsnorm(x_ref[...], g_ref[...]).astype(BF16)
    q_ref[...] = (_dot(xn, wq_ref[...]) * (HEAD_DIM ** -0.5 * LOG2E)).astype(BF16)
    kc = _dot(xn, wkc_ref[...])
    nrow = t2_ref.shape[1]
    for j in range(kc_scr.shape[0]):
        kc_scr[j] = kc[:, j * LANES:(j + 1) * LANES]
        for i in range(CMP_STRIDE):
            t2_ref[j, :, i * LANES:(i + 1) * LANES] = (
                kc_scr.at[j][pl.ds(i, nrow, stride=CMP_STRIDE), :].astype(BF16))
    kv_ref[...] = _dot(xn, wkv_ref[...]).astype(BF16)
    gt_ref[...] = _sigmoid(_dot(xn, wg_ref[...]))


def _nsa_proj(h, g, wq, wkc, wkv, wg, *, tm=1024):
    B, S, D = h.shape
    nq, nkc, nkv, ng = wq.shape[1], wkc.shape[1], wkv.shape[1], wg.shape[1]
    nslab = nkc // LANES
    const = lambda shape: pl.BlockSpec(shape, lambda b, t: (0,) * len(shape))
    row = lambda n: pl.BlockSpec((None, tm, n), lambda b, t: (b, t, 0))
    return pl.pallas_call(
        _nsa_proj_kernel,
        grid=(B, S // tm),
        in_specs=[row(D), const((1, D)), const((D, nq)), const((D, nkc)), const((D, nkv)),
                  const((D, ng))],
        out_specs=[row(nq),
                   pl.BlockSpec((None, nslab, tm // CMP_STRIDE, CMP_STRIDE * LANES),
                                lambda b, t: (b, 0, t, 0)),
                   row(nkv), row(ng)],
        out_shape=[jax.ShapeDtypeStruct((B, S, nq), BF16),
                   jax.ShapeDtypeStruct((B, nslab, S // CMP_STRIDE, CMP_STRIDE * LANES), BF16),
                   jax.ShapeDtypeStruct((B, S, nkv), BF16),
                   jax.ShapeDtypeStruct((B, S, ng), F32)],
        scratch_shapes=[pltpu.VMEM((nslab, tm, LANES), F32)],
        compiler_params=_params(("arbitrary", "arbitrary")),
        name="nsa_proj",
    )(h, g.reshape(1, D), wq, wkc, wkv, wg)


def _cmp_kernel(t2_ref, pos_ref, wt_ref, wb_ref, b1_ref, w2_ref, o_ref):
    nb, nrow, K = t2_ref.shape
    t2 = t2_ref[...].reshape(nb * nrow, K)
    w_top = wt_ref[...]
    w_bot = wb_ref[...]
    pos = pos_ref[...].astype(BF16)
    bias = _dot(pos[:, :K], w_top) + _dot(pos[:, K:], w_bot) + b1_ref[...]
    hid = _dot(t2, w_top) + pltpu.roll(_dot(t2, w_bot), nb * nrow - 1, 0) + bias[0:1, :]
    out = _dot(_gelu(hid).astype(BF16), w2_ref[...])
    o_ref[...] = out.reshape(nb, nrow, out.shape[1]).astype(o_ref.dtype)


def _nsa_compress(t2, pos, w_top, w_bot, b1, w2, *, nb=4):
    B, nslab, nrow, K = t2.shape
    npair = nslab // 2
    H2 = w_top.shape[2]
    nb = math.gcd(nb, B)
    per_kv = lambda shape: pl.BlockSpec((None,) + shape, lambda s, p, b: (s,) + (0,) * len(shape))
    return pl.pallas_call(
        _cmp_kernel,
        grid=(2, npair, B // nb),
        in_specs=[
            pl.BlockSpec((nb, None, nrow, K), lambda s, p, b: (b, s * npair + p, 0, 0)),
            per_kv((SUBLANES, 2 * K)), per_kv((K, H2)), per_kv((K, H2)), per_kv((1, H2)),
            per_kv((H2, LANES)),
        ],
        out_specs=pl.BlockSpec((nb, None, nrow, LANES), lambda s, p, b: (b, s, 0, p)),
        out_shape=jax.ShapeDtypeStruct((B, 2, nrow, npair * LANES), BF16),
        compiler_params=_params(("arbitrary", "arbitrary", "arbitrary")),
        name="nsa_compress",
    )(t2, pos, w_top, w_bot, b1, w2)


NSLOT = 2 * GROUP
LOG2E = 1.4426950408889634
NPIECE = 4
SEL_LANE0 = 0
POS_LANE0 = 32
CMP_LANE0 = 40
QK_AHEAD = 3
SCORE_BUFS = 4
PROB_BUFS = 4
PV_BEHIND = 2
ONES_ROWS = 16


def _bf16_pieces(x, n):
    out, r = [], np.float64(x)
    for _ in range(n):
        p = np.float64(np.asarray(r, np.float32).astype(ml_dtypes.bfloat16).astype(np.float32))
        out.append(p)
        r = r - p
    return out


def _feature_base(gi):
    return HEAD_DIM if gi == 0 else 0


def _attn_tables(S, ncmp, tq):
    npair = N_KV // 2
    qf = np.zeros((npair, NSLOT, HEAD_DIM, tq), np.float32)
    for pair in range(npair):
        for slot in range(NSLOT):
            h = NSLOT * pair + slot
            for i, p in enumerate(_bf16_pieces(2.0 ** (-(h + 1) / 2.0) * LOG2E, NPIECE)):
                qf[pair, slot, POS_LANE0 + 2 * i] = SEL_BLOCK * p
                qf[pair, slot, POS_LANE0 + 2 * i + 1] = p
                qf[pair, slot, CMP_LANE0 + i] = CMP_STRIDE * p
    pos = np.arange(S)
    fs = np.zeros((2, S, LANES), np.float32)
    fw = np.zeros((2, S, LANES), np.float32)
    fc = np.zeros((2, ncmp, LANES), np.float32)
    for gi in range(2):
        f0 = _feature_base(gi)
        fs[gi, pos, f0 + SEL_LANE0 + pos // SEL_BLOCK] = 1.0
        for i in range(NPIECE):
            for f in (fs, fw):
                f[gi, :, f0 + POS_LANE0 + 2 * i] = pos // SEL_BLOCK
                f[gi, :, f0 + POS_LANE0 + 2 * i + 1] = pos % SEL_BLOCK
            fc[gi, :, f0 + CMP_LANE0 + i] = np.arange(ncmp)
    return (jnp.asarray(qf, BF16), jnp.asarray(fs, BF16), jnp.asarray(fw, BF16),
            jnp.asarray(fc, BF16))


def _attn_kernel(q_ref, ks_ref, vs_ref, kw_ref, vw_ref, kc_ref, vc_ref, gt_ref, ovt_ref, qf_ref,
                 fs_ref, fw_ref, fc_ref, o_ref, qaug_scr, kaug_scr, vt_scr, bias_scr, m_scr,
                 acc_scr, oc_scr, psum_scr, s_scr, p_scr, next_scr, *, tq, seq):
    tk = tq
    qt = pl.program_id(2)
    q0 = qt * tq
    ncmp = kc_ref.shape[0]
    nsel = seq // SEL_BLOCK
    ntile = seq // tk

    @pl.when(qt == 0)
    def _():
        own0 = lax.broadcasted_iota(jnp.int32, (tk, LANES), 1) < HEAD_DIM
        for br, k_ref, f_ref, v_ref in ((0, ks_ref, fs_ref, vs_ref), (1, kw_ref, fw_ref, vw_ref)):
            for j in range(ntile):
                rows = slice(j * tk, (j + 1) * tk)
                k2 = k_ref[rows, :]
                kaug_scr[br, 0, j] = jnp.where(own0, k2, f_ref[0, rows, :])
                kaug_scr[br, 1, j] = jnp.where(own0, f_ref[1, rows, :], k2)
                vt = v_ref[rows, :].T
                ones = jnp.ones((ONES_ROWS, tk), BF16)
                for gi in range(2):
                    vt_scr[br, j, gi] = jnp.concatenate(
                        [vt[gi * HEAD_DIM:(gi + 1) * HEAD_DIM, :], ones], axis=0)
        d = (lax.broadcasted_iota(jnp.int32, (tk, tq), 1)
             - lax.broadcasted_iota(jnp.int32, (tk, tq), 0))
        bias_scr[0] = jnp.where(d < 0, 0.0, NEG)
        bias_scr[1] = jnp.where(d >= 0, 0.0, NEG)

    m_scr[...] = jnp.full(m_scr.shape, NEG, F32)
    acc_scr[...] = jnp.zeros(acc_scr.shape, F32)

    for r in range(GROUP):
        qct = q_ref[:, r * LANES:(r + 1) * LANES].T
        qaug_scr[r] = jnp.concatenate([qct[0:HEAD_DIM, :], qf_ref[r]], axis=0)
        qaug_scr[GROUP + r] = jnp.concatenate([qf_ref[GROUP + r], qct[HEAD_DIM:, :]], axis=0)

    own0c = lax.broadcasted_iota(jnp.int32, (ncmp, LANES), 1) < HEAD_DIM
    kc2 = kc_ref[...]
    kcaug = [jnp.where(own0c, kc2, fc_ref[0]), jnp.where(own0c, fc_ref[1], kc2)]
    vct = vc_ref[...].T
    cend = lax.broadcasted_iota(jnp.int32, (ncmp, tq), 0) * CMP_STRIDE + (CMP_BLOCK - 1)
    cmp_bias = jnp.where(cend <= (q0 + lax.broadcasted_iota(jnp.int32, (ncmp, tq), 1)), 0.0, NEG)
    nbuf = s_scr.shape[0]
    assert NSLOT % nbuf == 0

    def issue_cmp_scores(slot):
        s_scr[slot % nbuf, 0:ncmp, :] = _dot(kcaug[slot // GROUP], qaug_scr[slot])

    def issue_scores(br, j, slot):
        s_scr[slot % nbuf] = _dot(kaug_scr[br, slot // GROUP, j], qaug_scr[slot])

    def cmp_values(slot):
        gi = slot // GROUP
        oc_scr[slot] = _dot(vct[gi * HEAD_DIM:(gi + 1) * HEAD_DIM, :], p_scr[slot % PROB_BUFS, 0:ncmp, :])

    win_lo = jnp.maximum(qt - 2, 0)
    for slot in range(QK_AHEAD):
        issue_cmp_scores(slot)
    psum = [None, None]
    for slot in range(NSLOT):
        gi = slot // GROUP
        sm = s_scr[slot % nbuf, 0:ncmp, :] + cmp_bias
        mx = jnp.max(sm, axis=0, keepdims=True)
        e = jnp.exp2(sm - mx)
        l = jnp.sum(e, axis=0, keepdims=True)
        p = e * jnp.where(mx > 0.5 * NEG, 1.0 / l, 0.0)
        psum[gi] = p if psum[gi] is None else psum[gi] + p
        p_scr[slot % PROB_BUFS, 0:ncmp, :] = p.astype(BF16)
        if slot + QK_AHEAD < NSLOT:
            issue_cmp_scores(slot + QK_AHEAD)
        else:
            issue_scores(1, win_lo, slot + QK_AHEAD - NSLOT)
        if slot >= 1:
            cmp_values(slot - 1)
    cmp_values(NSLOT - 1)
    for gi in range(2):
        psum_scr[gi] = psum[gi]

    def normalized(br, slot):
        return (acc_scr[br, slot, 0:HEAD_DIM, :]
                * (1.0 / acc_scr[br, slot, HEAD_DIM:HEAD_DIM + 1, :]))

    def tile(br, j, ahead, bias, side_work=None, nkeys=tk):
        def weighted_values(slot, alpha):
            acc_scr[br, slot] = alpha * acc_scr[br, slot] + _dot(
                vt_scr[br, j, slot // GROUP, :, 0:nkeys], p_scr[slot % PROB_BUFS, 0:nkeys, :])

        alphas = []
        for slot in range(NSLOT):
            s = s_scr[slot % nbuf, 0:nkeys, :]
            if bias is not None:
                s = s + bias
            m_prev = m_scr[br, slot]
            m_new = jnp.maximum(m_prev, jnp.max(s, axis=0, keepdims=True))
            alphas.append(jnp.exp2(m_prev - m_new))
            p_scr[slot % PROB_BUFS, 0:nkeys, :] = jnp.exp2(s - m_new).astype(BF16)
            m_scr[br, slot] = m_new
            if slot + QK_AHEAD < NSLOT:
                issue_scores(br, j, slot + QK_AHEAD)
            elif ahead is not None:
                issue_scores(ahead[0], ahead[1], slot + QK_AHEAD - NSLOT)
            if slot >= PV_BEHIND:
                weighted_values(slot - PV_BEHIND, alphas[slot - PV_BEHIND])
            if side_work is not None:
                side_work(slot)
        for slot in range(NSLOT - PV_BEHIND, NSLOT):
            weighted_values(slot, alphas[slot])

    jidx = lax.broadcasted_iota(jnp.int32, (nsel, tq), 0)
    ntop = min(N_SELECT, nsel)
    topk = {}

    def topk_start():
        ovt = ovt_ref[...]
        cur = (q0 + lax.broadcasted_iota(jnp.int32, (nsel, tq), 1)) // SEL_BLOCK
        forced = (jidx == 0) | (jidx == cur) | (jidx == cur - 1)
        future = jidx > cur
        for gi in range(2):
            ps = psum_scr[gi]
            p1 = ps.astype(BF16)
            r1 = ps - p1.astype(F32)
            p2 = r1.astype(BF16)
            p3 = (r1 - p2.astype(F32)).astype(BF16)
            imp = _dot(ovt, p1) + _dot(ovt, p2) + _dot(ovt, p3)
            topk[gi] = (jnp.where(forced, FORCED_BONUS, jnp.where(future, -1.0, imp)),
                        jnp.full((nsel, tq), NEG, F32))

    def topk_round():
        for gi in range(2):
            score, selneg = topk[gi]
            mx = jnp.max(score, axis=0, keepdims=True)
            cand = jnp.where(score == mx, jidx, nsel)
            first = jnp.min(cand, axis=0, keepdims=True)
            hit = jidx == first
            topk[gi] = (jnp.where(hit, -3e38, score), jnp.where(hit, 0.0, selneg))

    def topk_finish():
        for gi in range(2):
            f0 = _feature_base(gi) + SEL_LANE0
            selneg = topk[gi][1].astype(BF16)
            for r in range(GROUP):
                qaug_scr[gi * GROUP + r, f0:f0 + nsel, :] = selneg
        chosen = jnp.max(jnp.maximum(topk[0][1], topk[1][1]), axis=1, keepdims=True)
        blk_per_tile = tk // SEL_BLOCK
        assert blk_per_tile < 16 and ntile <= 8
        tile_of_blk = lax.broadcasted_iota(jnp.int32, (nsel, 1), 0) // blk_per_tile
        digits = jnp.sum(jnp.where(chosen == 0.0, jnp.left_shift(1, 4 * tile_of_blk), 0))
        nxt = qt
        count = jnp.int32(0)
        for j in reversed(range(ntile)):
            next_scr[j] = nxt
            hit = (jnp.right_shift(digits, 4 * j) & 15) != 0
            if j == 0:
                hit = True
            take = hit & (j < qt)
            nxt = jnp.where(take, j, nxt)
            count = count + take.astype(jnp.int32)
        topk['count'] = count
        topk['only_block0'] = (digits & 15) == 1

    def topk_side_work(slot):
        last = NSLOT - QK_AHEAD - 1
        if slot == 0:
            topk_start()
        for i in range(ntop):
            if i * (last + 1) // ntop == slot:
                topk_round()
        if slot == last:
            topk_finish()

    @pl.when(qt >= 2)
    def _():
        tile(1, qt - 2, (1, qt - 1), bias_scr[0])

    @pl.when(qt >= 1)
    def _():
        tile(1, qt - 1, (1, qt), None)

    tile(1, qt, (0, 0), bias_scr[1], topk_side_work)

    count = topk['count']
    after_first = next_scr[0]

    @pl.when((count > 0) & topk['only_block0'])
    def _():
        tile(0, 0, (0, after_first), None, nkeys=SEL_BLOCK)

    @pl.when((count > 0) & jnp.logical_not(topk['only_block0']))
    def _():
        tile(0, 0, (0, after_first), None)

    def chosen_tile(_, j):
        j_next = next_scr[j]
        tile(0, j, (0, j_next), None)
        return j_next

    lax.fori_loop(1, count, chosen_tile, after_first)
    tile(0, qt, None, bias_scr[1])

    gtt = gt_ref[...].T
    for r in range(GROUP):
        comb = []
        for gi in range(2):
            slot = gi * GROUP + r
            comb.append(gtt[slot:slot + 1, :] * oc_scr[slot]
                        + gtt[NSLOT + slot:NSLOT + slot + 1, :] * normalized(0, slot)
                        + gtt[2 * NSLOT + slot:2 * NSLOT + slot + 1, :] * normalized(1, slot))
        col = jnp.concatenate(comb, axis=0).astype(o_ref.dtype)
        o_ref[:, r * LANES:(r + 1) * LANES] = col.T


def _nsa_attention(q, kv, kvc, gates, *, tq=256):
    B, S, _ = q.shape
    npair = N_KV // 2
    ncmp = kvc.shape[2]
    nsel = S // SEL_BLOCK
    assert S % tq == 0 and WINDOW == 2 * tq and nsel <= SEL_LANE0 + POS_LANE0
    cstart = np.arange(ncmp) * CMP_STRIDE
    selj = np.arange(nsel)
    ovt = ((cstart[None, :] < (selj[:, None] + 1) * SEL_BLOCK)
           & (cstart[None, :] + CMP_BLOCK > selj[:, None] * SEL_BLOCK))
    ovt = jnp.asarray(ovt, BF16)
    qf, fs, fw, fc = _attn_tables(S, ncmp, tq)

    def kvspec(cb):
        return pl.BlockSpec((None, S, LANES), lambda b, p, t: (b, 0, cb + p))

    def cspec(s):
        return pl.BlockSpec((None, None, ncmp, LANES), lambda b, p, t: (b, s, 0, p))

    const = lambda shape: pl.BlockSpec(shape, lambda b, p, t: (0,) * len(shape))
    qw = GROUP * LANES
    ntile = S // tq
    return pl.pallas_call(
        functools.partial(_attn_kernel, tq=tq, seq=S),
        grid=(B, npair, S // tq),
        in_specs=[
            pl.BlockSpec((None, tq, qw), lambda b, p, t: (b, t, p)),
            kvspec(0), kvspec(2), kvspec(4), kvspec(6),
            cspec(0), cspec(1),
            pl.BlockSpec((None, tq, LANES), lambda b, p, t: (b, t, p)),
            const((nsel, ncmp)),
            pl.BlockSpec((None, NSLOT, HEAD_DIM, tq), lambda b, p, t: (p, 0, 0, 0)),
            const((2, S, LANES)), const((2, S, LANES)), const((2, ncmp, LANES)),
        ],
        out_specs=pl.BlockSpec((None, tq, qw), lambda b, p, t: (b, t, p)),
        out_shape=jax.ShapeDtypeStruct((B, S, npair * qw), BF16),
        scratch_shapes=[
            pltpu.VMEM((NSLOT, LANES, tq), BF16),
            pltpu.VMEM((2, 2, ntile, tq, LANES), BF16),
            pltpu.VMEM((2, ntile, 2, HEAD_DIM + ONES_ROWS, tq), BF16),
            pltpu.VMEM((2, tq, tq), F32),
            pltpu.VMEM((2, NSLOT, 1, tq), F32),
            pltpu.VMEM((2, NSLOT, HEAD_DIM + ONES_ROWS, tq), F32),
            pltpu.VMEM((NSLOT, HEAD_DIM, tq), F32),
            pltpu.VMEM((2, ncmp, tq), F32),
            pltpu.VMEM((SCORE_BUFS, tq, tq), F32),
            pltpu.VMEM((PROB_BUFS, tq, tq), BF16),
            pltpu.SMEM((ntile,), jnp.int32),
        ],
        compiler_params=_params(("arbitrary", "arbitrary", "arbitrary")),
        name="nsa_attention",
    )(q, kv, kv, kv, kv, kvc, kvc, gates, ovt, qf, fs, fw, fc)


def _head_pair_perm():
    perm = np.zeros(N_HEADS * HEAD_DIM, np.int32)
    d = np.arange(HEAD_DIM)
    for pair in range(N_KV // 2):
        for r in range(GROUP):
            for gi in range(2):
                h = (2 * pair + gi) * GROUP + r
                n0 = pair * GROUP * LANES + r * LANES + gi * HEAD_DIM
                perm[n0 + d] = h * HEAD_DIM + d
    return perm


def _nsa_layer(h, g, w_in, cmp_pos, cmp_w1, cmp_b1, cmp_w2, w_out):
    B, S, D = h.shape
    qcols = N_HEADS * HEAD_DIM
    kvcols = 3 * 2 * N_KV * HEAD_DIM
    ngate = 3 * N_HEADS
    perm = _head_pair_perm()
    ccols = 2 * N_KV * HEAD_DIM
    wq = w_in[:, :qcols][:, perm].astype(BF16)
    wkc = w_in[:, qcols:qcols + ccols].astype(BF16)
    wkv = w_in[:, qcols + ccols:qcols + kvcols].astype(BF16)
    wg_src = w_in[:, qcols + kvcols:]
    wg = jnp.zeros((D, (N_KV // 2) * LANES), F32)
    for pair in range(N_KV // 2):
        for br in range(3):
            src = br * N_HEADS + pair * NSLOT
            dst = pair * LANES + br * NSLOT
            wg = wg.at[:, dst:dst + NSLOT].set(wg_src[:, src:src + NSLOT])
    wg = wg.astype(BF16)

    q, t2, kv, gates = _nsa_proj(h, g, wq, wkc, wkv, wg)

    hid = cmp_w1.shape[2]
    eye = jnp.eye(2, dtype=BF16)
    w1r = cmp_w1.astype(BF16).reshape(2, 2, CMP_STRIDE, HEAD_DIM, hid)
    w1p = jnp.einsum('stidh,gk->stigdkh', w1r, eye).reshape(2, 2, CMP_STRIDE * LANES, 2 * hid)
    posr = cmp_pos.reshape(2, 2, CMP_STRIDE, 1, HEAD_DIM)
    posp = jnp.broadcast_to(posr, (2, 2, CMP_STRIDE, 2, HEAD_DIM)).reshape(2, 1, -1)
    posp = jnp.broadcast_to(posp, (2, SUBLANES, posp.shape[2]))
    b1p = jnp.tile(cmp_b1, (1, 2)).reshape(2, 1, 2 * hid)
    w2p = jnp.einsum('shd,gk->sghkd', cmp_w2.astype(BF16), eye).reshape(2, 2 * hid, LANES)
    kvc = _nsa_compress(t2, posp, w1p[:, 0], w1p[:, 1], b1p, w2p)

    return _nsa_attention(q, kv, kvc, gates), w_out[perm, :].astype(BF16)


def kernel(x, lru_norm_g, lru_w_in, lru_conv_w, lru_conv_b, lru_gate_w, lru_gate_b, lru_a_param,
           lru_w_out, nsa_norm_g, nsa_w_in, nsa_cmp_pos, nsa_cmp_w1, nsa_cmp_b1, nsa_cmp_w2,
           nsa_w_out, ffn_norm_g, ffn_w_in, ffn_conv_w, ffn_conv_b, ffn_w_out, final_norm_g):
    h = _lru_layer(x, lru_norm_g[0], lru_w_in[0], lru_conv_w[0], lru_conv_b[0], lru_gate_w[0],
                   lru_gate_b[0], lru_a_param[0], lru_w_out[0])
    h = _ffn_layer(h, ffn_norm_g[0], ffn_w_in[0], ffn_conv_w[0], ffn_conv_b[0], ffn_w_out[0],
                   final_norm_g, final_norm=False)
    o, wo = _nsa_layer(h, nsa_norm_g[0], nsa_w_in[0], nsa_cmp_pos[0], nsa_cmp_w1[0],
                       nsa_cmp_b1[0], nsa_cmp_w2[0], nsa_w_out[0])
    return _ffn_layer(h, ffn_norm_g[1], ffn_w_in[1], ffn_conv_w[1], ffn_conv_b[1], ffn_w_out[1],
                      final_norm_g, final_norm=True, mix=o, wmix=wo)
```
